```python
import jax, jax.numpy as jnp
from jax import lax
import numpy as np


D_MODEL = 4096
BATCH = 2
SEQ = 8192
DEPTH = 2

HEAD_DIM = 128
N_ATTN_HEADS = D_MODEL // HEAD_DIM
NSA_HEADS = N_ATTN_HEADS // 2
NSA_KV_GROUPS = 4
NSA_GROUP_SIZE = NSA_HEADS // NSA_KV_GROUPS
NSA_CMP_STRIDE = 16
NSA_CMP_LEN = 2 * NSA_CMP_STRIDE
NSA_SEL_LEN = 64
NSA_N_SEL = 16
NSA_WINDOW = 512
NSA_QBLK = 32
MOBA_HEADS = N_ATTN_HEADS - NSA_HEADS
MOBA_BLOCK = 256
MOBA_TOPK = 3
MOBA_QBLK = 16
ATT_WIDTH = (NSA_HEADS + MOBA_HEADS) * HEAD_DIM

CONV_WIDTH = 3
CONV_CHANNELS = D_MODEL // 2
GLA_HEADS = 16
GLA_DK = 64
GLA_DV = 128
GLA_GATE_RANK = 16
GLA_GATE_TAU = 16.0
GLA_CHUNK = 64
MIX_WIDTH = CONV_CHANNELS + GLA_HEADS * GLA_DV

D_FF = 256 * ((8 * D_MODEL // 3 + 255) // 256)
ALPHA = (2 * DEPTH) ** 0.25
BETA = (8 * DEPTH) ** -0.25
LN_EPS = 1e-5
NORM_EPS = 1e-6
NEG_INF = -1e30
SEL_FORCE = 1e4

NSA_KV_DIM = NSA_KV_GROUPS * HEAD_DIM
L0_SPLITS = (NSA_HEADS * HEAD_DIM,) + (NSA_KV_DIM,) * 6 + (NSA_HEADS * 3,) + (MOBA_HEADS * HEAD_DIM,) * 3
L0_IN_DIM = sum(L0_SPLITS)
L1_SPLITS = (CONV_CHANNELS,) * 3 + (GLA_HEADS * GLA_DK,) * 2 + (GLA_HEADS * GLA_DV,) * 2 + (GLA_GATE_RANK,)
L1_IN_DIM = sum(L1_SPLITS)
N_EVEN = (DEPTH + 1) // 2
N_ODD = DEPTH // 2

kernel_name = 'hybrid_nsa_moba_shortconv_gla_macaron_deepnorm'


def _split(z, sizes):
    return jnp.split(z, np.cumsum(sizes)[:-1].tolist(), axis=-1)


def _layer_norm(x, g, b):
    xf = x.astype(jnp.float32)
    mu = jnp.mean(xf, axis=-1, keepdims=True)
    var = jnp.mean(jnp.square(xf - mu), axis=-1, keepdims=True)
    return ((xf - mu) * lax.rsqrt(var + LN_EPS) * g + b).astype(x.dtype)


def _swiglu(x, w_gate, w_up, w_down):
    return (jax.nn.silu(x @ w_gate) * (x @ w_up)) @ w_down


def _masked_softmax(s, mask):
    s = jnp.where(mask, s, NEG_INF)
    m = jnp.max(s, axis=-1, keepdims=True)
    p = jnp.where(mask, jnp.exp(s - m), 0.0)
    return p / jnp.maximum(jnp.sum(p, axis=-1, keepdims=True), 1e-30)


def _alibi_slopes(n):
    return jnp.exp2(-8.0 * jnp.arange(1, n + 1, dtype=jnp.float32) / n)


def _compress(k, pos, w1, w2):
    B, S, G, dh = k.shape
    ch = k.reshape(B, S // NSA_CMP_STRIDE, NSA_CMP_STRIDE, G, dh)
    blocks = jnp.concatenate([ch[:, :-1], ch[:, 1:]], axis=2) + pos[None, None, :, None, :]
    n_cmp = blocks.shape[1]
    flat = blocks.transpose(0, 1, 3, 2, 4).reshape(B, n_cmp, G, NSA_CMP_LEN * dh)
    return jax.nn.gelu(flat @ w1) @ w2


def _nsa(q, k_cmp, v_cmp, k_slc, v_slc, k_win, v_win, gates, slopes, pos_k, w1_k, w2_k, pos_v, w1_v, w2_v):
    B, S, G, R, dh = q.shape
    scale = dh ** -0.5
    f32 = jnp.float32
    kc = _compress(k_cmp, pos_k, w1_k, w2_k)
    vc = _compress(v_cmp, pos_v, w1_v, w2_v)
    n_cmp = kc.shape[1]
    cmp_start = jnp.arange(n_cmp) * NSA_CMP_STRIDE
    cmp_end = cmp_start + NSA_CMP_LEN - 1
    n_sel = S // NSA_SEL_LEN
    k_top = min(NSA_N_SEL, n_sel)
    sel_start = jnp.arange(n_sel) * NSA_SEL_LEN
    member = ((cmp_start[:, None] < sel_start[None, :] + NSA_SEL_LEN)
              & (cmp_start[:, None] + NSA_CMP_LEN > sel_start[None, :])).astype(f32)
    ks_blk = k_slc.reshape(B, n_sel, NSA_SEL_LEN, G, dh).transpose(0, 3, 1, 2, 4)
    vs_blk = v_slc.reshape(B, n_sel, NSA_SEL_LEN, G, dh).transpose(0, 3, 1, 2, 4)
    pad = ((0, 0), (NSA_WINDOW, 0), (0, 0), (0, 0))
    kw_pad = jnp.pad(k_win, pad)
    vw_pad = jnp.pad(v_win, pad)
    b_ix = jnp.arange(B)[:, None, None, None]
    g_ix = jnp.arange(G)[None, None, :, None]
    sel_ids = jnp.arange(n_sel)
    sl = slopes[None, None, :, :, None]
    T = NSA_QBLK

    def chunk(args):
        c, qc, gc = args
        t = c * T + jnp.arange(T)
        tf = t.astype(f32)
        s = jnp.einsum('btgrd,bngd->btgrn', qc, kc).astype(f32) * scale
        s = s - sl * (tf[:, None] - cmp_end[None, :].astype(f32))[None, :, None, None, :]
        p_cmp = _masked_softmax(s, (cmp_end[None, :] <= t[:, None])[None, :, None, None, :])
        o_cmp = jnp.einsum('btgrn,bngd->btgrd', p_cmp.astype(vc.dtype), vc)
        imp = jnp.einsum('btgn,nj->btgj', jnp.sum(p_cmp, axis=3), member)
        own = (t // NSA_SEL_LEN)[:, None]
        forced = (sel_ids == 0) | (sel_ids == own) | (sel_ids == own - 1)
        future = sel_ids > own
        imp = jnp.where(forced[None, :, None, :], imp + SEL_FORCE,
                        jnp.where(future[None, :, None, :], -1.0, imp))
        idx = lax.top_k(imp, k_top)[1]
        kg = ks_blk[b_ix, g_ix, idx].reshape(B, T, G, k_top * NSA_SEL_LEN, dh)
        vg = vs_blk[b_ix, g_ix, idx].reshape(B, T, G, k_top * NSA_SEL_LEN, dh)
        kpos = (idx[..., None] * NSA_SEL_LEN + jnp.arange(NSA_SEL_LEN)).reshape(B, T, G, 1, k_top * NSA_SEL_LEN)
        s = jnp.einsum('btgrd,btgkd->btgrk', qc, kg).astype(f32) * scale
        s = s - sl * (tf[None, :, None, None, None] - kpos.astype(f32))
        p = _masked_softmax(s, kpos <= t[None, :, None, None, None])
        o_slc = jnp.einsum('btgrk,btgkd->btgrd', p.astype(vg.dtype), vg)
        kw = lax.dynamic_slice_in_dim(kw_pad, c * T, NSA_WINDOW + T, axis=1)
        vw = lax.dynamic_slice_in_dim(vw_pad, c * T, NSA_WINDOW + T, axis=1)
        wpos = c * T - NSA_WINDOW + jnp.arange(NSA_WINDOW + T)
        wmask = (wpos[None, :] >= 0) & (wpos[None, :] <= t[:, None]) & (wpos[None, :] > t[:, None] - NSA_WINDOW)
        s = jnp.einsum('btgrd,bkgd->btgrk', qc, kw).astype(f32) * scale
        s = s - sl * (tf[:, None] - wpos[None, :].astype(f32))[None, :, None, None, :]
        p = _masked_softmax(s, wmask[None, :, None, None, :])
        o_win = jnp.einsum('btgrk,bkgd->btgrd', p.astype(vw.dtype), vw)
        return gc[..., 0:1] * o_cmp + gc[..., 1:2] * o_slc + gc[..., 2:3] * o_win

    n_q = S // T
    qch = q.reshape(B, n_q, T, G, R, dh).swapaxes(0, 1)
    gch = gates.reshape(B, n_q, T, G, R, 3).swapaxes(0, 1)
    out = lax.map(chunk, (jnp.arange(n_q), qch, gch))
    return out.swapaxes(0, 1).reshape(B, S, G * R * dh)


def _moba(q, k, v, slopes):
    B, S, H, dh = q.shape
    scale = dh ** -0.5
    f32 = jnp.float32
    n_blk = -(-S // MOBA_BLOCK)
    pad = ((0, 0), (0, n_blk * MOBA_BLOCK - S), (0, 0), (0, 0))
    k_blk = jnp.pad(k, pad).reshape(B, n_blk, MOBA_BLOCK, H, dh).transpose(0, 3, 1, 2, 4)
    v_blk = jnp.pad(v, pad).reshape(B, n_blk, MOBA_BLOCK, H, dh).transpose(0, 3, 1, 2, 4)
    k_mean = jnp.mean(k_blk, axis=3)
    n_top = min(MOBA_TOPK, n_blk - 1)
    n_gather = n_top + 1
    b_ix = jnp.arange(B)[:, None, None, None]
    h_ix = jnp.arange(H)[None, None, :, None]
    blk_ids = jnp.arange(n_blk)
    sl = slopes[None, None, :, None]
    T = MOBA_QBLK

    def chunk(args):
        c, qc = args
        t = c * T + jnp.arange(T)
        tf = t.astype(f32)
        own = jnp.broadcast_to((t // MOBA_BLOCK)[None, :, None, None], (B, T, H, 1))
        if n_top > 0:
            gate = jnp.einsum('bthd,bhnd->bthn', qc, k_mean).astype(f32)
            past = blk_ids < own
            gate = jnp.where(past, gate, NEG_INF)
            top = lax.top_k(gate, n_top)[1]
            idx = jnp.concatenate([top, own], axis=-1)
            valid = jnp.concatenate([top < own, jnp.ones_like(own, dtype=bool)], axis=-1)
        else:
            idx = own
            valid = jnp.ones_like(own, dtype=bool)
        kg = k_blk[b_ix, h_ix, idx].reshape(B, T, H, n_gather * MOBA_BLOCK, dh)
        vg = v_blk[b_ix, h_ix, idx].reshape(B, T, H, n_gather * MOBA_BLOCK, dh)
        kpos = (idx[..., None] * MOBA_BLOCK + jnp.arange(MOBA_BLOCK)).reshape(B, T, H, n_gather * MOBA_BLOCK)
        mask = jnp.repeat(valid, MOBA_BLOCK, axis=-1) & (kpos <= t[None, :, None, None])
        s = jnp.einsum('bthd,bthkd->bthk', qc, kg).astype(f32) * scale
        s = s - sl * (tf[None, :, None, None] - kpos.astype(f32))
        p = _masked_softmax(s, mask)
        return jnp.einsum('bthk,bthkd->bthd', p.astype(vg.dtype), vg)

    n_q = S // T
    qch = q.reshape(B, n_q, T, H, dh).swapaxes(0, 1)
    out = lax.map(chunk, (jnp.arange(n_q), qch))
    return out.swapaxes(0, 1).reshape(B, S, H * dh)


def _attn_mixer(x, w_in, w_out, pos_k, w1_k, w2_k, pos_v, w1_v, w2_v):
    B, S, _ = x.shape
    G, R, dh, H = NSA_KV_GROUPS, NSA_GROUP_SIZE, HEAD_DIM, MOBA_HEADS
    z = x @ w_in
    nq, kc, vc, ks, vs, kw, vw, ng, mq, mk, mv = _split(z, L0_SPLITS)
    kv = lambda a: a.reshape(B, S, G, dh)
    gates = jax.nn.sigmoid(ng).reshape(B, S, G, R, 3)
    slopes = _alibi_slopes(N_ATTN_HEADS)
    o_nsa = _nsa(nq.reshape(B, S, G, R, dh), kv(kc), kv(vc), kv(ks), kv(vs), kv(kw), kv(vw), gates,
                 slopes[0::2].reshape(G, R), pos_k, w1_k, w2_k, pos_v, w1_v, w2_v)
    o_moba = _moba(mq.reshape(B, S, H, dh), mk.reshape(B, S, H, dh), mv.reshape(B, S, H, dh), slopes[1::2])
    return jnp.concatenate([o_nsa, o_moba], axis=-1) @ w_out


def _short_conv(u, w):
    return lax.conv_general_dilated(u, w[:, None, :], window_strides=(1,), padding=((CONV_WIDTH - 1, 0),),
                                    dimension_numbers=('NWC', 'WIO', 'NWC'), feature_group_count=u.shape[-1])


def _gla(q, k, v, log_a, g_out, norm_g):
    B, S, H, dk = q.shape
    dv = v.shape[-1]
    f32 = jnp.float32
    L = GLA_CHUNK
    nc = S // L
    qf = q.astype(f32).reshape(B, nc, L, H, dk) * dk ** -0.5
    kf = k.astype(f32).reshape(B, nc, L, H, dk)
    vf = v.astype(f32).reshape(B, nc, L, H, dv)
    b = jnp.cumsum(log_a.reshape(B, nc, L, H, dk), axis=2)
    b_last = b[:, :, -1]
    q_t = qf * jnp.exp(b)
    k_t = kf * jnp.exp(-b)
    causal = jnp.tril(jnp.ones((L, L), dtype=bool))
    att = jnp.where(causal, jnp.einsum('bnihd,bnjhd->bnhij', q_t, k_t), 0.0)
    o_intra = jnp.einsum('bnhij,bnjhe->bnihe', att, vf)
    u = jnp.einsum('bnjhd,bnjhe->bnhde', kf * jnp.exp(b_last[:, :, None] - b), vf)
    decay = jnp.exp(b_last)

    def step(state, inp):
        dec, uc = inp
        return dec[..., None] * state + uc, state

    state0 = jnp.zeros((B, H, dk, dv), f32)
    _, s_prev = lax.scan(step, state0, (jnp.moveaxis(decay, 1, 0), jnp.moveaxis(u, 1, 0)))
    s_prev = jnp.moveaxis(s_prev, 0, 1)
    o = o_intra + jnp.einsum('bnihd,bnhde->bnihe', q_t, s_prev)
    o = o.reshape(B, S, H, dv)
    o = o * lax.rsqrt(jnp.mean(jnp.square(o), axis=-1, keepdims=True) + NORM_EPS) * norm_g
    return o.astype(v.dtype).reshape(B, S, H * dv) * jax.nn.silu(g_out)


def _conv_gla_mixer(x, w_in, w_out, conv_w, w_a2, b_a, norm_g):
    B, S, _ = x.shape
    H, dk, dv = GLA_HEADS, GLA_DK, GLA_DV
    z = x @ w_in
    gb, gc, h, q, k, v, g, za = _split(z, L1_SPLITS)
    y_conv = gb * _short_conv(gc * h, conv_w)
    log_a = jax.nn.log_sigmoid((za @ w_a2 + b_a).astype(jnp.float32)) / GLA_GATE_TAU
    y_gla = _gla(q.reshape(B, S, H, dk), k.reshape(B, S, H, dk), v.reshape(B, S, H, dv),
                 log_a.reshape(B, S, H, dk), g, norm_g)
    return jnp.concatenate([y_conv, y_gla], axis=-1) @ w_out


def setup_inputs(seed: int = 0) -> dict:
    key = jax.random.key(seed)
    keys = iter(jax.random.split(key, 32))
    f32 = jnp.float32

    def nrm(shape, scale):
        return jax.random.normal(next(keys), shape, f32) * scale

    dh = HEAD_DIM
    cmp_in = NSA_CMP_LEN * dh
    return {
        'x': nrm((BATCH, SEQ, D_MODEL), 1.0),
        'ln_g': 1.0 + nrm((DEPTH, 3, D_MODEL), 0.01),
        'ln_b': nrm((DEPTH, 3, D_MODEL), 0.01),
        'ffn_pre_wg': nrm((DEPTH, D_MODEL, D_FF), D_MODEL ** -0.5),
        'ffn_pre_wu': nrm((DEPTH, D_MODEL, D_FF), D_MODEL ** -0.5),
        'ffn_pre_wd': nrm((DEPTH, D_FF, D_MODEL), BETA * D_FF ** -0.5),
        'ffn_post_wg': nrm((DEPTH, D_MODEL, D_FF), D_MODEL ** -0.5),
        'ffn_post_wu': nrm((DEPTH, D_MODEL, D_FF), D_MODEL ** -0.5),
        'ffn_post_wd': nrm((DEPTH, D_FF, D_MODEL), BETA * D_FF ** -0.5),
        'att_w_in': nrm((N_EVEN, D_MODEL, L0_IN_DIM), D_MODEL ** -0.5),
        'att_w_out': nrm((N_EVEN, ATT_WIDTH, D_MODEL), BETA * ATT_WIDTH ** -0.5),
        'nsa_pos_k': nrm((N_EVEN, NSA_CMP_LEN, dh), 0.1),
        'nsa_w1_k': nrm((N_EVEN, cmp_in, dh), cmp_in ** -0.5),
        'nsa_w2_k': nrm((N_EVEN, dh, dh), dh ** -0.5),
        'nsa_pos_v': nrm((N_EVEN, NSA_CMP_LEN, dh), 0.1),
        'nsa_w1_v': nrm((N_EVEN, cmp_in, dh), cmp_in ** -0.5),
        'nsa_w2_v': nrm((N_EVEN, dh, dh), dh ** -0.5),
        'mix_w_in': nrm((N_ODD, D_MODEL, L1_IN_DIM), D_MODEL ** -0.5),
        'mix_w_out': nrm((N_ODD, MIX_WIDTH, D_MODEL), BETA * MIX_WIDTH ** -0.5),
        'conv_w': nrm((N_ODD, CONV_WIDTH, CONV_CHANNELS), CONV_WIDTH ** -0.5),
        'gla_w_a2': nrm((N_ODD, GLA_GATE_RANK, GLA_HEADS * GLA_DK), GLA_GATE_RANK ** -0.5),
        'gla_b_a': nrm((N_ODD, GLA_HEADS * GLA_DK), 0.1),
        'gla_norm_g': 1.0 + nrm((N_ODD, GLA_DV), 0.01),
    }


def reference(x, ln_g, ln_b, ffn_pre_wg, ffn_pre_wu, ffn_pre_wd, ffn_post_wg, ffn_post_wu, ffn_post_wd,
              att_w_in, att_w_out, nsa_pos_k, nsa_w1_k, nsa_w2_k, nsa_pos_v, nsa_w1_v, nsa_w2_v,
              mix_w_in, mix_w_out, conv_w, gla_w_a2, gla_b_a, gla_norm_g):
    for layer in range(DEPTH):
        h = _swiglu(x, ffn_pre_wg[layer], ffn_pre_wu[layer], ffn_pre_wd[layer])
        x = _layer_norm(ALPHA * x + 0.5 * h, ln_g[layer, 0], ln_b[layer, 0])
        i = layer // 2
        if layer % 2 == 0:
            y = _attn_mixer(x, att_w_in[i], att_w_out[i], nsa_pos_k[i], nsa_w1_k[i], nsa_w2_k[i],
                            nsa_pos_v[i], nsa_w1_v[i], nsa_w2_v[i])
        else:
            y = _conv_gla_mixer(x, mix_w_in[i], mix_w_out[i], conv_w[i], gla_w_a2[i], gla_b_a[i], gla_norm_g[i])
        x = _layer_norm(ALPHA * x + y, ln_g[layer, 1], ln_b[layer, 1])
        h = _swiglu(x, ffn_post_wg[layer], ffn_post_wu[layer], ffn_post_wd[layer])
        x = _layer_norm(ALPHA * x + 0.5 * h, ln_g[layer, 2], ln_b[layer, 2])
    return x
```

```python
import functools
import math

import jax
import jax.numpy as jnp
from jax import lax
from jax.experimental import pallas as pl
from jax.experimental.pallas import tpu as pltpu

F32 = jnp.float32
BF16 = jnp.bfloat16

D_MODEL = 4096
DEPTH = 2
HEAD_DIM = 128
NSA_HEADS = 16
NSA_KV_GROUPS = 4
NSA_GROUP_SIZE = 4
NSA_CMP_STRIDE = 16
NSA_CMP_LEN = 32
NSA_SEL_LEN = 64
NSA_N_SEL = 16
NSA_WINDOW = 512
MOBA_HEADS = 16
MOBA_BLOCK = 256
MOBA_TOPK = 3
N_ATTN_HEADS = NSA_HEADS + MOBA_HEADS
CONV_CHANNELS = 2048
GLA_HEADS = 16
GLA_DK = 64
GLA_DV = 128
GLA_GATE_RANK = 16
GLA_GATE_TAU = 16.0
GLA_CHUNK = 64
D_FF = 11008
ALPHA = (2 * DEPTH) ** 0.25
LN_EPS = 1e-5
NORM_EPS = 1e-6
NEG_INF = -1e30
SEL_FORCE = 1e4

LANE = 128
VMEM_LIMIT = 56 * 1024 * 1024

NT_DIMS = (((1,), (1,)), ((), ()))
TN_DIMS = (((0,), (0,)), ((), ()))


def _round_up(n, m):
    return (n + m - 1) // m * m


def _params(*sem):
    return pltpu.CompilerParams(dimension_semantics=sem, vmem_limit_bytes=VMEM_LIMIT)


def _masked_softmax(s, mask):
    s = jnp.where(mask, s, NEG_INF)
    m = jnp.max(s, axis=-1, keepdims=True)
    p = jnp.where(mask, jnp.exp(s - m), 0.0)
    return p / jnp.maximum(jnp.sum(p, axis=-1, keepdims=True), 1e-30)


def _top_k_mask(vals, k):
    n = vals.shape[-1]
    lane = lax.broadcasted_iota(jnp.int32, vals.shape, vals.ndim - 1)
    sel = jnp.zeros(vals.shape, jnp.bool_)
    g = vals
    for _ in range(k):
        m = jnp.max(g, axis=-1, keepdims=True)
        idx = jnp.min(jnp.where(g == m, lane, n), axis=-1, keepdims=True)
        pick = lane == idx
        sel = sel | pick
        g = jnp.where(pick, -jnp.inf, g)
    return sel


def _mm_kernel(a_ref, b_ref, o_ref, *, act):
    r = jnp.dot(a_ref[...], b_ref[...], preferred_element_type=F32)
    if act == "sigmoid":
        r = jax.nn.sigmoid(r)
    o_ref[...] = r.astype(o_ref.dtype)


def _matmul(a, b, out_dtype, *, tm=1024, tn=1024, act=None):
    m, k = a.shape
    n = b.shape[1]
    tm, tn = min(tm, m), min(tn, n)
    assert m % tm == 0 and n % tn == 0
    return pl.pallas_call(
        functools.partial(_mm_kernel, act=act),
        out_shape=jax.ShapeDtypeStruct((m, n), out_dtype),
        grid=(m // tm, n // tn),
        in_specs=[pl.BlockSpec((tm, k), lambda i, j: (i, 0)),
                  pl.BlockSpec((k, tn), lambda i, j: (0, j))],
        out_specs=pl.BlockSpec((tm, tn), lambda i, j: (i, j)),
        compiler_params=_params("parallel", "arbitrary"),
        name="matmul",
    )(a, b)


def _mm2_kernel(a1_ref, a2_ref, b1_ref, b2_ref, o_ref):
    r = jnp.dot(a1_ref[...], b1_ref[...], preferred_element_type=F32)
    r = r + jnp.dot(a2_ref[...], b2_ref[...], preferred_element_type=F32)
    o_ref[...] = r.astype(o_ref.dtype)


def _matmul2(a1, a2, b1, b2, out_dtype, *, tm=1024, tn=1024):
    m, k1 = a1.shape
    k2 = a2.shape[1]
    n = b1.shape[1]
    tm, tn = min(tm, m), min(tn, n)
    assert m % tm == 0 and n % tn == 0
    return pl.pallas_call(
        _mm2_kernel,
        out_shape=jax.ShapeDtypeStruct((m, n), out_dtype),
        grid=(m // tm, n // tn),
        in_specs=[pl.BlockSpec((tm, k1), lambda i, j: (i, 0)),
                  pl.BlockSpec((tm, k2), lambda i, j: (i, 0)),
                  pl.BlockSpec((k1, tn), lambda i, j: (0, j)),
                  pl.BlockSpec((k2, tn), lambda i, j: (0, j))],
        out_specs=pl.BlockSpec((tm, tn), lambda i, j: (i, j)),
        compiler_params=_params("parallel", "arbitrary"),
        name="matmul2",
    )(a1, a2, b1, b2)


def _ffn_gu_kernel(x_ref, wg_ref, wu_ref, o_ref):
    x = x_ref[...]
    g = jnp.dot(x, wg_ref[...], preferred_element_type=F32)
    u = jnp.dot(x, wu_ref[...], preferred_element_type=F32)
    o_ref[...] = (jax.nn.silu(g) * u).astype(o_ref.dtype)


def _ffn_gu(x, wg, wu, *, tm=1024, tn=512):
    m, k = x.shape
    n = wg.shape[1]
    tm, tn = min(tm, m), min(tn, n)
    assert m % tm == 0 and n % tn == 0
    return pl.pallas_call(
        _ffn_gu_kernel,
        out_shape=jax.ShapeDtypeStruct((m, n), BF16),
        grid=(m // tm, n // tn),
        in_specs=[pl.BlockSpec((tm, k), lambda i, j: (i, 0)),
                  pl.BlockSpec((k, tn), lambda i, j: (0, j)),
                  pl.BlockSpec((k, tn), lambda i, j: (0, j))],
        out_specs=pl.BlockSpec((tm, tn), lambda i, j: (i, j)),
        compiler_params=_params("parallel", "arbitrary"),
        name="ffn_gate_up",
    )(x, wg, wu)


def _mm_acc_kernel(a_ref, b_ref, o_ref):
    @pl.when(pl.program_id(1) == 0)
    def _():
        o_ref[...] = jnp.zeros_like(o_ref)

    o_ref[...] += jnp.dot(a_ref[...], b_ref[...], preferred_element_type=F32)


def _matmul_kacc(a, b, *, tm=512, tk=1024):
    m, k = a.shape
    n = b.shape[1]
    tm, tk = min(tm, m), min(tk, k)
    assert m % tm == 0 and k % tk == 0
    return pl.pallas_call(
        _mm_acc_kernel,
        out_shape=jax.ShapeDtypeStruct((m, n), F32),
        grid=(m // tm, k // tk),
        in_specs=[pl.BlockSpec((tm, tk), lambda i, kk: (i, kk)),
                  pl.BlockSpec((tk, n), lambda i, kk: (kk, 0))],
        out_specs=pl.BlockSpec((tm, n), lambda i, kk: (i, 0)),
        compiler_params=_params("parallel", "arbitrary"),
        name="matmul_kacc",
    )(a, b)


def _res_ln_kernel(x_ref, h_ref, g_ref, b_ref, of_ref, ob_ref, *, coef):
    v = ALPHA * x_ref[...] + coef * h_ref[...]
    mu = jnp.mean(v, axis=-1, keepdims=True)
    d = v - mu
    var = jnp.mean(d * d, axis=-1, keepdims=True)
    y = d * lax.rsqrt(var + LN_EPS) * g_ref[...] + b_ref[...]
    of_ref[...] = y
    ob_ref[...] = y.astype(BF16)


def _res_ln(x, h, g, b, coef, *, tm=256):
    m, d = x.shape
    tm = min(tm, m)
    assert m % tm == 0
    row = pl.BlockSpec((tm, d), lambda i: (i, 0))
    vec = pl.BlockSpec((1, d), lambda i: (0, 0))
    return pl.pallas_call(
        functools.partial(_res_ln_kernel, coef=coef),
        out_shape=(jax.ShapeDtypeStruct((m, d), F32), jax.ShapeDtypeStruct((m, d), BF16)),
        grid=(m // tm,),
        in_specs=[row, row, vec, vec],
        out_specs=(row, row),
        compiler_params=_params("parallel"),
        name="residual_layernorm",
    )(x, h, g.reshape(1, d), b.reshape(1, d))


def _moba_kernel(slope_ref, q_ref, k_ref, v_ref, o_ref, kmean_ref, *, nblk, blk, n_top, scale):
    h = pl.program_id(1)
    qi = pl.program_id(2)
    slope = slope_ref[h]
    dh = q_ref.shape[-1]

    @pl.when(qi == 0)
    def _():
        kf = k_ref[...].astype(F32).reshape(nblk, blk, dh)
        kmean_ref[...] = jnp.sum(kf, axis=1) / blk

    q = q_ref[...]
    gate = lax.dot_general(q, kmean_ref[...].astype(BF16), NT_DIMS, preferred_element_type=F32)
    lane = lax.broadcasted_iota(jnp.int32, gate.shape, 1)
    past = lane < qi
    sel = _top_k_mask(jnp.where(past, gate, NEG_INF), n_top) & past
    sel_f = sel.astype(F32)

    row = lax.broadcasted_iota(jnp.int32, (blk, blk), 0)
    col = lax.broadcasted_iota(jnp.int32, (blk, blk), 1)
    rel = (row - col).astype(F32)
    causal = col <= row

    def body(jj, carry):
        m_i, l_i, acc = carry
        j = qi - jj
        k0 = pl.multiple_of(j * blk, blk)
        kj = k_ref[pl.ds(k0, blk), :]
        vj = v_ref[pl.ds(k0, blk), :]
        s = lax.dot_general(q, kj, NT_DIMS, preferred_element_type=F32) * scale
        s = s - slope * (rel + (jj * blk).astype(F32))
        picked = jnp.sum(jnp.where(lane == j, sel_f, 0.0), axis=1, keepdims=True) > 0.5
        own = jnp.full((blk, 1), jj, jnp.int32) == 0
        mask = (causal & own) | (picked & ~own)
        s = jnp.where(mask, s, NEG_INF)
        m_new = jnp.maximum(m_i, jnp.max(s, axis=1, keepdims=True))
        p = jnp.where(mask, jnp.exp(s - m_new), 0.0)
        a = jnp.exp(m_i - m_new)
        l_new = a * l_i + jnp.sum(p, axis=1, keepdims=True)
        acc = a * acc + jnp.dot(p.astype(BF16), vj, preferred_element_type=F32)
        return m_new, l_new, acc

    init = (jnp.full((blk, 1), NEG_INF, F32), jnp.zeros((blk, 1), F32), jnp.zeros((blk, dh), F32))
    _, l_i, acc = lax.fori_loop(0, qi + 1, body, init)
    o_ref[...] = (acc / jnp.maximum(l_i, 1e-30)).astype(o_ref.dtype)


def _moba(z, slopes, *, batch, seq, q_col, k_col, v_col):
    blk = MOBA_BLOCK
    assert seq % blk == 0
    nblk = seq // blk
    n_top = min(MOBA_TOPK, nblk - 1)
    assert n_top > 0
    dh = HEAD_DIM
    kern = functools.partial(_moba_kernel, nblk=nblk, blk=blk, n_top=n_top, scale=dh ** -0.5)
    return pl.pallas_call(
        kern,
        out_shape=jax.ShapeDtypeStruct((batch * seq, MOBA_HEADS * dh), BF16),
        grid=(batch, MOBA_HEADS, nblk),
        in_specs=[pl.BlockSpec(memory_space=pltpu.SMEM),
                  pl.BlockSpec((blk, dh), lambda b, h, i: (b * nblk + i, q_col + h)),
                  pl.BlockSpec((seq, dh), lambda b, h, i: (b, k_col + h)),
                  pl.BlockSpec((seq, dh), lambda b, h, i: (b, v_col + h))],
        out_specs=pl.BlockSpec((blk, dh), lambda b, h, i: (b * nblk + i, h)),
        scratch_shapes=[pltpu.VMEM((nblk, dh), F32)],
        compiler_params=_params("parallel", "parallel", "arbitrary"),
        name="moba_attention",
    )(slopes, z, z, z)


def _compress_kernel(x_ref, pos_ref, w1_ref, w2_ref, o_ref, *, nc):
    dh = x_ref.shape[-1]
    half = NSA_CMP_STRIDE
    acc_lo = jnp.zeros((nc, dh), F32)
    acc_hi = jnp.zeros((nc, dh), F32)
    for p in range(half):
        xp = x_ref[pl.ds(p, nc, stride=half), :]
        lo = (xp + pos_ref[0, p:p + 1, :]).astype(BF16)
        hi = (xp + pos_ref[0, half + p:half + p + 1, :]).astype(BF16)
        acc_lo += jnp.dot(lo, w1_ref[0, p * dh:(p + 1) * dh, :], preferred_element_type=F32)
        acc_hi += jnp.dot(hi, w1_ref[0, (half + p) * dh:(half + p + 1) * dh, :], preferred_element_type=F32)
    pre = acc_lo + pltpu.roll(acc_hi, nc - 1, 0)
    hid = jax.nn.gelu(pre)
    o_ref[0, 0] = jnp.dot(hid.astype(BF16), w2_ref[0], preferred_element_type=F32).astype(o_ref.dtype)


def _nsa_compress(zf, pos, w1, w2, *, batch, seq):
    dh = HEAD_DIM
    g = NSA_KV_GROUPS
    nc = seq // NSA_CMP_STRIDE
    return pl.pallas_call(
        functools.partial(_compress_kernel, nc=nc),
        out_shape=jax.ShapeDtypeStruct((batch, 2 * g, nc, dh), BF16),
        grid=(batch, 2 * g),
        in_specs=[pl.BlockSpec((seq, dh), lambda b, c: (b, c)),
                  pl.BlockSpec((1, NSA_CMP_LEN, dh), lambda b, c: (c // g, 0, 0)),
                  pl.BlockSpec((1, NSA_CMP_LEN * dh, dh), lambda b, c: (c // g, 0, 0)),
                  pl.BlockSpec((1, dh, dh), lambda b, c: (c // g, 0, 0))],
        out_specs=pl.BlockSpec((1, 1, nc, dh), lambda b, c: (b, c, 0, 0)),
        compiler_params=_params("parallel", "arbitrary"),
        name="nsa_compress",
    )(zf, pos, w1, w2)


def _nsa_kernel(slope_ref, q_ref, kc_ref, vc_ref, ks_ref, vs_ref, kw_ref, vw_ref, gate_ref, o_ref,
                *, tq, tk, nc, ns, k_top, scale):
    g = pl.program_id(1)
    qi = pl.program_id(2)
    dh = HEAD_DIM
    nr = NSA_GROUP_SIZE
    rows = nr * tq
    q0 = qi * tq

    qb = q_ref[...]
    qs = jnp.concatenate([qb[:, r * dh:(r + 1) * dh] for r in range(nr)], axis=0)
    rid = lax.broadcasted_iota(jnp.int32, (rows, 1), 0)
    t = q0 + rid % tq
    tf = t.astype(F32)
    head = rid // tq
    slope = jnp.zeros((rows, 1), F32)
    for r in range(nr):
        slope = jnp.where(head == r, slope_ref[g * nr + r], slope)

    s = lax.dot_general(qs, kc_ref[0, 0], NT_DIMS, preferred_element_type=F32) * scale
    cmp_end = lax.broadcasted_iota(jnp.int32, (1, nc), 1) * NSA_CMP_STRIDE + (NSA_CMP_LEN - 1)
    s = s - slope * (tf - cmp_end.astype(F32))
    p_cmp = _masked_softmax(s, cmp_end <= t)
    o_cmp = jnp.dot(p_cmp.astype(BF16), vc_ref[0, 0], preferred_element_type=F32)

    p_sum = p_cmp[0:tq]
    for r in range(1, nr):
        p_sum = p_sum + p_cmp[r * tq:(r + 1) * tq]
    ci = lax.broadcasted_iota(jnp.int32, (nc, ns), 0) * NSA_CMP_STRIDE
    sj = lax.broadcasted_iota(jnp.int32, (nc, ns), 1) * NSA_SEL_LEN
    member = ((ci < sj + NSA_SEL_LEN) & (ci + NSA_CMP_LEN > sj)).astype(F32)
    imp = jnp.dot(p_sum, member, preferred_element_type=F32, precision=lax.Precision.HIGHEST)
    tt = q0 + lax.broadcasted_iota(jnp.int32, (tq, 1), 0)
    own = tt // NSA_SEL_LEN
    sid = lax.broadcasted_iota(jnp.int32, (tq, ns), 1)
    forced = (sid == 0) | (sid == own) | (sid == own - 1)
    imp = jnp.where(forced, imp + SEL_FORCE, jnp.where(sid > own, -1.0, imp))
    sel = _top_k_mask(imp, k_top).astype(BF16)

    bpt = tk // NSA_SEL_LEN
    erow = lax.broadcasted_iota(jnp.int32, (ns, tk), 0)
    ecol = lax.broadcasted_iota(jnp.int32, (ns, tk), 1) // NSA_SEL_LEN
    kcol = lax.broadcasted_iota(jnp.int32, (1, tk), 1)

    def body(jt, carry):
        m_i, l_i, acc = carry
        k0 = pl.multiple_of(jt * tk, tk)
        kj = ks_ref[pl.ds(k0, tk), :]
        vj = vs_ref[pl.ds(k0, tk), :]
        kpos = k0 + kcol
        sc = lax.dot_general(qs, kj, NT_DIMS, preferred_element_type=F32) * scale
        sc = sc - slope * (tf - kpos.astype(F32))
        expand = (erow == ecol + jt * bpt).astype(BF16)
        tok = jnp.dot(sel, expand, preferred_element_type=F32) > 0.5
        mask = jnp.concatenate([tok] * nr, axis=0) & (kpos <= t)
        sc = jnp.where(mask, sc, NEG_INF)
        m_new = jnp.maximum(m_i, jnp.max(sc, axis=1, keepdims=True))
        p = jnp.where(mask, jnp.exp(sc - m_new), 0.0)
        a = jnp.exp(m_i - m_new)
        l_new = a * l_i + jnp.sum(p, axis=1, keepdims=True)
        acc = a * acc + jnp.dot(p.astype(BF16), vj, preferred_element_type=F32)
        return m_new, l_new, acc

    init = (jnp.full((rows, 1), NEG_INF, F32), jnp.zeros((rows, 1), F32), jnp.zeros((rows, dh), F32))
    n_tiles = (q0 + tq - 1) // tk + 1
    _, l_i, acc = lax.fori_loop(0, n_tiles, body, init)
    o_slc = acc / jnp.maximum(l_i, 1e-30)

    span = NSA_WINDOW + tq
    w0 = pl.multiple_of(jnp.maximum(q0 - NSA_WINDOW, 0), tq)
    kwin = kw_ref[pl.ds(w0, span), :]
    vwin = vw_ref[pl.ds(w0, span), :]
    wpos = w0 + lax.broadcasted_iota(jnp.int32, (1, span), 1)
    s = lax.dot_general(qs, kwin, NT_DIMS, preferred_element_type=F32) * scale
    s = s - slope * (tf - wpos.astype(F32))
    p_win = _masked_softmax(s, (wpos <= t) & (wpos > t - NSA_WINDOW))
    o_win = jnp.dot(p_win.astype(BF16), vwin, preferred_element_type=F32)

    gt = gate_ref[0, 0]
    for r in range(nr):
        sl = slice(r * tq, (r + 1) * tq)
        o = (gt[:, 3 * r:3 * r + 1] * o_cmp[sl] + gt[:, 3 * r + 1:3 * r + 2] * o_slc[sl]
             + gt[:, 3 * r + 2:3 * r + 3] * o_win[sl])
        o_ref[:, r * dh:(r + 1) * dh] = o.astype(o_ref.dtype)


def _nsa(z, cmp, gates, slopes, *, batch, seq, q_col, ks_col, vs_col, kw_col, vw_col, tq=128, tk=512):
    dh = HEAD_DIM
    g = NSA_KV_GROUPS
    nr = NSA_GROUP_SIZE
    tk = min(tk, seq)
    assert seq % tk == 0 and seq % tq == 0 and NSA_WINDOW % tq == 0 and seq >= NSA_WINDOW + tq
    nq = seq // tq
    nc = seq // NSA_CMP_STRIDE
    ns = seq // NSA_SEL_LEN
    k_top = min(NSA_N_SEL, ns)
    kern = functools.partial(_nsa_kernel, tq=tq, tk=tk, nc=nc, ns=ns, k_top=k_top, scale=dh ** -0.5)
    seq_blk = lambda col: pl.BlockSpec((seq, dh), lambda b, gg, i: (b, col + gg))
    return pl.pallas_call(
        kern,
        out_shape=jax.ShapeDtypeStruct((batch * seq, NSA_HEADS * dh), BF16),
        grid=(batch, g, nq),
        in_specs=[pl.BlockSpec(memory_space=pltpu.SMEM),
                  pl.BlockSpec((tq, nr * dh), lambda b, gg, i: (b * nq + i, q_col // nr + gg)),
                  pl.BlockSpec((1, 1, nc, dh), lambda b, gg, i: (b, gg, 0, 0)),
                  pl.BlockSpec((1, 1, nc, dh), lambda b, gg, i: (b, g + gg, 0, 0)),
                  seq_blk(ks_col), seq_blk(vs_col), seq_blk(kw_col), seq_blk(vw_col),
                  pl.BlockSpec((1, 1, tq, 3 * nr), lambda b, gg, i: (b, gg, i, 0))],
        out_specs=pl.BlockSpec((tq, nr * dh), lambda b, gg, i: (b * nq + i, gg)),
        compiler_params=_params("parallel", "parallel", "arbitrary"),
        name="nsa_attention",
    )(slopes, z, cmp, cmp, z, z, z, z, gates)


def _conv_kernel(gb_ref, gc_ref, h_ref, gcp_ref, hp_ref, w_ref, o_ref, *, tiles_per_seq):
    i = pl.program_id(0)
    u = gc_ref[...] * h_ref[...]
    prev = gcp_ref[...] * hp_ref[...]
    prev = jnp.where(i % tiles_per_seq == 0, 0.0, prev)
    r = lax.broadcasted_iota(jnp.int32, u.shape, 0)
    u1 = jnp.where(r == 0, prev[7:8], pltpu.roll(u, 1, 0))
    u2 = jnp.where(r == 0, prev[6:7], jnp.where(r == 1, prev[7:8], pltpu.roll(u, 2, 0)))
    w = w_ref[...]
    y = gb_ref[...] * (w[0:1] * u2 + w[1:2] * u1 + w[2:3] * u)
    o_ref[...] = y.astype(o_ref.dtype)


def _short_conv(z, w, *, seq, gb_col, gc_col, h_col, tt=512, tc=512):
    m = z.shape[0]
    c = w.shape[1]
    tt = min(tt, seq)
    assert seq % tt == 0 and c % tc == 0 and tt % 8 == 0
    sub = tt // 8
    cur = lambda col: pl.BlockSpec((tt, tc), lambda i, j: (i, col + j))
    prv = lambda col: pl.BlockSpec((8, tc), lambda i, j: (jnp.maximum(i * sub - 1, 0), col + j))
    return pl.pallas_call(
        functools.partial(_conv_kernel, tiles_per_seq=seq // tt),
        out_shape=jax.ShapeDtypeStruct((m, c), BF16),
        grid=(m // tt, c // tc),
        in_specs=[cur(gb_col), cur(gc_col), cur(h_col), prv(gc_col), prv(h_col),
                  pl.BlockSpec((3, tc), lambda i, j: (0, j))],
        out_specs=pl.BlockSpec((tt, tc), lambda i, j: (i, j)),
        compiler_params=_params("parallel", "arbitrary"),
        name="short_conv",
    )(z, z, z, z, z, w)


def _gate_decay_kernel(x_ref, w1_ref, w2_ref, b_ref, o_ref):
    za = jnp.dot(x_ref[...], w1_ref[...], preferred_element_type=F32)
    pre = jnp.dot(za.astype(BF16), w2_ref[...], preferred_element_type=F32) + b_ref[...]
    ls = -(jnp.maximum(-pre, 0.0) + jnp.log1p(jnp.exp(-jnp.abs(pre))))
    o_ref[...] = ls / GLA_GATE_TAU


def _gate_decay(x, w1, w2, b, *, tm=1024):
    m, k = x.shape
    r = w1.shape[1]
    n = w2.shape[1]
    tm = min(tm, m)
    return pl.pallas_call(
        _gate_decay_kernel,
        out_shape=jax.ShapeDtypeStruct((m, n), F32),
        grid=(m // tm,),
        in_specs=[pl.BlockSpec((tm, k), lambda i: (i, 0)),
                  pl.BlockSpec((k, r), lambda i: (0, 0)),
                  pl.BlockSpec((r, n), lambda i: (0, 0)),
                  pl.BlockSpec((1, n), lambda i: (0, 0))],
        out_specs=pl.BlockSpec((tm, n), lambda i: (i, 0)),
        compiler_params=_params("parallel"),
        name="gla_gate_decay",
    )(x, w1, w2, b.reshape(1, n))


def _gla_kernel(q_ref, k_ref, v_ref, g_ref, la_ref, ng_ref, o_ref, st_ref, *, tc):
    L = GLA_CHUNK
    dk, dv = GLA_DK, GLA_DV
    hp = q_ref.shape[-1] // dk

    @pl.when(pl.program_id(2) == 0)
    def _():
        st_ref[...] = jnp.zeros_like(st_ref)

    ri = lax.broadcasted_iota(jnp.int32, (L, L), 0)
    ci = lax.broadcasted_iota(jnp.int32, (L, L), 1)
    causal = ci <= ri
    tri = causal.astype(F32)
    ng = ng_ref[...]

    for c in range(tc // L):
        rs = slice(c * L, (c + 1) * L)
        b = jnp.dot(tri, la_ref[rs, :], preferred_element_type=F32, precision=lax.Precision.HIGHEST)
        b_last = b[L - 1:L, :]
        eb = jnp.exp(b)
        q_t = (q_ref[rs, :] * dk ** -0.5) * eb
        k_raw = k_ref[rs, :]
        k_t = k_raw * jnp.exp(-b)
        k_d = k_raw * jnp.exp(b_last - b)
        dec = jnp.exp(b_last)
        for h in range(hp):
            ks = slice(h * dk, (h + 1) * dk)
            vs = slice(h * dv, (h + 1) * dv)
            qh = q_t[:, ks].astype(BF16)
            vh = v_ref[rs, vs].astype(BF16)
            att = lax.dot_general(qh, k_t[:, ks].astype(BF16), NT_DIMS, preferred_element_type=F32)
            att = jnp.where(causal, att, 0.0)
            o = jnp.dot(att.astype(BF16), vh, preferred_element_type=F32)
            st = st_ref[h]
            o = o + lax.dot_general(qh, st.astype(BF16), NT_DIMS, preferred_element_type=F32)
            u_t = lax.dot_general(vh, k_d[:, ks].astype(BF16), TN_DIMS, preferred_element_type=F32)
            st_ref[h] = st * dec[:, ks] + u_t
            o = o * lax.rsqrt(jnp.mean(o * o, axis=-1, keepdims=True) + NORM_EPS) * ng
            o_ref[rs, vs] = (o * jax.nn.silu(g_ref[rs, vs])).astype(o_ref.dtype)


def _gla(z, la, norm_g, *, batch, seq, q_col, k_col, v_col, g_col, tc=256, hp=2):
    dk, dv = GLA_DK, GLA_DV
    tc = min(tc, seq)
    assert seq % tc == 0 and tc % GLA_CHUNK == 0 and GLA_HEADS % hp == 0
    nt = seq // tc
    qk = lambda col: pl.BlockSpec((tc, hp * dk), lambda b, h, i: (b * nt + i, col + h))
    vg = lambda col: pl.BlockSpec((tc, hp * dv), lambda b, h, i: (b * nt + i, col + h))
    return pl.pallas_call(
        functools.partial(_gla_kernel, tc=tc),
        out_shape=jax.ShapeDtypeStruct((batch * seq, GLA_HEADS * dv), BF16),
        grid=(batch, GLA_HEADS // hp, nt),
        in_specs=[qk(q_col), qk(k_col), vg(v_col), vg(g_col), qk(0),
                  pl.BlockSpec((1, dv), lambda b, h, i: (0, 0))],
        out_specs=pl.BlockSpec((tc, hp * dv), lambda b, h, i: (b * nt + i, h)),
        scratch_shapes=[pltpu.VMEM((hp, dv, dk), F32)],
        compiler_params=_params("parallel", "parallel", "arbitrary"),
        name="gla",
    )(z, z, z, z, la, norm_g.reshape(1, dv))


def _pad_cols(w, n):
    return jnp.pad(w, ((0, 0), (0, n - w.shape[1])))


def _ffn_sublayer(xf, xb, wg, wu, wd, ln_g, ln_b):
    dff = wg.shape[1]
    dffp = _round_up(dff, 1024)
    gu = _ffn_gu(xb, _pad_cols(wg, dffp).astype(BF16), _pad_cols(wu, dffp).astype(BF16))
    wd_p = jnp.pad(wd, ((0, dffp - dff), (0, 0))).astype(BF16)
    h = _matmul_kacc(gu, wd_p)
    return _res_ln(xf, h, ln_g, ln_b, 0.5)


def _alibi_slopes(n):
    return jnp.exp2(-8.0 * jnp.arange(1, n + 1, dtype=F32) / n)


def _attn_sublayer(xf, xb, w_in, w_out, pos_k, w1_k, w2_k, pos_v, w1_v, w2_v, ln_g, ln_b, *, batch, seq):
    dh = HEAD_DIM
    nq = NSA_HEADS * dh
    kv = NSA_KV_GROUPS * dh
    c_cmp = nq
    c_rest = nq + 2 * kv
    c_gate = nq + 6 * kv
    n_gate = NSA_HEADS * 3
    c_moba = c_gate + n_gate
    w_b = jnp.concatenate([w_in[:, :nq], w_in[:, c_rest:c_gate], w_in[:, c_moba:]], axis=1).astype(BF16)
    w_f = w_in[:, c_cmp:c_rest].astype(BF16)
    w_g = _pad_cols(w_in[:, c_gate:c_moba], LANE).astype(BF16)
    zb = _matmul(xb, w_b, BF16)
    zf = _matmul(xb, w_f, F32)
    gates = _matmul(xb, w_g, F32, act="sigmoid")[:, :n_gate]
    gates = gates.reshape(batch, seq, NSA_KV_GROUPS, 3 * NSA_GROUP_SIZE).transpose(0, 2, 1, 3)

    slopes = _alibi_slopes(N_ATTN_HEADS)
    cmp = _nsa_compress(zf, jnp.stack([pos_k, pos_v]), jnp.stack([w1_k, w1_v]).astype(BF16),
                        jnp.stack([w2_k, w2_v]).astype(BF16), batch=batch, seq=seq)
    u = nq // dh
    o_nsa = _nsa(zb, cmp, gates, slopes[0::2], batch=batch, seq=seq, q_col=0,
                 ks_col=u, vs_col=u + 4, kw_col=u + 8, vw_col=u + 12)
    o_moba = _moba(zb, slopes[1::2], batch=batch, seq=seq, q_col=u + 16, k_col=u + 32, v_col=u + 48)
    w_o = w_out.astype(BF16)
    y = _matmul2(o_nsa, o_moba, w_o[:nq], w_o[nq:], F32)
    return _res_ln(xf, y, ln_g, ln_b, 1.0)


def _mix_sublayer(xf, xb, w_in, w_out, conv_w, w_a2, b_a, norm_g, ln_g, ln_b, *, batch, seq):
    cc = CONV_CHANNELS
    hk = GLA_HEADS * GLA_DK
    hv = GLA_HEADS * GLA_DV
    c_za = 3 * cc + 2 * hk + 2 * hv
    z = _matmul(xb, w_in[:, :c_za].astype(BF16), F32)
    w_za = _pad_cols(w_in[:, c_za:], LANE).astype(BF16)
    w_a2p = jnp.pad(w_a2, ((0, LANE - w_a2.shape[0]), (0, 0))).astype(BF16)
    la = _gate_decay(xb, w_za, w_a2p, b_a)
    y_conv = _short_conv(z, conv_w, seq=seq, gb_col=0, gc_col=cc // 512, h_col=2 * cc // 512)
    hp = 2
    y_gla = _gla(z, la, norm_g, batch=batch, seq=seq, q_col=3 * cc // (hp * GLA_DK),
                 k_col=(3 * cc + hk) // (hp * GLA_DK), v_col=(3 * cc + 2 * hk) // (hp * GLA_DV),
                 g_col=(3 * cc + 2 * hk + hv) // (hp * GLA_DV), hp=hp)
    w_o = w_out.astype(BF16)
    y = _matmul2(y_conv, y_gla, w_o[:cc], w_o[cc:], F32)
    return _res_ln(xf, y, ln_g, ln_b, 1.0)


def kernel(x, ln_g, ln_b, ffn_pre_wg, ffn_pre_wu, ffn_pre_wd, ffn_post_wg, ffn_post_wu, ffn_post_wd,
           att_w_in, att_w_out, nsa_pos_k, nsa_w1_k, nsa_w2_k, nsa_pos_v, nsa_w1_v, nsa_w2_v,
           mix_w_in, mix_w_out, conv_w, gla_w_a2, gla_b_a, gla_norm_g):
    batch, seq, d = x.shape
    xf = x.reshape(batch * seq, d)
    xb = xf.astype(BF16)
    for layer in range(DEPTH):
        xf, xb = _ffn_sublayer(xf, xb, ffn_pre_wg[layer], ffn_pre_wu[layer], ffn_pre_wd[layer],
                               ln_g[layer, 0], ln_b[layer, 0])
        i = layer // 2
        if layer % 2 == 0:
            xf, xb = _attn_sublayer(xf, xb, att_w_in[i], att_w_out[i], nsa_pos_k[i], nsa_w1_k[i], nsa_w2_k[i],
                                    nsa_pos_v[i], nsa_w1_v[i], nsa_w2_v[i], ln_g[layer, 1], ln_b[layer, 1],
                                    batch=batch, seq=seq)
        else:
            xf, xb = _mix_sublayer(xf, xb, mix_w_in[i], mix_w_out[i], conv_w[i], gla_w_a2[i], gla_b_a[i],
                                   gla_norm_g[i], ln_g[layer, 1], ln_b[layer, 1], batch=batch, seq=seq)
        xf, xb = _ffn_sublayer(xf, xb, ffn_post_wg[layer], ffn_post_wu[layer], ffn_post_wd[layer],
                               ln_g[layer, 2], ln_b[layer, 2])
    return xf.reshape(batch, seq, d)
```

```python
import functools
import math

import jax
import jax.numpy as jnp
from jax import lax
from jax.experimental import pallas as pl
from jax.experimental.pallas import tpu as pltpu

F32 = jnp.float32
BF16 = jnp.bfloat16

D_MODEL = 4096
DEPTH = 2
HEAD_DIM = 128
NSA_HEADS = 16
NSA_KV_GROUPS = 4
NSA_GROUP_SIZE = 4
NSA_CMP_STRIDE = 16
NSA_CMP_LEN = 32
NSA_SEL_LEN = 64
NSA_N_SEL = 16
NSA_WINDOW = 512
MOBA_HEADS = 16
MOBA_BLOCK = 256
MOBA_TOPK = 3
N_ATTN_HEADS = NSA_HEADS + MOBA_HEADS
CONV_CHANNELS = 2048
GLA_HEADS = 16
GLA_DK = 64
GLA_DV = 128
GLA_GATE_RANK = 16
GLA_GATE_TAU = 16.0
GLA_CHUNK = 64
D_FF = 11008
ALPHA = (2 * DEPTH) ** 0.25
LN_EPS = 1e-5
NORM_EPS = 1e-6
NEG_INF = -1e30
SEL_FORCE = 1e4
LOG2E = math.log2(math.e)

LANE = 128
ONES_ROWS = 16
VMEM_LIMIT = 56 * 1024 * 1024

NT_DIMS = (((1,), (1,)), ((), ()))
TN_DIMS = (((0,), (0,)), ((), ()))


def _round_up(n, m):
    return (n + m - 1) // m * m


def _params(*sem):
    return pltpu.CompilerParams(dimension_semantics=sem, vmem_limit_bytes=VMEM_LIMIT)


def _top_k_rows(vals, k):
    n = vals.shape[0]
    rid = lax.broadcasted_iota(jnp.int32, vals.shape, 0)
    sel = jnp.zeros(vals.shape, jnp.bool_)
    g = vals
    for _ in range(k):
        m = jnp.max(g, axis=0, keepdims=True)
        idx = jnp.min(jnp.where(g == m, rid, n), axis=0, keepdims=True)
        pick = rid == idx
        sel = sel | pick
        g = jnp.where(pick, -jnp.inf, g)
    return sel


def _pv_with_sums(vt, p):
    ones = jnp.ones((ONES_ROWS, vt.shape[1]), BF16)
    return jnp.dot(jnp.concatenate([vt, ones], axis=0), p.astype(BF16), preferred_element_type=F32)


def _mm_kernel(a_ref, b_ref, o_ref):
    o_ref[...] = jnp.dot(a_ref[...], b_ref[...], preferred_element_type=F32).astype(o_ref.dtype)


def _matmul(a, b, out_dtype, *, tm=1024, tn=1024):
    m, k = a.shape
    n = b.shape[1]
    tm, tn = min(tm, m), min(tn, n)
    assert m % tm == 0 and n % tn == 0
    return pl.pallas_call(
        _mm_kernel,
        out_shape=jax.ShapeDtypeStruct((m, n), out_dtype),
        grid=(m // tm, n // tn),
        in_specs=[pl.BlockSpec((tm, k), lambda i, j: (i, 0)),
                  pl.BlockSpec((k, tn), lambda i, j: (0, j))],
        out_specs=pl.BlockSpec((tm, tn), lambda i, j: (i, j)),
        compiler_params=_params("parallel", "arbitrary"),
        name="matmul",
    )(a, b)


def _mm_nt_kernel(w_ref, x_ref, o_ref, *, act):
    r = lax.dot_general(w_ref[...], x_ref[...], NT_DIMS, preferred_element_type=F32)
    if act == "sigmoid":
        r = jax.nn.sigmoid(r)
    o_ref[...] = r.astype(o_ref.dtype)


def _matmul_nt(wt, x, out_dtype, *, tn=1024, tm=1024, act=None):
    n, k = wt.shape
    m = x.shape[0]
    tn, tm = min(tn, n), min(tm, m)
    assert m % tm == 0 and n % tn == 0
    return pl.pallas_call(
        functools.partial(_mm_nt_kernel, act=act),
        out_shape=jax.ShapeDtypeStruct((n, m), out_dtype),
        grid=(m // tm, n // tn),
        in_specs=[pl.BlockSpec((tn, k), lambda i, j: (j, 0)),
                  pl.BlockSpec((tm, k), lambda i, j: (i, 0))],
        out_specs=pl.BlockSpec((tn, tm), lambda i, j: (j, i)),
        compiler_params=_params("parallel", "arbitrary"),
        name="matmul_nt",
    )(wt, x)


def _mm2_kernel(a1_ref, a2_ref, b1_ref, b2_ref, o_ref):
    r = jnp.dot(a1_ref[...], b1_ref[...], preferred_element_type=F32)
    r = r + jnp.dot(a2_ref[...], b2_ref[...], preferred_element_type=F32)
    o_ref[...] = r.astype(o_ref.dtype)


def _matmul2(a1, a2, b1, b2, out_dtype, *, tm=1024, tn=1024):
    m, k1 = a1.shape
    k2 = a2.shape[1]
    n = b1.shape[1]
    tm, tn = min(tm, m), min(tn, n)
    assert m % tm == 0 and n % tn == 0
    return pl.pallas_call(
        _mm2_kernel,
        out_shape=jax.ShapeDtypeStruct((m, n), out_dtype),
        grid=(m // tm, n // tn),
        in_specs=[pl.BlockSpec((tm, k1), lambda i, j: (i, 0)),
                  pl.BlockSpec((tm, k2), lambda i, j: (i, 0)),
                  pl.BlockSpec((k1, tn), lambda i, j: (0, j)),
                  pl.BlockSpec((k2, tn), lambda i, j: (0, j))],
        out_specs=pl.BlockSpec((tm, tn), lambda i, j: (i, j)),
        compiler_params=_params("parallel", "arbitrary"),
        name="matmul2",
    )(a1, a2, b1, b2)


def _ffn_gu_kernel(x_ref, wg_ref, wu_ref, o_ref):
    x = x_ref[...]
    g = jnp.dot(x, wg_ref[...], preferred_element_type=F32)
    u = jnp.dot(x, wu_ref[...], preferred_element_type=F32)
    o_ref[...] = (jax.nn.silu(g) * u).astype(o_ref.dtype)


def _ffn_gu(x, wg, wu, *, tm=1024, tn=512):
    m, k = x.shape
    n = wg.shape[1]
    tm, tn = min(tm, m), min(tn, n)
    assert m % tm == 0 and n % tn == 0
    return pl.pallas_call(
        _ffn_gu_kernel,
        out_shape=jax.ShapeDtypeStruct((m, n), BF16),
        grid=(m // tm, n // tn),
        in_specs=[pl.BlockSpec((tm, k), lambda i, j: (i, 0)),
                  pl.BlockSpec((k, tn), lambda i, j: (0, j)),
                  pl.BlockSpec((k, tn), lambda i, j: (0, j))],
        out_specs=pl.BlockSpec((tm, tn), lambda i, j: (i, j)),
        compiler_params=_params("parallel", "arbitrary"),
        name="ffn_gate_up",
    )(x, wg, wu)


def _mm_acc_kernel(a_ref, b_ref, o_ref):
    @pl.when(pl.program_id(1) == 0)
    def _():
        o_ref[...] = jnp.zeros_like(o_ref)

    o_ref[...] += jnp.dot(a_ref[...], b_ref[...], preferred_element_type=F32)


def _matmul_kacc(a, b, *, tm=512, tk=1024):
    m, k = a.shape
    n = b.shape[1]
    tm, tk = min(tm, m), min(tk, k)
    assert m % tm == 0 and k % tk == 0
    return pl.pallas_call(
        _mm_acc_kernel,
        out_shape=jax.ShapeDtypeStruct((m, n), F32),
        grid=(m // tm, k // tk),
        in_specs=[pl.BlockSpec((tm, tk), lambda i, kk: (i, kk)),
                  pl.BlockSpec((tk, n), lambda i, kk: (kk, 0))],
        out_specs=pl.BlockSpec((tm, n), lambda i, kk: (i, 0)),
        compiler_params=_params("parallel", "arbitrary"),
        name="matmul_kacc",
    )(a, b)


def _res_ln_kernel(x_ref, h_ref, g_ref, b_ref, of_ref, ob_ref, *, coef):
    v = ALPHA * x_ref[...] + coef * h_ref[...]
    mu = jnp.mean(v, axis=-1, keepdims=True)
    d = v - mu
    var = jnp.mean(d * d, axis=-1, keepdims=True)
    y = d * lax.rsqrt(var + LN_EPS) * g_ref[...] + b_ref[...]
    of_ref[...] = y
    ob_ref[...] = y.astype(BF16)


def _res_ln(x, h, g, b, coef, *, tm=256):
    m, d = x.shape
    tm = min(tm, m)
    assert m % tm == 0
    row = pl.BlockSpec((tm, d), lambda i: (i, 0))
    vec = pl.BlockSpec((1, d), lambda i: (0, 0))
    return pl.pallas_call(
        functools.partial(_res_ln_kernel, coef=coef),
        out_shape=(jax.ShapeDtypeStruct((m, d), F32), jax.ShapeDtypeStruct((m, d), BF16)),
        grid=(m // tm,),
        in_specs=[row, row, vec, vec],
        out_specs=(row, row),
        compiler_params=_params("parallel"),
        name="residual_layernorm",
    )(x, h, g.reshape(1, d), b.reshape(1, d))


def _moba_kernel(slope_ref, qt_ref, k_ref, vt_ref, o_ref, kmean_ref, bias_ref, sel_ref,
                 *, nblk, blk, n_top, c2, hp):
    hg = pl.program_id(1)
    qi = pl.program_id(2)
    dh = HEAD_DIM
    kc = lax.broadcasted_iota(jnp.int32, (blk, blk), 0)
    qr = lax.broadcasted_iota(jnp.int32, (blk, blk), 1)
    slope2 = [slope_ref[hg * hp + h] * LOG2E for h in range(hp)]
    cols = [slice(h * dh, (h + 1) * dh) for h in range(hp)]

    @pl.when(qi == 0)
    def _():
        for h in range(hp):
            kf = k_ref[:, cols[h]].astype(F32).reshape(nblk, blk, dh)
            kmean_ref[h] = (jnp.sum(kf, axis=1) / blk).astype(BF16)
            bias_ref[h] = slope2[h] * (qr - kc).astype(F32)

    qt = [qt_ref[cols[h], :] for h in range(hp)]
    gate = [jnp.dot(kmean_ref[h], qt[h], preferred_element_type=F32) for h in range(hp)]
    past = lax.broadcasted_iota(jnp.int32, (nblk, blk), 0) < qi
    for h in range(hp):
        sel = _top_k_rows(jnp.where(past, gate[h], NEG_INF), n_top) & past
        sel_ref[h] = sel.astype(F32)

    def scores(h, j):
        k0 = pl.multiple_of(j * blk, blk)
        return jnp.dot(k_ref[pl.ds(k0, blk), cols[h]], qt[h], preferred_element_type=F32) * c2 - bias_ref[h]

    def pv(h, j, p):
        k0 = pl.multiple_of(j * blk, blk)
        return _pv_with_sums(vt_ref[cols[h], pl.ds(k0, blk)], p)

    s0 = [jnp.where(kc <= qr, scores(h, qi), NEG_INF) for h in range(hp)]
    m0 = [jnp.max(s0[h], axis=0, keepdims=True) for h in range(hp)]
    acc0 = [pv(h, qi, jnp.exp2(s0[h] - m0[h])) for h in range(hp)]
    init = []
    for h in range(hp):
        init += [m0[h], acc0[h]]

    def body(jj, carry):
        j = qi - jj
        s = [scores(h, j) for h in range(hp)]
        m_new, scale, p = [], [], []
        for h in range(hp):
            m_i = carry[2 * h]
            off = slope2[h] * (jj * blk).astype(F32)
            picked = sel_ref[h, pl.ds(j, 1), :] > 0.5
            m_h = jnp.where(picked, jnp.maximum(m_i, jnp.max(s[h], axis=0, keepdims=True) - off), m_i)
            shift = jnp.where(picked, m_h + off, -NEG_INF)
            m_new.append(m_h)
            scale.append(jnp.exp2(m_i - m_h))
            p.append(jnp.exp2(s[h] - shift))
        upd = [pv(h, j, p[h]) for h in range(hp)]
        out = []
        for h in range(hp):
            out += [m_new[h], scale[h] * carry[2 * h + 1] + upd[h]]
        return tuple(out)

    fin = lax.fori_loop(1, qi + 1, body, tuple(init))
    for h in range(hp):
        acc = fin[2 * h + 1]
        o_t = acc[0:dh] / jnp.maximum(acc[dh:dh + 1], 1e-30)
        o_ref[:, cols[h]] = o_t.T.astype(o_ref.dtype)


def _moba(zt, zn, slopes, *, batch, seq, qt_row, vt_row, k_col, hp=4):
    blk = MOBA_BLOCK
    assert seq % blk == 0 and MOBA_HEADS % hp == 0
    assert qt_row % hp == 0 and vt_row % hp == 0 and k_col % hp == 0
    nblk = seq // blk
    n_top = min(MOBA_TOPK, nblk - 1)
    assert n_top > 0
    dh = HEAD_DIM
    kern = functools.partial(_moba_kernel, nblk=nblk, blk=blk, n_top=n_top, c2=dh ** -0.5 * LOG2E, hp=hp)
    return pl.pallas_call(
        kern,
        out_shape=jax.ShapeDtypeStruct((batch * seq, MOBA_HEADS * dh), BF16),
        grid=(batch, MOBA_HEADS // hp, nblk),
        in_specs=[pl.BlockSpec(memory_space=pltpu.SMEM),
                  pl.BlockSpec((hp * dh, blk), lambda b, h, i: (qt_row // hp + h, b * nblk + i)),
                  pl.BlockSpec((seq, hp * dh), lambda b, h, i: (b, k_col // hp + h)),
                  pl.BlockSpec((hp * dh, seq), lambda b, h, i: (vt_row // hp + h, b))],
        out_specs=pl.BlockSpec((blk, hp * dh), lambda b, h, i: (b * nblk + i, h)),
        scratch_shapes=[pltpu.VMEM((hp, nblk, dh), BF16),
                        pltpu.VMEM((hp, blk, blk), F32),
                        pltpu.VMEM((hp, nblk, blk), F32)],
        compiler_params=_params("parallel", "parallel", "arbitrary"),
        name="moba_attention",
    )(slopes, zt, zn, zt)


def _compress_kernel(x_ref, pos_ref, w1_ref, w2_ref, o_ref, ot_ref, *, nc):
    dh = x_ref.shape[-1]
    half = NSA_CMP_STRIDE
    acc_lo = jnp.zeros((nc, dh), F32)
    acc_hi = jnp.zeros((nc, dh), F32)
    for p in range(half):
        xp = x_ref[pl.ds(p, nc, stride=half), :]
        lo = (xp + pos_ref[0, p:p + 1, :]).astype(BF16)
        hi = (xp + pos_ref[0, half + p:half + p + 1, :]).astype(BF16)
        acc_lo += jnp.dot(lo, w1_ref[0, p * dh:(p + 1) * dh, :], preferred_element_type=F32)
        acc_hi += jnp.dot(hi, w1_ref[0, (half + p) * dh:(half + p + 1) * dh, :], preferred_element_type=F32)
    pre = acc_lo + pltpu.roll(acc_hi, nc - 1, 0)
    hid = jax.nn.gelu(pre)
    out = jnp.dot(hid.astype(BF16), w2_ref[0], preferred_element_type=F32)
    o_ref[0, 0] = out.astype(o_ref.dtype)
    ot_ref[0, 0] = out.T.astype(ot_ref.dtype)


def _nsa_compress(zf, pos, w1, w2, *, batch, seq):
    dh = HEAD_DIM
    g = NSA_KV_GROUPS
    nc = seq // NSA_CMP_STRIDE
    return pl.pallas_call(
        functools.partial(_compress_kernel, nc=nc),
        out_shape=(jax.ShapeDtypeStruct((batch, 2 * g, nc, dh), BF16),
                   jax.ShapeDtypeStruct((batch, 2 * g, dh, nc), BF16)),
        grid=(batch, 2 * g),
        in_specs=[pl.BlockSpec((seq, dh), lambda b, c: (b, c)),
                  pl.BlockSpec((1, NSA_CMP_LEN, dh), lambda b, c: (c // g, 0, 0)),
                  pl.BlockSpec((1, NSA_CMP_LEN * dh, dh), lambda b, c: (c // g, 0, 0)),
                  pl.BlockSpec((1, dh, dh), lambda b, c: (c // g, 0, 0))],
        out_specs=(pl.BlockSpec((1, 1, nc, dh), lambda b, c: (b, c, 0, 0)),
                   pl.BlockSpec((1, 1, dh, nc), lambda b, c: (b, c, 0, 0))),
        compiler_params=_params("parallel", "arbitrary"),
        name="nsa_compress",
    )(zf, pos, w1, w2)


def _nsa_kernel(slope_ref, qt_ref, kc_ref, vct_ref, ks_ref, vst_ref, kw_ref, vwt_ref, gate_ref, o_ref,
                sel_ref, bias_ref, *, tq, tk, nc, ns, k_top, c2):
    g = pl.program_id(1)
    qi = pl.program_id(2)
    dh = HEAD_DIM
    nr = NSA_GROUP_SIZE
    nl = nr * tq
    bpt = tk // NSA_SEL_LEN
    q0 = qi * tq

    lane = lax.broadcasted_iota(jnp.int32, (1, nl), 1)
    tl = q0 + lane % tq
    head = lane // tq
    slope2 = jnp.zeros((1, nl), F32)
    for r in range(nr):
        slope2 = jnp.where(head == r, slope_ref[g * nr + r] * LOG2E, slope2)

    @pl.when(qi == 0)
    def _():
        kc_i = lax.broadcasted_iota(jnp.int32, (tk, nl), 0)
        bias_ref[...] = slope2 * (lane % tq - kc_i).astype(F32)

    qb = qt_ref[...]
    qt = jnp.concatenate([qb[r * dh:(r + 1) * dh, :] for r in range(nr)], axis=1)

    span = NSA_WINDOW + tq
    w0 = pl.multiple_of(jnp.maximum(q0 - NSA_WINDOW, 0), tq)
    raw_cmp = jnp.dot(kc_ref[0, 0], qt, preferred_element_type=F32)
    raw_win = jnp.dot(kw_ref[pl.ds(w0, span), :], qt, preferred_element_type=F32)

    cend = lax.broadcasted_iota(jnp.int32, (nc, 1), 0) * NSA_CMP_STRIDE + (NSA_CMP_LEN - 1)
    s = raw_cmp * c2 - slope2 * (tl - cend).astype(F32)
    vis = cend <= tl
    s = jnp.where(vis, s, NEG_INF)
    e = jnp.where(vis, jnp.exp2(s - jnp.max(s, axis=0, keepdims=True)), 0.0)
    p_cmp = e * (1.0 / jnp.maximum(jnp.sum(e, axis=0, keepdims=True), 1e-30))
    o_cmp = jnp.dot(vct_ref[0, 0], p_cmp.astype(BF16), preferred_element_type=F32)

    p_sum = p_cmp[:, 0:tq]
    for r in range(1, nr):
        p_sum = p_sum + p_cmp[:, r * tq:(r + 1) * tq]
    sj = lax.broadcasted_iota(jnp.int32, (ns, nc), 0) * NSA_SEL_LEN
    ci = lax.broadcasted_iota(jnp.int32, (ns, nc), 1) * NSA_CMP_STRIDE
    member_t = ((ci < sj + NSA_SEL_LEN) & (ci + NSA_CMP_LEN > sj)).astype(F32)
    imp = jnp.dot(member_t, p_sum, preferred_element_type=F32, precision=lax.Precision.HIGHEST)

    rel = (w0 + lax.broadcasted_iota(jnp.int32, (span, 1), 0)) - tl
    s = raw_win * c2 + slope2 * rel.astype(F32)
    s = jnp.where((rel <= 0) & (rel > -NSA_WINDOW), s, NEG_INF)
    acc = _pv_with_sums(vwt_ref[:, pl.ds(w0, span)], jnp.exp2(s - jnp.max(s, axis=0, keepdims=True)))
    o_win = acc[0:dh] / jnp.maximum(acc[dh:dh + 1], 1e-30)

    own = (q0 + lax.broadcasted_iota(jnp.int32, (1, tq), 1)) // NSA_SEL_LEN
    sid = lax.broadcasted_iota(jnp.int32, (ns, tq), 0)
    forced = (sid == 0) | (sid == own) | (sid == own - 1)
    imp = jnp.where(forced, imp + SEL_FORCE, jnp.where(sid > own, -1.0, imp))
    sel_ref[...] = _top_k_rows(imp, k_top).astype(F32)

    def sel_rows(jt):
        rows = sel_ref[pl.ds(pl.multiple_of(jt * bpt, bpt), bpt), :]
        return jnp.concatenate([rows] * nr, axis=1)

    def scores(jt):
        k0 = pl.multiple_of(jt * tk, tk)
        return jnp.dot(ks_ref[pl.ds(k0, tk), :], qt, preferred_element_type=F32) * c2 - bias_ref[...]

    def pv(jt, p):
        k0 = pl.multiple_of(jt * tk, tk)
        return _pv_with_sums(vst_ref[:, pl.ds(k0, tk)], p)

    def blockwise(x):
        return jnp.broadcast_to(x[:, None, :], (bpt, NSA_SEL_LEN, nl)).reshape(tk, nl)

    jd = q0 // tk
    kpos = jd * tk + lax.broadcasted_iota(jnp.int32, (tk, 1), 0)
    off = slope2 * (q0 - jd * tk).astype(F32)
    ok = (blockwise(sel_rows(jd)) > 0.5) & (kpos <= tl)
    s = jnp.where(ok, scores(jd), NEG_INF)
    m0 = jnp.max(s, axis=0, keepdims=True) - off
    acc0 = pv(jd, jnp.exp2(s - (m0 + off)))

    def body(jt, carry):
        m_i, acc = carry
        s = scores(jt)
        off = slope2 * (q0 - jt * tk).astype(F32)
        rows = sel_rows(jt) > 0.5
        blk_max = jnp.max(s.reshape(bpt, NSA_SEL_LEN, nl), axis=1)
        m_tile = jnp.max(jnp.where(rows, blk_max, NEG_INF), axis=0, keepdims=True)
        m_new = jnp.maximum(m_i, m_tile - off)
        shift = jnp.where(rows, m_new + off, -NEG_INF)
        acc = jnp.exp2(m_i - m_new) * acc + pv(jt, jnp.exp2(s - blockwise(shift)))
        return m_new, acc

    _, acc = lax.fori_loop(0, jd, body, (m0, acc0))
    o_slc = acc[0:dh] / jnp.maximum(acc[dh:dh + 1], 1e-30)

    gt = gate_ref[0]
    for r in range(nr):
        sl = slice(r * tq, (r + 1) * tq)
        o = (gt[3 * r:3 * r + 1] * o_cmp[:, sl] + gt[3 * r + 1:3 * r + 2] * o_slc[:, sl]
             + gt[3 * r + 2:3 * r + 3] * o_win[:, sl])
        o_ref[:, r * dh:(r + 1) * dh] = o.T.astype(o_ref.dtype)


def _nsa(zt, zn, cmp_n, cmp_t, gates_t, slopes, *, batch, seq, qt_row, vst_row, vwt_row, ks_col, kw_col,
         tq=128, tk=512):
    dh = HEAD_DIM
    g = NSA_KV_GROUPS
    nr = NSA_GROUP_SIZE
    tk = min(tk, seq)
    assert seq % tk == 0 and tk % tq == 0 and NSA_WINDOW % tq == 0 and seq >= NSA_WINDOW + tq
    nq = seq // tq
    nc = seq // NSA_CMP_STRIDE
    ns = seq // NSA_SEL_LEN
    k_top = min(NSA_N_SEL, ns)
    kern = functools.partial(_nsa_kernel, tq=tq, tk=tk, nc=nc, ns=ns, k_top=k_top, c2=dh ** -0.5 * LOG2E)
    key_blk = lambda col: pl.BlockSpec((seq, dh), lambda b, gg, i: (b, col + gg))
    val_blk = lambda row: pl.BlockSpec((dh, seq), lambda b, gg, i: (row + gg, b))
    return pl.pallas_call(
        kern,
        out_shape=jax.ShapeDtypeStruct((batch * seq, NSA_HEADS * dh), BF16),
        grid=(batch, g, nq),
        in_specs=[pl.BlockSpec(memory_space=pltpu.SMEM),
                  pl.BlockSpec((nr * dh, tq), lambda b, gg, i: (qt_row // nr + gg, b * nq + i)),
                  pl.BlockSpec((1, 1, nc, dh), lambda b, gg, i: (b, gg, 0, 0)),
                  pl.BlockSpec((1, 1, dh, nc), lambda b, gg, i: (b, g + gg, 0, 0)),
                  key_blk(ks_col), val_blk(vst_row), key_blk(kw_col), val_blk(vwt_row),
                  pl.BlockSpec((1, gates_t.shape[1], tq), lambda b, gg, i: (gg, 0, b * nq + i))],
        out_specs=pl.BlockSpec((tq, nr * dh), lambda b, gg, i: (b * nq + i, gg)),
        scratch_shapes=[pltpu.VMEM((ns, tq), F32),
                        pltpu.VMEM((tk, nr * tq), F32)],
        compiler_params=_params("parallel", "parallel", "arbitrary"),
        name="nsa_attention",
    )(slopes, zt, cmp_n, cmp_t, zn, zt, zn, zt, gates_t)


def _conv_kernel(gb_ref, gc_ref, h_ref, gcp_ref, hp_ref, w_ref, o_ref, *, tiles_per_seq):
    i = pl.program_id(0)
    u = gc_ref[...] * h_ref[...]
    prev = gcp_ref[...] * hp_ref[...]
    prev = jnp.where(i % tiles_per_seq == 0, 0.0, prev)
    r = lax.broadcasted_iota(jnp.int32, u.shape, 0)
    u1 = jnp.where(r == 0, prev[7:8], pltpu.roll(u, 1, 0))
    u2 = jnp.where(r == 0, prev[6:7], jnp.where(r == 1, prev[7:8], pltpu.roll(u, 2, 0)))
    w = w_ref[...]
    y = gb_ref[...] * (w[0:1] * u2 + w[1:2] * u1 + w[2:3] * u)
    o_ref[...] = y.astype(o_ref.dtype)


def _short_conv(z, w, *, seq, gb_col, gc_col, h_col, tt=512, tc=512):
    m = z.shape[0]
    c = w.shape[1]
    tt = min(tt, seq)
    assert seq % tt == 0 and c % tc == 0 and tt % 8 == 0
    sub = tt // 8
    cur = lambda col: pl.BlockSpec((tt, tc), lambda i, j: (i, col + j))
    prv = lambda col: pl.BlockSpec((8, tc), lambda i, j: (jnp.maximum(i * sub - 1, 0), col + j))
    return pl.pallas_call(
        functools.partial(_conv_kernel, tiles_per_seq=seq // tt),
        out_shape=jax.ShapeDtypeStruct((m, c), BF16),
        grid=(m // tt, c // tc),
        in_specs=[cur(gb_col), cur(gc_col), cur(h_col), prv(gc_col), prv(h_col),
                  pl.BlockSpec((3, tc), lambda i, j: (0, j))],
        out_specs=pl.BlockSpec((tt, tc), lambda i, j: (i, j)),
        compiler_params=_params("parallel", "arbitrary"),
        name="short_conv",
    )(z, z, z, z, z, w)


def _gate_decay_kernel(x_ref, w1_ref, w2_ref, b_ref, o_ref):
    za = jnp.dot(x_ref[...], w1_ref[...], preferred_element_type=F32)
    pre = jnp.dot(za.astype(BF16), w2_ref[...], preferred_element_type=F32) + b_ref[...]
    ls = -(jnp.maximum(-pre, 0.0) + jnp.log1p(jnp.exp(-jnp.abs(pre))))
    o_ref[...] = ls / GLA_GATE_TAU


def _gate_decay(x, w1, w2, b, *, tm=1024):
    m, k = x.shape
    r = w1.shape[1]
    n = w2.shape[1]
    tm = min(tm, m)
    return pl.pallas_call(
        _gate_decay_kernel,
        out_shape=jax.ShapeDtypeStruct((m, n), F32),
        grid=(m // tm,),
        in_specs=[pl.BlockSpec((tm, k), lambda i: (i, 0)),
                  pl.BlockSpec((k, r), lambda i: (0, 0)),
                  pl.BlockSpec((r, n), lambda i: (0, 0)),
                  pl.BlockSpec((1, n), lambda i: (0, 0))],
        out_specs=pl.BlockSpec((tm, n), lambda i: (i, 0)),
        compiler_params=_params("parallel"),
        name="gla_gate_decay",
    )(x, w1, w2, b.reshape(1, n))


def _gla_kernel(q_ref, k_ref, v_ref, g_ref, la_ref, ng_ref, o_ref, st_ref, *, tc):
    L = GLA_CHUNK
    dk, dv = GLA_DK, GLA_DV
    hp = q_ref.shape[-1] // dk

    @pl.when(pl.program_id(2) == 0)
    def _():
        st_ref[...] = jnp.zeros_like(st_ref)

    ri = lax.broadcasted_iota(jnp.int32, (L, L), 0)
    ci = lax.broadcasted_iota(jnp.int32, (L, L), 1)
    causal = ci <= ri
    tri = causal.astype(F32)
    ng = ng_ref[...]

    for c in range(tc // L):
        rs = slice(c * L, (c + 1) * L)
        b = jnp.dot(tri, la_ref[rs, :], preferred_element_type=F32, precision=lax.Precision.HIGHEST)
        b_last = b[L - 1:L, :]
        eb = jnp.exp(b)
        q_t = (q_ref[rs, :] * dk ** -0.5) * eb
        k_raw = k_ref[rs, :]
        k_t = k_raw * jnp.exp(-b)
        k_d = k_raw * jnp.exp(b_last - b)
        dec = jnp.exp(b_last)
        for h in range(hp):
            ks = slice(h * dk, (h + 1) * dk)
            vs = slice(h * dv, (h + 1) * dv)
            qh = q_t[:, ks].astype(BF16)
            vh = v_ref[rs, vs].astype(BF16)
            att = lax.dot_general(qh, k_t[:, ks].astype(BF16), NT_DIMS, preferred_element_type=F32)
            att = jnp.where(causal, att, 0.0)
            o = jnp.dot(att.astype(BF16), vh, preferred_element_type=F32)
            st = st_ref[h]
            o = o + lax.dot_general(qh, st.astype(BF16), NT_DIMS, preferred_element_type=F32)
            u_t = lax.dot_general(vh, k_d[:, ks].astype(BF16), TN_DIMS, preferred_element_type=F32)
            st_ref[h] = st * dec[:, ks] + u_t
            o = o * lax.rsqrt(jnp.mean(o * o, axis=-1, keepdims=True) + NORM_EPS) * ng
            o_ref[rs, vs] = (o * jax.nn.silu(g_ref[rs, vs])).astype(o_ref.dtype)


def _gla(z, la, norm_g, *, batch, seq, q_col, k_col, v_col, g_col, tc=256, hp=2):
    dk, dv = GLA_DK, GLA_DV
    tc = min(tc, seq)
    assert seq % tc == 0 and tc % GLA_CHUNK == 0 and GLA_HEADS % hp == 0
    nt = seq // tc
    qk = lambda col: pl.BlockSpec((tc, hp * dk), lambda b, h, i: (b * nt + i, col + h))
    vg = lambda col: pl.BlockSpec((tc, hp * dv), lambda b, h, i: (b * nt + i, col + h))
    return pl.pallas_call(
        functools.partial(_gla_kernel, tc=tc),
        out_shape=jax.ShapeDtypeStruct((batch * seq, GLA_HEADS * dv), BF16),
        grid=(batch, GLA_HEADS // hp, nt),
        in_specs=[qk(q_col), qk(k_col), vg(v_col), vg(g_col), qk(0),
                  pl.BlockSpec((1, dv), lambda b, h, i: (0, 0))],
        out_specs=pl.BlockSpec((tc, hp * dv), lambda b, h, i: (b * nt + i, h)),
        scratch_shapes=[pltpu.VMEM((hp, dv, dk), F32)],
        compiler_params=_params("parallel", "parallel", "arbitrary"),
        name="gla",
    )(z, z, z, z, la, norm_g.reshape(1, dv))


def _pad_cols(w, n):
    return jnp.pad(w, ((0, 0), (0, n - w.shape[1])))


def _ffn_sublayer(xf, xb, wg, wu, wd, ln_g, ln_b):
    dff = wg.shape[1]
    dffp = _round_up(dff, 1024)
    gu = _ffn_gu(xb, _pad_cols(wg, dffp).astype(BF16), _pad_cols(wu, dffp).astype(BF16))
    wd_p = jnp.pad(wd, ((0, dffp - dff), (0, 0))).astype(BF16)
    h = _matmul_kacc(gu, wd_p)
    return _res_ln(xf, h, ln_g, ln_b, 0.5)


def _alibi_slopes(n):
    return jnp.exp2(-8.0 * jnp.arange(1, n + 1, dtype=F32) / n)


def _attn_sublayer(xf, xb, w_in, w_out, pos_k, w1_k, w2_k, pos_v, w1_v, w2_v, ln_g, ln_b, *, batch, seq):
    dh = HEAD_DIM
    nq = NSA_HEADS * dh
    kv = NSA_KV_GROUPS * dh
    nm = MOBA_HEADS * dh
    n_gate = NSA_HEADS * 3
    c_kc, c_ks, c_vs, c_kw, c_vw, c_gate = (nq + i * kv for i in (0, 2, 3, 4, 5, 6))
    c_mq = c_gate + n_gate
    c_mk, c_mv = c_mq + nm, c_mq + 2 * nm
    col = lambda a, n: w_in[:, a:a + n]
    w_n = jnp.concatenate([col(c_ks, kv), col(c_kw, kv), col(c_mk, nm)], axis=1).astype(BF16)
    w_t = jnp.concatenate([col(0, nq), col(c_vs, kv), col(c_vw, kv), col(c_mq, nm), col(c_mv, nm)],
                          axis=1).T.astype(BF16)
    w_f = col(c_kc, 2 * kv).astype(BF16)
    w_g = _pad_cols(col(c_gate, n_gate), LANE).T.astype(BF16)
    zn = _matmul(xb, w_n, BF16)
    zt = _matmul_nt(w_t, xb, BF16)
    zf = _matmul(xb, w_f, F32)
    gates_t = _matmul_nt(w_g, xb, F32, act="sigmoid")[:n_gate]
    gates_t = jnp.pad(gates_t.reshape(NSA_KV_GROUPS, 3 * NSA_GROUP_SIZE, batch * seq), ((0, 0), (0, 4), (0, 0)))

    slopes = _alibi_slopes(N_ATTN_HEADS)
    cmp_n, cmp_t = _nsa_compress(zf, jnp.stack([pos_k, pos_v]), jnp.stack([w1_k, w1_v]).astype(BF16),
                                 jnp.stack([w2_k, w2_v]).astype(BF16), batch=batch, seq=seq)
    u = nq // dh
    o_nsa = _nsa(zt, zn, cmp_n, cmp_t, gates_t, slopes[0::2], batch=batch, seq=seq,
                 qt_row=0, vst_row=u, vwt_row=u + 4, ks_col=0, kw_col=4)
    o_moba = _moba(zt, zn, slopes[1::2], batch=batch, seq=seq, qt_row=u + 8, vt_row=u + 24, k_col=8)
    w_o = w_out.astype(BF16)
    y = _matmul2(o_nsa, o_moba, w_o[:nq], w_o[nq:], F32)
    return _res_ln(xf, y, ln_g, ln_b, 1.0)


def _mix_sublayer(xf, xb, w_in, w_out, conv_w, w_a2, b_a, norm_g, ln_g, ln_b, *, batch, seq):
    cc = CONV_CHANNELS
    hk = GLA_HEADS * GLA_DK
    hv = GLA_HEADS * GLA_DV
    c_za = 3 * cc + 2 * hk + 2 * hv
    z = _matmul(xb, w_in[:, :c_za].astype(BF16), F32)
    w_za = _pad_cols(w_in[:, c_za:], LANE).astype(BF16)
    w_a2p = jnp.pad(w_a2, ((0, LANE - w_a2.shape[0]), (0, 0))).astype(BF16)
    la = _gate_decay(xb, w_za, w_a2p, b_a)
    y_conv = _short_conv(z, conv_w, seq=seq, gb_col=0, gc_col=cc // 512, h_col=2 * cc // 512)
    hp = 2
    y_gla = _gla(z, la, norm_g, batch=batch, seq=seq, q_col=3 * cc // (hp * GLA_DK),
                 k_col=(3 * cc + hk) // (hp * GLA_DK), v_col=(3 * cc + 2 * hk) // (hp * GLA_DV),
                 g_col=(3 * cc + 2 * hk + hv) // (hp * GLA_DV), hp=hp)
    w_o = w_out.astype(BF16)
    y = _matmul2(y_conv, y_gla, w_o[:cc], w_o[cc:], F32)
    return _res_ln(xf, y, ln_g, ln_b, 1.0)


def kernel(x, ln_g, ln_b, ffn_pre_wg, ffn_pre_wu, ffn_pre_wd, ffn_post_wg, ffn_post_wu, ffn_post_wd,
           att_w_in, att_w_out, nsa_pos_k, nsa_w1_k, nsa_w2_k, nsa_pos_v, nsa_w1_v, nsa_w2_v,
           mix_w_in, mix_w_out, conv_w, gla_w_a2, gla_b_a, gla_norm_g):
    batch, seq, d = x.shape
    xf = x.reshape(batch * seq, d)
    xb = xf.astype(BF16)
    for layer in range(DEPTH):
        xf, xb = _ffn_sublayer(xf, xb, ffn_pre_wg[layer], ffn_pre_wu[layer], ffn_pre_wd[layer],
                               ln_g[layer, 0], ln_b[layer, 0])
        i = layer // 2
        if layer % 2 == 0:
            xf, xb = _attn_sublayer(xf, xb, att_w_in[i], att_w_out[i], nsa_pos_k[i], nsa_w1_k[i], nsa_w2_k[i],
                                    nsa_pos_v[i], nsa_w1_v[i], nsa_w2_v[i], ln_g[layer, 1], ln_b[layer, 1],
                                    batch=batch, seq=seq)
        else:
            xf, xb = _mix_sublayer(xf, xb, mix_w_in[i], mix_w_out[i], conv_w[i], gla_w_a2[i], gla_b_a[i],
                                   gla_norm_g[i], ln_g[layer, 1], ln_b[layer, 1], batch=batch, seq=seq)
        xf, xb = _ffn_sublayer(xf, xb, ffn_post_wg[layer], ffn_post_wu[layer], ffn_post_wd[layer],
                               ln_g[layer, 2], ln_b[layer, 2])
    return xf.reshape(batch, seq, d)
```

```python
import functools
import math

import jax
import jax.numpy as jnp
from jax import lax
from jax.experimental import pallas as pl
from jax.experimental.pallas import tpu as pltpu

F32 = jnp.float32
BF16 = jnp.bfloat16

D_MODEL = 4096
DEPTH = 2
HEAD_DIM = 128
NSA_HEADS = 16
NSA_KV_GROUPS = 4
NSA_GROUP_SIZE = 4
NSA_CMP_STRIDE = 16
NSA_CMP_LEN = 32
NSA_SEL_LEN = 64
NSA_N_SEL = 16
NSA_WINDOW = 512
MOBA_HEADS = 16
MOBA_BLOCK = 256
MOBA_TOPK = 3
N_ATTN_HEADS = NSA_HEADS + MOBA_HEADS
CONV_CHANNELS = 2048
GLA_HEADS = 16
GLA_DK = 64
GLA_DV = 128
GLA_GATE_RANK = 16
GLA_GATE_TAU = 16.0
GLA_CHUNK = 64
D_FF = 11008
ALPHA = (2 * DEPTH) ** 0.25
LN_EPS = 1e-5
NORM_EPS = 1e-6
NEG_INF = -1e30
SEL_FORCE = 1e4
LOG2E = math.log2(math.e)

LANE = 128
ONES_ROWS = 16
VMEM_LIMIT = 56 * 1024 * 1024

NT_DIMS = (((1,), (1,)), ((), ()))
TN_DIMS = (((0,), (0,)), ((), ()))


def _round_up(n, m):
    return (n + m - 1) // m * m


def _params(*sem):
    return pltpu.CompilerParams(dimension_semantics=sem, vmem_limit_bytes=VMEM_LIMIT)


def _top_k_rows(vals, k):
    n = vals.shape[0]
    rid = lax.broadcasted_iota(jnp.int32, vals.shape, 0)
    sel = jnp.zeros(vals.shape, jnp.bool_)
    g = vals
    for _ in range(k):
        m = jnp.max(g, axis=0, keepdims=True)
        idx = jnp.min(jnp.where(g == m, rid, n), axis=0, keepdims=True)
        pick = rid == idx
        sel = sel | pick
        g = jnp.where(pick, -jnp.inf, g)
    return sel


def _pv_with_sums(vt, p):
    ones = jnp.ones((ONES_ROWS, vt.shape[1]), BF16)
    return jnp.dot(jnp.concatenate([vt, ones], axis=0), p.astype(BF16), preferred_element_type=F32)


def _mm_kernel(a_ref, b_ref, o_ref):
    o_ref[...] = jnp.dot(a_ref[...], b_ref[...], preferred_element_type=F32).astype(o_ref.dtype)


def _matmul(a, b, out_dtype, *, tm=1024, tn=1024):
    m, k = a.shape
    n = b.shape[1]
    tm, tn = min(tm, m), min(tn, n)
    assert m % tm == 0 and n % tn == 0
    return pl.pallas_call(
        _mm_kernel,
        out_shape=jax.ShapeDtypeStruct((m, n), out_dtype),
        grid=(m // tm, n // tn),
        in_specs=[pl.BlockSpec((tm, k), lambda i, j: (i, 0)),
                  pl.BlockSpec((k, tn), lambda i, j: (0, j))],
        out_specs=pl.BlockSpec((tm, tn), lambda i, j: (i, j)),
        compiler_params=_params("parallel", "arbitrary"),
        name="matmul",
    )(a, b)


def _mm_nt_kernel(w_ref, x_ref, o_ref, *, act):
    r = lax.dot_general(w_ref[...], x_ref[...], NT_DIMS, preferred_element_type=F32)
    if act == "sigmoid":
        r = jax.nn.sigmoid(r)
    o_ref[...] = r.astype(o_ref.dtype)


def _matmul_nt(wt, x, out_dtype, *, tn=1024, tm=1024, act=None):
    n, k = wt.shape
    m = x.shape[0]
    tn, tm = min(tn, n), min(tm, m)
    assert m % tm == 0 and n % tn == 0
    return pl.pallas_call(
        functools.partial(_mm_nt_kernel, act=act),
        out_shape=jax.ShapeDtypeStruct((n, m), out_dtype),
        grid=(m // tm, n // tn),
        in_specs=[pl.BlockSpec((tn, k), lambda i, j: (j, 0)),
                  pl.BlockSpec((tm, k), lambda i, j: (i, 0))],
        out_specs=pl.BlockSpec((tn, tm), lambda i, j: (j, i)),
        compiler_params=_params("parallel", "arbitrary"),
        name="matmul_nt",
    )(wt, x)


def _mm2_kernel(a1_ref, a2_ref, b1_ref, b2_ref, o_ref):
    r = jnp.dot(a1_ref[...], b1_ref[...], preferred_element_type=F32)
    r = r + jnp.dot(a2_ref[...], b2_ref[...], preferred_element_type=F32)
    o_ref[...] = r.astype(o_ref.dtype)


def _matmul2(a1, a2, b1, b2, out_dtype, *, tm=1024, tn=1024):
    m, k1 = a1.shape
    k2 = a2.shape[1]
    n = b1.shape[1]
    tm, tn = min(tm, m), min(tn, n)
    assert m % tm == 0 and n % tn == 0
    return pl.pallas_call(
        _mm2_kernel,
        out_shape=jax.ShapeDtypeStruct((m, n), out_dtype),
        grid=(m // tm, n // tn),
        in_specs=[pl.BlockSpec((tm, k1), lambda i, j: (i, 0)),
                  pl.BlockSpec((tm, k2), lambda i, j: (i, 0)),
                  pl.BlockSpec((k1, tn), lambda i, j: (0, j)),
                  pl.BlockSpec((k2, tn), lambda i, j: (0, j))],
        out_specs=pl.BlockSpec((tm, tn), lambda i, j: (i, j)),
        compiler_params=_params("parallel", "arbitrary"),
        name="matmul2",
    )(a1, a2, b1, b2)


def _ffn_gu_kernel(x_ref, wg_ref, wu_ref, o_ref, wb_ref):
    tn = o_ref.shape[1]

    @pl.when(pl.program_id(1) == 0)
    def _():
        wb_ref[:, 0:tn] = wg_ref[0].astype(BF16)
        wb_ref[:, tn:2 * tn] = wu_ref[0].astype(BF16)

    r = jnp.dot(x_ref[...], wb_ref[...], preferred_element_type=F32)
    o_ref[...] = (jax.nn.silu(r[:, 0:tn]) * r[:, tn:2 * tn]).astype(o_ref.dtype)


def _ffn_gu(x, wg, wu, layer, *, tm=1024, tn=256):
    m, k = x.shape
    n = wg.shape[2]
    tm, tn = min(tm, m), min(tn, n)
    assert m % tm == 0 and n % tn == 0
    w_spec = pl.BlockSpec((1, k, tn), lambda j, i: (layer, 0, j))
    return pl.pallas_call(
        _ffn_gu_kernel,
        out_shape=jax.ShapeDtypeStruct((m, n), BF16),
        grid=(n // tn, m // tm),
        in_specs=[pl.BlockSpec((tm, k), lambda j, i: (i, 0)), w_spec, w_spec],
        out_specs=pl.BlockSpec((tm, tn), lambda j, i: (i, j)),
        scratch_shapes=[pltpu.VMEM((k, 2 * tn), BF16)],
        compiler_params=_params("parallel", "arbitrary"),
        name="ffn_gate_up",
    )(x, wg, wu)


def _res_ln_kernel(x_ref, h_ref, g_ref, b_ref, of_ref, ob_ref, *, coef):
    v = ALPHA * x_ref[...] + coef * h_ref[...]
    mu = jnp.mean(v, axis=-1, keepdims=True)
    d = v - mu
    var = jnp.mean(d * d, axis=-1, keepdims=True)
    y = d * lax.rsqrt(var + LN_EPS) * g_ref[...] + b_ref[...]
    of_ref[...] = y
    ob_ref[...] = y.astype(BF16)


def _res_ln(x, h, g, b, coef, *, tm=256):
    m, d = x.shape
    tm = min(tm, m)
    assert m % tm == 0
    row = pl.BlockSpec((tm, d), lambda i: (i, 0))
    vec = pl.BlockSpec((1, d), lambda i: (0, 0))
    return pl.pallas_call(
        functools.partial(_res_ln_kernel, coef=coef),
        out_shape=(jax.ShapeDtypeStruct((m, d), F32), jax.ShapeDtypeStruct((m, d), BF16)),
        grid=(m // tm,),
        in_specs=[row, row, vec, vec],
        out_specs=(row, row),
        compiler_params=_params("parallel"),
        name="residual_layernorm",
    )(x, h, g.reshape(1, d), b.reshape(1, d))


def _moba_kernel(slope_ref, qt_ref, k_ref, vt_ref, o_ref, kmean_ref, bias_ref, sel_ref,
                 *, nblk, blk, n_top, c2, hp):
    hg = pl.program_id(1)
    qi = pl.program_id(2)
    dh = HEAD_DIM
    kc = lax.broadcasted_iota(jnp.int32, (blk, blk), 0)
    qr = lax.broadcasted_iota(jnp.int32, (blk, blk), 1)
    slope2 = [slope_ref[hg * hp + h] * LOG2E for h in range(hp)]
    cols = [slice(h * dh, (h + 1) * dh) for h in range(hp)]

    @pl.when(qi == 0)
    def _():
        for h in range(hp):
            kf = k_ref[:, cols[h]].astype(F32).reshape(nblk, blk, dh)
            kmean_ref[h] = (jnp.sum(kf, axis=1) / blk).astype(BF16)
            bias_ref[h] = slope2[h] * (qr - kc).astype(F32)

    qt = [qt_ref[cols[h], :] for h in range(hp)]
    gate = [jnp.dot(kmean_ref[h], qt[h], preferred_element_type=F32) for h in range(hp)]
    past = lax.broadcasted_iota(jnp.int32, (nblk, blk), 0) < qi
    for h in range(hp):
        sel = _top_k_rows(jnp.where(past, gate[h], NEG_INF), n_top) & past
        sel_ref[h] = sel.astype(F32)

    def scores(h, j):
        k0 = pl.multiple_of(j * blk, blk)
        return jnp.dot(k_ref[pl.ds(k0, blk), cols[h]], qt[h], preferred_element_type=F32) * c2 - bias_ref[h]

    def pv(h, j, p):
        k0 = pl.multiple_of(j * blk, blk)
        return _pv_with_sums(vt_ref[cols[h], pl.ds(k0, blk)], p)

    s0 = [jnp.where(kc <= qr, scores(h, qi), NEG_INF) for h in range(hp)]
    m0 = [jnp.max(s0[h], axis=0, keepdims=True) for h in range(hp)]
    acc0 = [pv(h, qi, jnp.exp2(s0[h] - m0[h])) for h in range(hp)]
    init = []
    for h in range(hp):
        init += [m0[h], acc0[h]]

    def body(jj, carry):
        j = qi - jj
        s = [scores(h, j) for h in range(hp)]
        m_new, scale, p = [], [], []
        for h in range(hp):
            m_i = carry[2 * h]
            off = slope2[h] * (jj * blk).astype(F32)
            picked = sel_ref[h, pl.ds(j, 1), :] > 0.5
            m_h = jnp.where(picked, jnp.maximum(m_i, jnp.max(s[h], axis=0, keepdims=True) - off), m_i)
            shift = jnp.where(picked, m_h + off, -NEG_INF)
            m_new.append(m_h)
            scale.append(jnp.exp2(m_i - m_h))
            p.append(jnp.exp2(s[h] - shift))
        upd = [pv(h, j, p[h]) for h in range(hp)]
        out = []
        for h in range(hp):
            out += [m_new[h], scale[h] * carry[2 * h + 1] + upd[h]]
        return tuple(out)

    fin = lax.fori_loop(1, qi + 1, body, tuple(init))
    for h in range(hp):
        acc = fin[2 * h + 1]
        o_t = acc[0:dh] / jnp.maximum(acc[dh:dh + 1], 1e-30)
        o_ref[:, cols[h]] = o_t.T.astype(o_ref.dtype)


def _moba(zt, zn, slopes, *, batch, seq, qt_row, vt_row, k_col, hp=4):
    blk = MOBA_BLOCK
    assert seq % blk == 0 and MOBA_HEADS % hp == 0
    assert qt_row % hp == 0 and vt_row % hp == 0 and k_col % hp == 0
    nblk = seq // blk
    n_top = min(MOBA_TOPK, nblk - 1)
    assert n_top > 0
    dh = HEAD_DIM
    kern = functools.partial(_moba_kernel, nblk=nblk, blk=blk, n_top=n_top, c2=dh ** -0.5 * LOG2E, hp=hp)
    return pl.pallas_call(
        kern,
        out_shape=jax.ShapeDtypeStruct((batch * seq, MOBA_HEADS * dh), BF16),
        grid=(batch, MOBA_HEADS // hp, nblk),
        in_specs=[pl.BlockSpec(memory_space=pltpu.SMEM),
                  pl.BlockSpec((hp * dh, blk), lambda b, h, i: (qt_row // hp + h, b * nblk + i)),
                  pl.BlockSpec((seq, hp * dh), lambda b, h, i: (b, k_col // hp + h)),
                  pl.BlockSpec((hp * dh, seq), lambda b, h, i: (vt_row // hp + h, b))],
        out_specs=pl.BlockSpec((blk, hp * dh), lambda b, h, i: (b * nblk + i, h)),
        scratch_shapes=[pltpu.VMEM((hp, nblk, dh), BF16),
                        pltpu.VMEM((hp, blk, blk), F32),
                        pltpu.VMEM((hp, nblk, blk), F32)],
        compiler_params=_params("parallel", "parallel", "arbitrary"),
        name="moba_attention",
    )(slopes, zt, zn, zt)


def _compress_kernel(x_ref, pos_ref, w1_ref, w2_ref, o_ref, ot_ref, *, nc):
    dh = x_ref.shape[-1]
    half = NSA_CMP_STRIDE
    acc_lo = jnp.zeros((nc, dh), F32)
    acc_hi = jnp.zeros((nc, dh), F32)
    for p in range(half):
        xp = x_ref[pl.ds(p, nc, stride=half), :]
        lo = (xp + pos_ref[0, p:p + 1, :]).astype(BF16)
        hi = (xp + pos_ref[0, half + p:half + p + 1, :]).astype(BF16)
        acc_lo += jnp.dot(lo, w1_ref[0, p * dh:(p + 1) * dh, :], preferred_element_type=F32)
        acc_hi += jnp.dot(hi, w1_ref[0, (half + p) * dh:(half + p + 1) * dh, :], preferred_element_type=F32)
    pre = acc_lo + pltpu.roll(acc_hi, nc - 1, 0)
    hid = jax.nn.gelu(pre)
    out = jnp.dot(hid.astype(BF16), w2_ref[0], preferred_element_type=F32)
    o_ref[0, 0] = out.astype(o_ref.dtype)
    ot_ref[0, 0] = out.T.astype(ot_ref.dtype)


def _nsa_compress(zf, pos, w1, w2, *, batch, seq):
    dh = HEAD_DIM
    g = NSA_KV_GROUPS
    nc = seq // NSA_CMP_STRIDE
    return pl.pallas_call(
        functools.partial(_compress_kernel, nc=nc),
        out_shape=(jax.ShapeDtypeStruct((batch, 2 * g, nc, dh), BF16),
                   jax.ShapeDtypeStruct((batch, 2 * g, dh, nc), BF16)),
        grid=(batch, 2 * g),
        in_specs=[pl.BlockSpec((seq, dh), lambda b, c: (b, c)),
                  pl.BlockSpec((1, NSA_CMP_LEN, dh), lambda b, c: (c // g, 0, 0)),
                  pl.BlockSpec((1, NSA_CMP_LEN * dh, dh), lambda b, c: (c // g, 0, 0)),
                  pl.BlockSpec((1, dh, dh), lambda b, c: (c // g, 0, 0))],
        out_specs=(pl.BlockSpec((1, 1, nc, dh), lambda b, c: (b, c, 0, 0)),
                   pl.BlockSpec((1, 1, dh, nc), lambda b, c: (b, c, 0, 0))),
        compiler_params=_params("parallel", "arbitrary"),
        name="nsa_compress",
    )(zf, pos, w1, w2)


def _nsa_kernel(slope_ref, qt_ref, kc_ref, vct_ref, ks_ref, vst_ref, kw_ref, vwt_ref, gate_ref, o_ref,
                sel_ref, bias_ref, member_ref, *, tq, tk, nc, ns, k_top, c2):
    g = pl.program_id(1)
    qi = pl.program_id(2)
    dh = HEAD_DIM
    nr = NSA_GROUP_SIZE
    nl = nr * tq
    bpt = tk // NSA_SEL_LEN
    q0 = qi * tq

    lane = lax.broadcasted_iota(jnp.int32, (1, nl), 1)
    tl = q0 + lane % tq
    head = lane // tq
    slope2 = jnp.zeros((1, nl), F32)
    for r in range(nr):
        slope2 = jnp.where(head == r, slope_ref[g * nr + r] * LOG2E, slope2)

    @pl.when(qi == 0)
    def _():
        kc_i = lax.broadcasted_iota(jnp.int32, (2 * tk, nl), 0)
        bias_ref[...] = slope2 * (lane % tq - kc_i).astype(F32)
        sj = lax.broadcasted_iota(jnp.int32, (ns, nc), 0) * NSA_SEL_LEN
        ci = lax.broadcasted_iota(jnp.int32, (ns, nc), 1) * NSA_CMP_STRIDE
        member_ref[...] = ((ci < sj + NSA_SEL_LEN) & (ci + NSA_CMP_LEN > sj)).astype(F32)

    qb = qt_ref[...]
    qt = jnp.concatenate([qb[r * dh:(r + 1) * dh, :] for r in range(nr)], axis=1)

    span = NSA_WINDOW + tq
    w0 = pl.multiple_of(jnp.maximum(q0 - NSA_WINDOW, 0), tq)
    raw_cmp = jnp.dot(kc_ref[0, 0], qt, preferred_element_type=F32)
    raw_win = jnp.dot(kw_ref[pl.ds(w0, span), :], qt, preferred_element_type=F32)

    cend = lax.broadcasted_iota(jnp.int32, (nc, 1), 0) * NSA_CMP_STRIDE + (NSA_CMP_LEN - 1)
    s = raw_cmp * c2 - slope2 * (tl - cend).astype(F32)
    vis = cend <= tl
    s = jnp.where(vis, s, NEG_INF)
    e = jnp.where(vis, jnp.exp2(s - jnp.max(s, axis=0, keepdims=True)), 0.0)
    p_cmp = e * (1.0 / jnp.maximum(jnp.sum(e, axis=0, keepdims=True), 1e-30))
    o_cmp = jnp.dot(vct_ref[0, 0], p_cmp.astype(BF16), preferred_element_type=F32)

    p_sum = p_cmp[:, 0:tq]
    for r in range(1, nr):
        p_sum = p_sum + p_cmp[:, r * tq:(r + 1) * tq]
    imp = jnp.dot(member_ref[...], p_sum, preferred_element_type=F32, precision=lax.Precision.HIGHEST)

    rel = (w0 + lax.broadcasted_iota(jnp.int32, (span, 1), 0)) - tl
    s = raw_win * c2 + slope2 * rel.astype(F32)
    s = jnp.where((rel <= 0) & (rel > -NSA_WINDOW), s, NEG_INF)
    acc = _pv_with_sums(vwt_ref[:, pl.ds(w0, span)], jnp.exp2(s - jnp.max(s, axis=0, keepdims=True)))
    o_win = acc[0:dh] / jnp.maximum(acc[dh:dh + 1], 1e-30)

    own = (q0 + lax.broadcasted_iota(jnp.int32, (1, tq), 1)) // NSA_SEL_LEN
    sid = lax.broadcasted_iota(jnp.int32, (ns, tq), 0)
    forced = (sid == 0) | (sid == own) | (sid == own - 1)
    imp = jnp.where(forced, imp + SEL_FORCE, jnp.where(sid > own, -1.0, imp))
    sel_ref[...] = _top_k_rows(imp, k_top).astype(F32)

    def sel_rows(b0, nb):
        rows = sel_ref[pl.ds(pl.multiple_of(b0, bpt), nb), :]
        return jnp.concatenate([rows] * nr, axis=1)

    def scores(k0, nk):
        return jnp.dot(ks_ref[pl.ds(k0, nk), :], qt, preferred_element_type=F32) * c2 - bias_ref[0:nk, :]

    def blockwise(x):
        nb = x.shape[0]
        return jnp.broadcast_to(x[:, None, :], (nb, NSA_SEL_LEN, nl)).reshape(nb * NSA_SEL_LEN, nl)

    jd = q0 // tk
    kd = pl.multiple_of(jd * tk, tk)
    kpos = kd + lax.broadcasted_iota(jnp.int32, (tk, 1), 0)
    off = slope2 * (q0 - kd).astype(F32)
    ok = (blockwise(sel_rows(jd * bpt, bpt)) > 0.5) & (kpos <= tl)
    s = jnp.where(ok, scores(kd, tk), NEG_INF)
    m0 = jnp.max(s, axis=0, keepdims=True) - off
    acc0 = _pv_with_sums(vst_ref[:, pl.ds(kd, tk)], jnp.exp2(s - (m0 + off)))

    def body(i, carry):
        m_i, acc = carry
        k0 = pl.multiple_of(i * (2 * tk), 2 * tk)
        s = scores(k0, 2 * tk)
        off = slope2 * (q0 - k0).astype(F32)
        bid = i * (2 * bpt) + lax.broadcasted_iota(jnp.int32, (2 * bpt, 1), 0)
        rows = (sel_rows(i * (2 * bpt), 2 * bpt) > 0.5) & (bid < jd * bpt)
        blk_max = jnp.max(s.reshape(2 * bpt, NSA_SEL_LEN, nl), axis=1)
        m_tile = jnp.max(jnp.where(rows, blk_max, NEG_INF), axis=0, keepdims=True)
        m_new = jnp.maximum(m_i, m_tile - off)
        shift = jnp.where(rows, m_new + off, -NEG_INF)
        upd = _pv_with_sums(vst_ref[:, pl.ds(k0, 2 * tk)], jnp.exp2(s - blockwise(shift)))
        return m_new, jnp.exp2(m_i - m_new) * acc + upd

    _, acc = lax.fori_loop(0, (jd + 1) // 2, body, (m0, acc0))
    o_slc = acc[0:dh] / jnp.maximum(acc[dh:dh + 1], 1e-30)

    gt = gate_ref[0]
    for r in range(nr):
        sl = slice(r * tq, (r + 1) * tq)
        o = (gt[3 * r:3 * r + 1] * o_cmp[:, sl] + gt[3 * r + 1:3 * r + 2] * o_slc[:, sl]
             + gt[3 * r + 2:3 * r + 3] * o_win[:, sl])
        o_ref[:, r * dh:(r + 1) * dh] = o.T.astype(o_ref.dtype)


def _nsa(zt, zn, cmp_n, cmp_t, gates_t, slopes, *, batch, seq, qt_row, vst_row, vwt_row, ks_col, kw_col,
         tq=128, tk=512):
    dh = HEAD_DIM
    g = NSA_KV_GROUPS
    nr = NSA_GROUP_SIZE
    tk = min(tk, seq)
    assert seq % (2 * tk) == 0 and tk % tq == 0 and NSA_WINDOW % tq == 0 and seq >= NSA_WINDOW + tq
    nq = seq // tq
    nc = seq // NSA_CMP_STRIDE
    ns = seq // NSA_SEL_LEN
    k_top = min(NSA_N_SEL, ns)
    kern = functools.partial(_nsa_kernel, tq=tq, tk=tk, nc=nc, ns=ns, k_top=k_top, c2=dh ** -0.5 * LOG2E)
    key_blk = lambda col: pl.BlockSpec((seq, dh), lambda b, gg, i: (b, col + gg))
    val_blk = lambda row: pl.BlockSpec((dh, seq), lambda b, gg, i: (row + gg, b))
    return pl.pallas_call(
        kern,
        out_shape=jax.ShapeDtypeStruct((batch * seq, NSA_HEADS * dh), BF16),
        grid=(batch, g, nq),
        in_specs=[pl.BlockSpec(memory_space=pltpu.SMEM),
                  pl.BlockSpec((nr * dh, tq), lambda b, gg, i: (qt_row // nr + gg, b * nq + i)),
                  pl.BlockSpec((1, 1, nc, dh), lambda b, gg, i: (b, gg, 0, 0)),
                  pl.BlockSpec((1, 1, dh, nc), lambda b, gg, i: (b, g + gg, 0, 0)),
                  key_blk(ks_col), val_blk(vst_row), key_blk(kw_col), val_blk(vwt_row),
                  pl.BlockSpec((1, gates_t.shape[1], tq), lambda b, gg, i: (gg, 0, b * nq + i))],
        out_specs=pl.BlockSpec((tq, nr * dh), lambda b, gg, i: (b * nq + i, gg)),
        scratch_shapes=[pltpu.VMEM((ns, tq), F32),
                        pltpu.VMEM((2 * tk, nr * tq), F32),
                        pltpu.VMEM((ns, nc), F32)],
        compiler_params=_params("parallel", "parallel", "arbitrary"),
        name="nsa_attention",
    )(slopes, zt, cmp_n, cmp_t, zn, zt, zn, zt, gates_t)


def _conv_kernel(gb_ref, gc_ref, h_ref, gcp_ref, hp_ref, w_ref, o_ref, *, tiles_per_seq):
    i = pl.program_id(0)
    u = gc_ref[...] * h_ref[...]
    prev = gcp_ref[...] * hp_ref[...]
    prev = jnp.where(i % tiles_per_seq == 0, 0.0, prev)
    r = lax.broadcasted_iota(jnp.int32, u.shape, 0)
    u1 = jnp.where(r == 0, prev[7:8], pltpu.roll(u, 1, 0))
    u2 = jnp.where(r == 0, prev[6:7], jnp.where(r == 1, prev[7:8], pltpu.roll(u, 2, 0)))
    w = w_ref[...]
    y = gb_ref[...] * (w[0:1] * u2 + w[1:2] * u1 + w[2:3] * u)
    o_ref[...] = y.astype(o_ref.dtype)


def _short_conv(z, w, *, seq, gb_col, gc_col, h_col, tt=512, tc=512):
    m = z.shape[0]
    c = w.shape[1]
    tt = min(tt, seq)
    assert seq % tt == 0 and c % tc == 0 and tt % 8 == 0
    sub = tt // 8
    cur = lambda col: pl.BlockSpec((tt, tc), lambda i, j: (i, col + j))
    prv = lambda col: pl.BlockSpec((8, tc), lambda i, j: (jnp.maximum(i * sub - 1, 0), col + j))
    return pl.pallas_call(
        functools.partial(_conv_kernel, tiles_per_seq=seq // tt),
        out_shape=jax.ShapeDtypeStruct((m, c), BF16),
        grid=(m // tt, c // tc),
        in_specs=[cur(gb_col), cur(gc_col), cur(h_col), prv(gc_col), prv(h_col),
                  pl.BlockSpec((3, tc), lambda i, j: (0, j))],
        out_specs=pl.BlockSpec((tt, tc), lambda i, j: (i, j)),
        compiler_params=_params("parallel", "arbitrary"),
        name="short_conv",
    )(z, z, z, z, z, w)


def _gate_decay_kernel(x_ref, w1_ref, w2_ref, b_ref, o_ref):
    za = jnp.dot(x_ref[...], w1_ref[...], preferred_element_type=F32)
    pre = jnp.dot(za.astype(BF16), w2_ref[...], preferred_element_type=F32) + b_ref[...]
    ls = -(jnp.maximum(-pre, 0.0) + jnp.log1p(jnp.exp(-jnp.abs(pre))))
    o_ref[...] = ls / GLA_GATE_TAU


def _gate_decay(x, w1, w2, b, *, tm=1024):
    m, k = x.shape
    r = w1.shape[1]
    n = w2.shape[1]
    tm = min(tm, m)
    return pl.pallas_call(
        _gate_decay_kernel,
        out_shape=jax.ShapeDtypeStruct((m, n), F32),
        grid=(m // tm,),
        in_specs=[pl.BlockSpec((tm, k), lambda i: (i, 0)),
                  pl.BlockSpec((k, r), lambda i: (0, 0)),
                  pl.BlockSpec((r, n), lambda i: (0, 0)),
                  pl.BlockSpec((1, n), lambda i: (0, 0))],
        out_specs=pl.BlockSpec((tm, n), lambda i: (i, 0)),
        compiler_params=_params("parallel"),
        name="gla_gate_decay",
    )(x, w1, w2, b.reshape(1, n))


def _gla_kernel(q_ref, k_ref, v_ref, g_ref, la_ref, ng_ref, o_ref, st_ref, *, tc):
    L = GLA_CHUNK
    dk, dv = GLA_DK, GLA_DV
    hp = q_ref.shape[-1] // dk

    @pl.when(pl.program_id(2) == 0)
    def _():
        st_ref[...] = jnp.zeros_like(st_ref)

    ri = lax.broadcasted_iota(jnp.int32, (L, L), 0)
    ci = lax.broadcasted_iota(jnp.int32, (L, L), 1)
    causal = ci <= ri
    tri = causal.astype(F32)
    ng = ng_ref[...]

    for c in range(tc // L):
        rs = slice(c * L, (c + 1) * L)
        b = jnp.dot(tri, la_ref[rs, :], preferred_element_type=F32, precision=lax.Precision.HIGHEST)
        b_last = b[L - 1:L, :]
        eb = jnp.exp(b)
        q_t = (q_ref[rs, :] * dk ** -0.5) * eb
        k_raw = k_ref[rs, :]
        k_t = k_raw * jnp.exp(-b)
        k_d = k_raw * jnp.exp(b_last - b)
        dec = jnp.exp(b_last)
        for h in range(hp):
            ks = slice(h * dk, (h + 1) * dk)
            vs = slice(h * dv, (h + 1) * dv)
            qh = q_t[:, ks].astype(BF16)
            vh = v_ref[rs, vs].astype(BF16)
            att = lax.dot_general(qh, k_t[:, ks].astype(BF16), NT_DIMS, preferred_element_type=F32)
            att = jnp.where(causal, att, 0.0)
            o = jnp.dot(att.astype(BF16), vh, preferred_element_type=F32)
            st = st_ref[h]
            o = o + lax.dot_general(qh, st.astype(BF16), NT_DIMS, preferred_element_type=F32)
            u_t = lax.dot_general(vh, k_d[:, ks].astype(BF16), TN_DIMS, preferred_element_type=F32)
            st_ref[h] = st * dec[:, ks] + u_t
            o = o * lax.rsqrt(jnp.mean(o * o, axis=-1, keepdims=True) + NORM_EPS) * ng
            o_ref[rs, vs] = (o * jax.nn.silu(g_ref[rs, vs])).astype(o_ref.dtype)


def _gla(z, la, norm_g, *, batch, seq, q_col, k_col, v_col, g_col, tc=256, hp=2):
    dk, dv = GLA_DK, GLA_DV
    tc = min(tc, seq)
    assert seq % tc == 0 and tc % GLA_CHUNK == 0 and GLA_HEADS % hp == 0
    nt = seq // tc
    qk = lambda col: pl.BlockSpec((tc, hp * dk), lambda b, h, i: (b * nt + i, col + h))
    vg = lambda col: pl.BlockSpec((tc, hp * dv), lambda b, h, i: (b * nt + i, col + h))
    return pl.pallas_call(
        functools.partial(_gla_kernel, tc=tc),
        out_shape=jax.ShapeDtypeStruct((batch * seq, GLA_HEADS * dv), BF16),
        grid=(batch, GLA_HEADS // hp, nt),
        in_specs=[qk(q_col), qk(k_col), vg(v_col), vg(g_col), qk(0),
                  pl.BlockSpec((1, dv), lambda b, h, i: (0, 0))],
        out_specs=pl.BlockSpec((tc, hp * dv), lambda b, h, i: (b * nt + i, h)),
        scratch_shapes=[pltpu.VMEM((hp, dv, dk), F32)],
        compiler_params=_params("parallel", "parallel", "arbitrary"),
        name="gla",
    )(z, z, z, z, la, norm_g.reshape(1, dv))


def _pad_cols(w, n):
    return jnp.pad(w, ((0, 0), (0, n - w.shape[1])))


def _ffn_sublayer(xf, xb, wg, wu, wd, layer, ln_g, ln_b):
    gu = _ffn_gu(xb, wg, wu, layer)
    h = _matmul(gu, wd[layer].astype(BF16), F32, tm=512, tn=512)
    return _res_ln(xf, h, ln_g, ln_b, 0.5)


def _alibi_slopes(n):
    return jnp.exp2(-8.0 * jnp.arange(1, n + 1, dtype=F32) / n)


def _attn_sublayer(xf, xb, w_in, w_out, pos_k, w1_k, w2_k, pos_v, w1_v, w2_v, ln_g, ln_b, *, batch, seq):
    dh = HEAD_DIM
    nq = NSA_HEADS * dh
    kv = NSA_KV_GROUPS * dh
    nm = MOBA_HEADS * dh
    n_gate = NSA_HEADS * 3
    c_kc, c_ks, c_vs, c_kw, c_vw, c_gate = (nq + i * kv for i in (0, 2, 3, 4, 5, 6))
    c_mq = c_gate + n_gate
    c_mk, c_mv = c_mq + nm, c_mq + 2 * nm
    col = lambda a, n: w_in[:, a:a + n]
    w_n = jnp.concatenate([col(c_ks, kv), col(c_kw, kv), col(c_mk, nm)], axis=1).astype(BF16)
    w_t = jnp.concatenate([col(0, nq), col(c_vs, kv), col(c_vw, kv), col(c_mq, nm), col(c_mv, nm)],
                          axis=1).T.astype(BF16)
    w_f = col(c_kc, 2 * kv).astype(BF16)
    w_g = _pad_cols(col(c_gate, n_gate), LANE).T.astype(BF16)
    zn = _matmul(xb, w_n, BF16)
    zt = _matmul_nt(w_t, xb, BF16)
    zf = _matmul(xb, w_f, F32)
    gates_t = _matmul_nt(w_g, xb, F32, act="sigmoid")[:n_gate]
    gates_t = jnp.pad(gates_t.reshape(NSA_KV_GROUPS, 3 * NSA_GROUP_SIZE, batch * seq), ((0, 0), (0, 4), (0, 0)))

    slopes = _alibi_slopes(N_ATTN_HEADS)
    cmp_n, cmp_t = _nsa_compress(zf, jnp.stack([pos_k, pos_v]), jnp.stack([w1_k, w1_v]).astype(BF16),
                                 jnp.stack([w2_k, w2_v]).astype(BF16), batch=batch, seq=seq)
    u = nq // dh
    o_nsa = _nsa(zt, zn, cmp_n, cmp_t, gates_t, slopes[0::2], batch=batch, seq=seq,
                 qt_row=0, vst_row=u, vwt_row=u + 4, ks_col=0, kw_col=4)
    o_moba = _moba(zt, zn, slopes[1::2], batch=batch, seq=seq, qt_row=u + 8, vt_row=u + 24, k_col=8)
    w_o = w_out.astype(BF16)
    y = _matmul2(o_nsa, o_moba, w_o[:nq], w_o[nq:], F32)
    return _res_ln(xf, y, ln_g, ln_b, 1.0)


def _mix_sublayer(xf, xb, w_in, w_out, conv_w, w_a2, b_a, norm_g, ln_g, ln_b, *, batch, seq):
    cc = CONV_CHANNELS
    hk = GLA_HEADS * GLA_DK
    hv = GLA_HEADS * GLA_DV
    c_za = 3 * cc + 2 * hk + 2 * hv
    z = _matmul(xb, w_in[:, :c_za].astype(BF16), F32)
    w_za = _pad_cols(w_in[:, c_za:], LANE).astype(BF16)
    w_a2p = jnp.pad(w_a2, ((0, LANE - w_a2.shape[0]), (0, 0))).astype(BF16)
    la = _gate_decay(xb, w_za, w_a2p, b_a)
    y_conv = _short_conv(z, conv_w, seq=seq, gb_col=0, gc_col=cc // 512, h_col=2 * cc // 512)
    hp = 2
    y_gla = _gla(z, la, norm_g, batch=batch, seq=seq, q_col=3 * cc // (hp * GLA_DK),
                 k_col=(3 * cc + hk) // (hp * GLA_DK), v_col=(3 * cc + 2 * hk) // (hp * GLA_DV),
                 g_col=(3 * cc + 2 * hk + hv) // (hp * GLA_DV), hp=hp)
    w_o = w_out.astype(BF16)
    y = _matmul2(y_conv, y_gla, w_o[:cc], w_o[cc:], F32)
    return _res_ln(xf, y, ln_g, ln_b, 1.0)


def kernel(x, ln_g, ln_b, ffn_pre_wg, ffn_pre_wu, ffn_pre_wd, ffn_post_wg, ffn_post_wu, ffn_post_wd,
           att_w_in, att_w_out, nsa_pos_k, nsa_w1_k, nsa_w2_k, nsa_pos_v, nsa_w1_v, nsa_w2_v,
           mix_w_in, mix_w_out, conv_w, gla_w_a2, gla_b_a, gla_norm_g):
    batch, seq, d = x.shape
    xf = x.reshape(batch * seq, d)
    xb = xf.astype(BF16)
    for layer in range(DEPTH):
        xf, xb = _ffn_sublayer(xf, xb, ffn_pre_wg, ffn_pre_wu, ffn_pre_wd, layer, ln_g[layer, 0], ln_b[layer, 0])
        i = layer // 2
        if layer % 2 == 0:
            xf, xb = _attn_sublayer(xf, xb, att_w_in[i], att_w_out[i], nsa_pos_k[i], nsa_w1_k[i], nsa_w2_k[i],
                                    nsa_pos_v[i], nsa_w1_v[i], nsa_w2_v[i], ln_g[layer, 1], ln_b[layer, 1],
                                    batch=batch, seq=seq)
        else:
            xf, xb = _mix_sublayer(xf, xb, mix_w_in[i], mix_w_out[i], conv_w[i], gla_w_a2[i], gla_b_a[i],
                                   gla_norm_g[i], ln_g[layer, 1], ln_b[layer, 1], batch=batch, seq=seq)
        xf, xb = _ffn_sublayer(xf, xb, ffn_post_wg, ffn_post_wu, ffn_post_wd, layer, ln_g[layer, 2], ln_b[layer, 2])
    return xf.reshape(batch, seq, d)
```

```python
import functools
import math

import jax
import jax.numpy as jnp
from jax import lax
from jax.experimental import pallas as pl
from jax.experimental.pallas import tpu as pltpu

F32 = jnp.float32
BF16 = jnp.bfloat16

D_MODEL = 4096
DEPTH = 2
HEAD_DIM = 128
NSA_HEADS = 16
NSA_KV_GROUPS = 4
NSA_GROUP_SIZE = 4
NSA_CMP_STRIDE = 16
NSA_CMP_LEN = 32
NSA_SEL_LEN = 64
NSA_N_SEL = 16
NSA_WINDOW = 512
MOBA_HEADS = 16
MOBA_BLOCK = 256
MOBA_TOPK = 3
N_ATTN_HEADS = NSA_HEADS + MOBA_HEADS
CONV_CHANNELS = 2048
GLA_HEADS = 16
GLA_DK = 64
GLA_DV = 128
GLA_GATE_RANK = 16
GLA_GATE_TAU = 16.0
GLA_CHUNK = 64
D_FF = 11008
ALPHA = (2 * DEPTH) ** 0.25
LN_EPS = 1e-5
NORM_EPS = 1e-6
NEG_INF = -1e30
SEL_FORCE = 1e4
LOG2E = math.log2(math.e)

LANE = 128
ONES_ROWS = 16
VMEM_LIMIT = 56 * 1024 * 1024

NT_DIMS = (((1,), (1,)), ((), ()))
TN_DIMS = (((0,), (0,)), ((), ()))


def _round_up(n, m):
    return (n + m - 1) // m * m


def _params(*sem):
    return pltpu.CompilerParams(dimension_semantics=sem, vmem_limit_bytes=VMEM_LIMIT)


def _top_k_rows(vals, k):
    n = vals.shape[0]
    rid = lax.broadcasted_iota(jnp.int32, vals.shape, 0)
    sel = jnp.zeros(vals.shape, jnp.bool_)
    g = vals
    for _ in range(k):
        m = jnp.max(g, axis=0, keepdims=True)
        idx = jnp.min(jnp.where(g == m, rid, n), axis=0, keepdims=True)
        pick = rid == idx
        sel = sel | pick
        g = jnp.where(pick, -jnp.inf, g)
    return sel


def _pv_with_sums(vt, p):
    ones = jnp.ones((ONES_ROWS, vt.shape[1]), BF16)
    return jnp.dot(jnp.concatenate([vt, ones], axis=0), p.astype(BF16), preferred_element_type=F32)


def _mm_kernel(a_ref, b_ref, o_ref):
    o_ref[...] = jnp.dot(a_ref[...], b_ref[0], preferred_element_type=F32).astype(o_ref.dtype)


def _matmul(a, b, out_dtype, *, layer=0, tm=1024, tn=1024):
    if b.ndim == 2:
        b = b[None]
    m, k = a.shape
    n = b.shape[2]
    tm, tn = min(tm, m), min(tn, n)
    assert m % tm == 0 and n % tn == 0
    return pl.pallas_call(
        _mm_kernel,
        out_shape=jax.ShapeDtypeStruct((m, n), out_dtype),
        grid=(m // tm, n // tn),
        in_specs=[pl.BlockSpec((tm, k), lambda i, j: (i, 0)),
                  pl.BlockSpec((1, k, tn), lambda i, j: (layer, 0, j))],
        out_specs=pl.BlockSpec((tm, tn), lambda i, j: (i, j)),
        compiler_params=_params("parallel", "arbitrary"),
        name="matmul",
    )(a, b)


def _mm_nt_kernel(w_ref, x_ref, o_ref, *, act):
    r = lax.dot_general(w_ref[...], x_ref[...], NT_DIMS, preferred_element_type=F32)
    if act == "sigmoid":
        r = jax.nn.sigmoid(r)
    o_ref[...] = r.astype(o_ref.dtype)


def _matmul_nt(wt, x, out_dtype, *, tn=1024, tm=1024, act=None):
    n, k = wt.shape
    m = x.shape[0]
    tn, tm = min(tn, n), min(tm, m)
    assert m % tm == 0 and n % tn == 0
    return pl.pallas_call(
        functools.partial(_mm_nt_kernel, act=act),
        out_shape=jax.ShapeDtypeStruct((n, m), out_dtype),
        grid=(m // tm, n // tn),
        in_specs=[pl.BlockSpec((tn, k), lambda i, j: (j, 0)),
                  pl.BlockSpec((tm, k), lambda i, j: (i, 0))],
        out_specs=pl.BlockSpec((tn, tm), lambda i, j: (j, i)),
        compiler_params=_params("parallel", "arbitrary"),
        name="matmul_nt",
    )(wt, x)


def _mm2_kernel(a1_ref, a2_ref, b1_ref, b2_ref, o_ref):
    r = jnp.dot(a1_ref[...], b1_ref[...], preferred_element_type=F32)
    r = r + jnp.dot(a2_ref[...], b2_ref[...], preferred_element_type=F32)
    o_ref[...] = r.astype(o_ref.dtype)


def _matmul2(a1, a2, b1, b2, out_dtype, *, tm=1024, tn=1024):
    m, k1 = a1.shape
    k2 = a2.shape[1]
    n = b1.shape[1]
    tm, tn = min(tm, m), min(tn, n)
    assert m % tm == 0 and n % tn == 0
    return pl.pallas_call(
        _mm2_kernel,
        out_shape=jax.ShapeDtypeStruct((m, n), out_dtype),
        grid=(m // tm, n // tn),
        in_specs=[pl.BlockSpec((tm, k1), lambda i, j: (i, 0)),
                  pl.BlockSpec((tm, k2), lambda i, j: (i, 0)),
                  pl.BlockSpec((k1, tn), lambda i, j: (0, j)),
                  pl.BlockSpec((k2, tn), lambda i, j: (0, j))],
        out_specs=pl.BlockSpec((tm, tn), lambda i, j: (i, j)),
        compiler_params=_params("parallel", "arbitrary"),
        name="matmul2",
    )(a1, a2, b1, b2)


def _ffn_gu_kernel(x_ref, wg_ref, wu_ref, o_ref, wb_ref):
    tn = o_ref.shape[1]

    @pl.when(pl.program_id(1) == 0)
    def _():
        wb_ref[:, 0:tn] = wg_ref[0].astype(BF16)
        wb_ref[:, tn:2 * tn] = wu_ref[0].astype(BF16)

    r = jnp.dot(x_ref[...], wb_ref[...], preferred_element_type=F32)
    o_ref[...] = (jax.nn.silu(r[:, 0:tn]) * r[:, tn:2 * tn]).astype(o_ref.dtype)


def _ffn_gu(x, wg, wu, layer, *, tm=1024, tn=256):
    m, k = x.shape
    n = wg.shape[2]
    tm, tn = min(tm, m), min(tn, n)
    assert m % tm == 0 and n % tn == 0
    w_spec = pl.BlockSpec((1, k, tn), lambda j, i: (layer, 0, j))
    return pl.pallas_call(
        _ffn_gu_kernel,
        out_shape=jax.ShapeDtypeStruct((m, n), BF16),
        grid=(n // tn, m // tm),
        in_specs=[pl.BlockSpec((tm, k), lambda j, i: (i, 0)), w_spec, w_spec],
        out_specs=pl.BlockSpec((tm, tn), lambda j, i: (i, j)),
        scratch_shapes=[pltpu.VMEM((k, 2 * tn), BF16)],
        compiler_params=_params("parallel", "arbitrary"),
        name="ffn_gate_up",
    )(x, wg, wu)


def _res_ln_kernel(x_ref, h_ref, g_ref, b_ref, of_ref, ob_ref, *, coef):
    v = ALPHA * x_ref[...] + coef * h_ref[...]
    mu = jnp.mean(v, axis=-1, keepdims=True)
    d = v - mu
    var = jnp.mean(d * d, axis=-1, keepdims=True)
    y = d * lax.rsqrt(var + LN_EPS) * g_ref[...] + b_ref[...]
    of_ref[...] = y
    ob_ref[...] = y.astype(BF16)


def _res_ln(x, h, g, b, coef, *, tm=256):
    m, d = x.shape
    tm = min(tm, m)
    assert m % tm == 0
    row = pl.BlockSpec((tm, d), lambda i: (i, 0))
    vec = pl.BlockSpec((1, d), lambda i: (0, 0))
    return pl.pallas_call(
        functools.partial(_res_ln_kernel, coef=coef),
        out_shape=(jax.ShapeDtypeStruct((m, d), F32), jax.ShapeDtypeStruct((m, d), BF16)),
        grid=(m // tm,),
        in_specs=[row, row, vec, vec],
        out_specs=(row, row),
        compiler_params=_params("parallel"),
        name="residual_layernorm",
    )(x, h, g.reshape(1, d), b.reshape(1, d))


def _moba_kernel(slope_ref, qt_ref, k_ref, vt_ref, o_ref, kmean_ref, bias_ref, sel_ref,
                 *, nblk, blk, n_top, c2, hp):
    hg = pl.program_id(1)
    qi = pl.program_id(2)
    dh = HEAD_DIM
    kc = lax.broadcasted_iota(jnp.int32, (blk, blk), 0)
    qr = lax.broadcasted_iota(jnp.int32, (blk, blk), 1)
    slope2 = [slope_ref[hg * hp + h] * LOG2E for h in range(hp)]
    cols = [slice(h * dh, (h + 1) * dh) for h in range(hp)]

    @pl.when(qi == 0)
    def _():
        kc2 = lax.broadcasted_iota(jnp.int32, (2 * blk, blk), 0)
        qr2 = lax.broadcasted_iota(jnp.int32, (2 * blk, blk), 1)
        for h in range(hp):
            kf = k_ref[:, cols[h]].astype(F32).reshape(nblk, blk, dh)
            kmean_ref[h] = (jnp.sum(kf, axis=1) / blk).astype(BF16)
            bias_ref[h] = slope2[h] * (qr2 - kc2).astype(F32)

    qt = [qt_ref[cols[h], :] for h in range(hp)]
    gate = [jnp.dot(kmean_ref[h], qt[h], preferred_element_type=F32) for h in range(hp)]
    past = lax.broadcasted_iota(jnp.int32, (nblk, blk), 0) < qi
    for h in range(hp):
        sel = _top_k_rows(jnp.where(past, gate[h], NEG_INF), n_top) & past
        sel_ref[h] = sel.astype(F32)

    def scores(h, k0, nk):
        return (jnp.dot(k_ref[pl.ds(k0, nk), cols[h]], qt[h], preferred_element_type=F32) * c2
                - bias_ref[h, 0:nk, :])

    def pv(h, k0, nk, p):
        return _pv_with_sums(vt_ref[cols[h], pl.ds(k0, nk)], p)

    kq = pl.multiple_of(qi * blk, blk)
    s0 = [jnp.where(kc <= qr, scores(h, kq, blk), NEG_INF) for h in range(hp)]
    m0 = [jnp.max(s0[h], axis=0, keepdims=True) for h in range(hp)]
    acc0 = [pv(h, kq, blk, jnp.exp2(s0[h] - m0[h])) for h in range(hp)]
    init = []
    for h in range(hp):
        init += [m0[h], acc0[h]]

    def body(t, carry):
        j = qi - 1 - 2 * t
        pb = jnp.maximum(j - 1, 0)
        k0 = pl.multiple_of(pb * blk, blk)
        s = [scores(h, k0, 2 * blk) for h in range(hp)]
        valid = pb + lax.broadcasted_iota(jnp.int32, (2, 1), 0) <= j
        m_new, scale, p = [], [], []
        for h in range(hp):
            m_i = carry[2 * h]
            off = slope2[h] * ((qi - pb) * blk).astype(F32)
            rows = jnp.concatenate([sel_ref[h, pl.ds(pb, 1), :], sel_ref[h, pl.ds(pb + 1, 1), :]], axis=0)
            picked = (rows > 0.5) & valid
            blk_max = jnp.max(s[h].reshape(2, blk, blk), axis=1)
            m_pair = jnp.max(jnp.where(picked, blk_max, NEG_INF), axis=0, keepdims=True)
            m_h = jnp.maximum(m_i, m_pair - off)
            shift = jnp.where(picked, m_h + off, -NEG_INF)
            shift = jnp.broadcast_to(shift[:, None, :], (2, blk, blk)).reshape(2 * blk, blk)
            m_new.append(m_h)
            scale.append(jnp.exp2(m_i - m_h))
            p.append(jnp.exp2(s[h] - shift))
        upd = [pv(h, k0, 2 * blk, p[h]) for h in range(hp)]
        out = []
        for h in range(hp):
            out += [m_new[h], scale[h] * carry[2 * h + 1] + upd[h]]
        return tuple(out)

    fin = lax.fori_loop(0, (qi + 1) // 2, body, tuple(init))
    for h in range(hp):
        acc = fin[2 * h + 1]
        o_t = acc[0:dh] / jnp.maximum(acc[dh:dh + 1], 1e-30)
        o_ref[:, cols[h]] = o_t.T.astype(o_ref.dtype)


def _moba(zt, zn, slopes, *, batch, seq, qt_row, vt_row, k_col, hp=4):
    blk = MOBA_BLOCK
    assert seq % blk == 0 and MOBA_HEADS % hp == 0
    assert qt_row % hp == 0 and vt_row % hp == 0 and k_col % hp == 0
    nblk = seq // blk
    n_top = min(MOBA_TOPK, nblk - 1)
    assert n_top > 0
    dh = HEAD_DIM
    kern = functools.partial(_moba_kernel, nblk=nblk, blk=blk, n_top=n_top, c2=dh ** -0.5 * LOG2E, hp=hp)
    return pl.pallas_call(
        kern,
        out_shape=jax.ShapeDtypeStruct((batch * seq, MOBA_HEADS * dh), BF16),
        grid=(batch, MOBA_HEADS // hp, nblk),
        in_specs=[pl.BlockSpec(memory_space=pltpu.SMEM),
                  pl.BlockSpec((hp * dh, blk), lambda b, h, i: (qt_row // hp + h, b * nblk + i)),
                  pl.BlockSpec((seq, hp * dh), lambda b, h, i: (b, k_col // hp + h)),
                  pl.BlockSpec((hp * dh, seq), lambda b, h, i: (vt_row // hp + h, b))],
        out_specs=pl.BlockSpec((blk, hp * dh), lambda b, h, i: (b * nblk + i, h)),
        scratch_shapes=[pltpu.VMEM((hp, nblk, dh), BF16),
                        pltpu.VMEM((hp, 2 * blk, blk), F32),
                        pltpu.VMEM((hp, nblk, blk), F32)],
        compiler_params=_params("parallel", "parallel", "arbitrary"),
        name="moba_attention",
    )(slopes, zt, zn, zt)


def _compress_kernel(x_ref, pos_ref, w1_ref, w2_ref, o_ref, ot_ref, *, nc):
    dh = x_ref.shape[-1]
    half = NSA_CMP_STRIDE
    acc_lo = jnp.zeros((nc, dh), F32)
    acc_hi = jnp.zeros((nc, dh), F32)
    for p in range(half):
        xp = x_ref[pl.ds(p, nc, stride=half), :]
        lo = (xp + pos_ref[0, p:p + 1, :]).astype(BF16)
        hi = (xp + pos_ref[0, half + p:half + p + 1, :]).astype(BF16)
        acc_lo += jnp.dot(lo, w1_ref[0, p * dh:(p + 1) * dh, :], preferred_element_type=F32)
        acc_hi += jnp.dot(hi, w1_ref[0, (half + p) * dh:(half + p + 1) * dh, :], preferred_element_type=F32)
    pre = acc_lo + pltpu.roll(acc_hi, nc - 1, 0)
    hid = jax.nn.gelu(pre)
    out = jnp.dot(hid.astype(BF16), w2_ref[0], preferred_element_type=F32)
    o_ref[0, 0] = out.astype(o_ref.dtype)
    ot_ref[0, 0] = out.T.astype(ot_ref.dtype)


def _nsa_compress(zf, pos, w1, w2, *, batch, seq):
    dh = HEAD_DIM
    g = NSA_KV_GROUPS
    nc = seq // NSA_CMP_STRIDE
    return pl.pallas_call(
        functools.partial(_compress_kernel, nc=nc),
        out_shape=(jax.ShapeDtypeStruct((batch, 2 * g, nc, dh), BF16),
                   jax.ShapeDtypeStruct((batch, 2 * g, dh, nc), BF16)),
        grid=(batch, 2 * g),
        in_specs=[pl.BlockSpec((seq, dh), lambda b, c: (b, c)),
                  pl.BlockSpec((1, NSA_CMP_LEN, dh), lambda b, c: (c // g, 0, 0)),
                  pl.BlockSpec((1, NSA_CMP_LEN * dh, dh), lambda b, c: (c // g, 0, 0)),
                  pl.BlockSpec((1, dh, dh), lambda b, c: (c // g, 0, 0))],
        out_specs=(pl.BlockSpec((1, 1, nc, dh), lambda b, c: (b, c, 0, 0)),
                   pl.BlockSpec((1, 1, dh, nc), lambda b, c: (b, c, 0, 0))),
        compiler_params=_params("parallel", "arbitrary"),
        name="nsa_compress",
    )(zf, pos, w1, w2)


def _nsa_kernel(slope_ref, qt_ref, kc_ref, vct_ref, ks_ref, vst_ref, kw_ref, vwt_ref, gate_ref, o_ref,
                sel_ref, bias_ref, member_ref, ecmp_ref, *, tq, tk, nc, ns, k_top, c2):
    g = pl.program_id(1)
    qi = pl.program_id(2)
    dh = HEAD_DIM
    nr = NSA_GROUP_SIZE
    nl = nr * tq
    bpt = tk // NSA_SEL_LEN
    q0 = qi * tq

    lane = lax.broadcasted_iota(jnp.int32, (1, nl), 1)
    tl = q0 + lane % tq
    head = lane // tq
    slope2 = jnp.zeros((1, nl), F32)
    for r in range(nr):
        slope2 = jnp.where(head == r, slope_ref[g * nr + r] * LOG2E, slope2)

    @pl.when(qi == 0)
    def _():
        kc_i = lax.broadcasted_iota(jnp.int32, (2 * tk, nl), 0)
        bias_ref[...] = slope2 * (lane % tq - kc_i).astype(F32)
        sj = lax.broadcasted_iota(jnp.int32, (ns, nc), 0) * NSA_SEL_LEN
        ci = lax.broadcasted_iota(jnp.int32, (ns, nc), 1) * NSA_CMP_STRIDE
        member_ref[...] = ((ci < sj + NSA_SEL_LEN) & (ci + NSA_CMP_LEN > sj)).astype(BF16)
        ecmp_ref[...] = slope2 * (lax.broadcasted_iota(jnp.int32, (nc, nl), 0) * NSA_CMP_STRIDE
                                  + (NSA_CMP_LEN - 1)).astype(F32)

    qb = qt_ref[...]
    qt = jnp.concatenate([qb[r * dh:(r + 1) * dh, :] for r in range(nr)], axis=1)

    span = NSA_WINDOW + tq
    w0 = pl.multiple_of(jnp.maximum(q0 - NSA_WINDOW, 0), tq)
    raw_cmp = jnp.dot(kc_ref[0, 0], qt, preferred_element_type=F32)
    raw_win = jnp.dot(kw_ref[pl.ds(w0, span), :], qt, preferred_element_type=F32)

    cend = lax.broadcasted_iota(jnp.int32, (nc, 1), 0) * NSA_CMP_STRIDE + (NSA_CMP_LEN - 1)
    s = (raw_cmp * c2 + ecmp_ref[...]) - slope2 * tl.astype(F32)
    vis = cend <= tl
    s = jnp.where(vis, s, NEG_INF)
    e = jnp.where(vis, jnp.exp2(s - jnp.max(s, axis=0, keepdims=True)), 0.0)
    p_cmp = e * (1.0 / jnp.maximum(jnp.sum(e, axis=0, keepdims=True), 1e-30))
    o_cmp = jnp.dot(vct_ref[0, 0], p_cmp.astype(BF16), preferred_element_type=F32)

    p_sum = p_cmp[:, 0:tq]
    for r in range(1, nr):
        p_sum = p_sum + p_cmp[:, r * tq:(r + 1) * tq]
    p_hi = p_sum.astype(BF16)
    rest = p_sum - p_hi.astype(F32)
    p_mid = rest.astype(BF16)
    p_lo = (rest - p_mid.astype(F32)).astype(BF16)
    member = member_ref[...]
    imp = (jnp.dot(member, p_hi, preferred_element_type=F32) + jnp.dot(member, p_mid, preferred_element_type=F32)
           + jnp.dot(member, p_lo, preferred_element_type=F32))

    rel = (w0 + lax.broadcasted_iota(jnp.int32, (span, 1), 0)) - tl
    s = (raw_win * c2 - bias_ref[0:span, :]) + slope2 * (w0 - q0).astype(F32)
    s = jnp.where((rel <= 0) & (rel > -NSA_WINDOW), s, NEG_INF)
    acc = _pv_with_sums(vwt_ref[:, pl.ds(w0, span)], jnp.exp2(s - jnp.max(s, axis=0, keepdims=True)))
    o_win = acc[0:dh] / jnp.maximum(acc[dh:dh + 1], 1e-30)

    own = (q0 + lax.broadcasted_iota(jnp.int32, (1, tq), 1)) // NSA_SEL_LEN
    sid = lax.broadcasted_iota(jnp.int32, (ns, tq), 0)
    forced = (sid == 0) | (sid == own) | (sid == own - 1)
    imp = jnp.where(forced, imp + SEL_FORCE, jnp.where(sid > own, -1.0, imp))
    sel_ref[...] = _top_k_rows(imp, k_top).astype(F32)

    def sel_rows(b0, nb):
        rows = sel_ref[pl.ds(pl.multiple_of(b0, bpt), nb), :]
        return jnp.concatenate([rows] * nr, axis=1)

    def scores(k0, nk):
        return jnp.dot(ks_ref[pl.ds(k0, nk), :], qt, preferred_element_type=F32) * c2 - bias_ref[0:nk, :]

    def blockwise(x):
        nb = x.shape[0]
        return jnp.broadcast_to(x[:, None, :], (nb, NSA_SEL_LEN, nl)).reshape(nb * NSA_SEL_LEN, nl)

    jd = q0 // tk
    kd = pl.multiple_of(jd * tk, tk)
    kpos = kd + lax.broadcasted_iota(jnp.int32, (tk, 1), 0)
    off = slope2 * (q0 - kd).astype(F32)
    ok = (blockwise(sel_rows(jd * bpt, bpt)) > 0.5) & (kpos <= tl)
    s = jnp.where(ok, scores(kd, tk), NEG_INF)
    m0 = jnp.max(s, axis=0, keepdims=True) - off
    acc0 = _pv_with_sums(vst_ref[:, pl.ds(kd, tk)], jnp.exp2(s - (m0 + off)))

    def sweep(i, live, carry):
        m_i, acc = carry
        k0 = pl.multiple_of(i * (2 * tk), 2 * tk)
        s = scores(k0, 2 * tk)
        off = slope2 * (q0 - k0).astype(F32)
        rows = jnp.concatenate([live] * nr, axis=1) > 0.5
        blk_max = jnp.max(s.reshape(2 * bpt, NSA_SEL_LEN, nl), axis=1)
        m_tile = jnp.max(jnp.where(rows, blk_max, NEG_INF), axis=0, keepdims=True)
        m_new = jnp.maximum(m_i, m_tile - off)
        shift = jnp.where(rows, m_new + off, -NEG_INF)
        upd = _pv_with_sums(vst_ref[:, pl.ds(k0, 2 * tk)], jnp.exp2(s - blockwise(shift)))
        return m_new, jnp.exp2(m_i - m_new) * acc + upd

    def body(i, carry):
        b0 = pl.multiple_of(i * (2 * bpt), 2 * bpt)
        bid = b0 + lax.broadcasted_iota(jnp.int32, (2 * bpt, 1), 0)
        live = jnp.where(bid < jd * bpt, sel_ref[pl.ds(b0, 2 * bpt), :], 0.0)
        return lax.cond(jnp.max(live) > 0.5, lambda c: sweep(i, live, c), lambda c: c, carry)

    _, acc = lax.fori_loop(0, (jd + 1) // 2, body, (m0, acc0))
    o_slc = acc[0:dh] / jnp.maximum(acc[dh:dh + 1], 1e-30)

    gt = gate_ref[0]
    for r in range(nr):
        sl = slice(r * tq, (r + 1) * tq)
        o = (gt[3 * r:3 * r + 1] * o_cmp[:, sl] + gt[3 * r + 1:3 * r + 2] * o_slc[:, sl]
             + gt[3 * r + 2:3 * r + 3] * o_win[:, sl])
        o_ref[:, r * dh:(r + 1) * dh] = o.T.astype(o_ref.dtype)


def _nsa(zt, zn, cmp_n, cmp_t, gates_t, slopes, *, batch, seq, qt_row, vst_row, vwt_row, ks_col, kw_col,
         tq=128, tk=512):
    dh = HEAD_DIM
    g = NSA_KV_GROUPS
    nr = NSA_GROUP_SIZE
    tk = min(tk, seq)
    assert seq % (2 * tk) == 0 and tk % tq == 0 and NSA_WINDOW % tq == 0
    assert NSA_WINDOW + tq <= min(seq, 2 * tk)
    nq = seq // tq
    nc = seq // NSA_CMP_STRIDE
    ns = seq // NSA_SEL_LEN
    k_top = min(NSA_N_SEL, ns)
    kern = functools.partial(_nsa_kernel, tq=tq, tk=tk, nc=nc, ns=ns, k_top=k_top, c2=dh ** -0.5 * LOG2E)
    key_blk = lambda col: pl.BlockSpec((seq, dh), lambda b, gg, i: (b, col + gg))
    val_blk = lambda row: pl.BlockSpec((dh, seq), lambda b, gg, i: (row + gg, b))
    return pl.pallas_call(
        kern,
        out_shape=jax.ShapeDtypeStruct((batch * seq, NSA_HEADS * dh), BF16),
        grid=(batch, g, nq),
        in_specs=[pl.BlockSpec(memory_space=pltpu.SMEM),
                  pl.BlockSpec((nr * dh, tq), lambda b, gg, i: (qt_row // nr + gg, b * nq + i)),
                  pl.BlockSpec((1, 1, nc, dh), lambda b, gg, i: (b, gg, 0, 0)),
                  pl.BlockSpec((1, 1, dh, nc), lambda b, gg, i: (b, g + gg, 0, 0)),
                  key_blk(ks_col), val_blk(vst_row), key_blk(kw_col), val_blk(vwt_row),
                  pl.BlockSpec((1, gates_t.shape[1], tq), lambda b, gg, i: (gg, 0, b * nq + i))],
        out_specs=pl.BlockSpec((tq, nr * dh), lambda b, gg, i: (b * nq + i, gg)),
        scratch_shapes=[pltpu.VMEM((ns, tq), F32),
                        pltpu.VMEM((2 * tk, nr * tq), F32),
                        pltpu.VMEM((ns, nc), BF16),
                        pltpu.VMEM((nc, nr * tq), F32)],
        compiler_params=_params("parallel", "parallel", "arbitrary"),
        name="nsa_attention",
    )(slopes, zt, cmp_n, cmp_t, zn, zt, zn, zt, gates_t)


def _conv_kernel(gb_ref, gc_ref, h_ref, gcp_ref, hp_ref, w_ref, o_ref, *, tiles_per_seq):
    i = pl.program_id(0)
    u = gc_ref[...] * h_ref[...]
    prev = gcp_ref[...] * hp_ref[...]
    prev = jnp.where(i % tiles_per_seq == 0, 0.0, prev)
    r = lax.broadcasted_iota(jnp.int32, u.shape, 0)
    u1 = jnp.where(r == 0, prev[7:8], pltpu.roll(u, 1, 0))
    u2 = jnp.where(r == 0, prev[6:7], jnp.where(r == 1, prev[7:8], pltpu.roll(u, 2, 0)))
    w = w_ref[...]
    y = gb_ref[...] * (w[0:1] * u2 + w[1:2] * u1 + w[2:3] * u)
    o_ref[...] = y.astype(o_ref.dtype)


def _short_conv(z, w, *, seq, gb_col, gc_col, h_col, tt=512, tc=512):
    m = z.shape[0]
    c = w.shape[1]
    tt = min(tt, seq)
    assert seq % tt == 0 and c % tc == 0 and tt % 8 == 0
    sub = tt // 8
    cur = lambda col: pl.BlockSpec((tt, tc), lambda i, j: (i, col + j))
    prv = lambda col: pl.BlockSpec((8, tc), lambda i, j: (jnp.maximum(i * sub - 1, 0), col + j))
    return pl.pallas_call(
        functools.partial(_conv_kernel, tiles_per_seq=seq // tt),
        out_shape=jax.ShapeDtypeStruct((m, c), BF16),
        grid=(m // tt, c // tc),
        in_specs=[cur(gb_col), cur(gc_col), cur(h_col), prv(gc_col), prv(h_col),
                  pl.BlockSpec((3, tc), lambda i, j: (0, j))],
        out_specs=pl.BlockSpec((tt, tc), lambda i, j: (i, j)),
        compiler_params=_params("parallel", "arbitrary"),
        name="short_conv",
    )(z, z, z, z, z, w)


def _gate_decay_kernel(x_ref, w1_ref, w2_ref, b_ref, o_ref):
    za = jnp.dot(x_ref[...], w1_ref[...], preferred_element_type=F32)
    pre = jnp.dot(za.astype(BF16), w2_ref[...], preferred_element_type=F32) + b_ref[...]
    ls = -(jnp.maximum(-pre, 0.0) + jnp.log1p(jnp.exp(-jnp.abs(pre))))
    o_ref[...] = ls / GLA_GATE_TAU


def _gate_decay(x, w1, w2, b, *, tm=1024):
    m, k = x.shape
    r = w1.shape[1]
    n = w2.shape[1]
    tm = min(tm, m)
    return pl.pallas_call(
        _gate_decay_kernel,
        out_shape=jax.ShapeDtypeStruct((m, n), F32),
        grid=(m // tm,),
        in_specs=[pl.BlockSpec((tm, k), lambda i: (i, 0)),
                  pl.BlockSpec((k, r), lambda i: (0, 0)),
                  pl.BlockSpec((r, n), lambda i: (0, 0)),
                  pl.BlockSpec((1, n), lambda i: (0, 0))],
        out_specs=pl.BlockSpec((tm, n), lambda i: (i, 0)),
        compiler_params=_params("parallel"),
        name="gla_gate_decay",
    )(x, w1, w2, b.reshape(1, n))


def _gla_kernel(q_ref, k_ref, v_ref, g_ref, la_ref, ng_ref, o_ref, st_ref, *, tc):
    L = GLA_CHUNK
    dk, dv = GLA_DK, GLA_DV
    hp = q_ref.shape[-1] // dk

    @pl.when(pl.program_id(2) == 0)
    def _():
        st_ref[...] = jnp.zeros_like(st_ref)

    ri = lax.broadcasted_iota(jnp.int32, (L, L), 0)
    ci = lax.broadcasted_iota(jnp.int32, (L, L), 1)
    causal = ci <= ri
    tri = causal.astype(F32)
    ng = ng_ref[...]

    for c in range(tc // L):
        rs = slice(c * L, (c + 1) * L)
        b = jnp.dot(tri, la_ref[rs, :], preferred_element_type=F32, precision=lax.Precision.HIGHEST)
        b_last = b[L - 1:L, :]
        eb = jnp.exp(b)
        q_t = (q_ref[rs, :] * dk ** -0.5) * eb
        k_raw = k_ref[rs, :]
        k_t = k_raw * jnp.exp(-b)
        k_d = k_raw * jnp.exp(b_last - b)
        dec = jnp.exp(b_last)
        for h in range(hp):
            ks = slice(h * dk, (h + 1) * dk)
            vs = slice(h * dv, (h + 1) * dv)
            qh = q_t[:, ks].astype(BF16)
            vh = v_ref[rs, vs].astype(BF16)
            att = lax.dot_general(qh, k_t[:, ks].astype(BF16), NT_DIMS, preferred_element_type=F32)
            att = jnp.where(causal, att, 0.0)
            o = jnp.dot(att.astype(BF16), vh, preferred_element_type=F32)
            st = st_ref[h]
            o = o + lax.dot_general(qh, st.astype(BF16), NT_DIMS, preferred_element_type=F32)
            u_t = lax.dot_general(vh, k_d[:, ks].astype(BF16), TN_DIMS, preferred_element_type=F32)
            st_ref[h] = st * dec[:, ks] + u_t
            o = o * lax.rsqrt(jnp.mean(o * o, axis=-1, keepdims=True) + NORM_EPS) * ng
            o_ref[rs, vs] = (o * jax.nn.silu(g_ref[rs, vs])).astype(o_ref.dtype)


def _gla(z, la, norm_g, *, batch, seq, q_col, k_col, v_col, g_col, tc=256, hp=2):
    dk, dv = GLA_DK, GLA_DV
    tc = min(tc, seq)
    assert seq % tc == 0 and tc % GLA_CHUNK == 0 and GLA_HEADS % hp == 0
    nt = seq // tc
    qk = lambda col: pl.BlockSpec((tc, hp * dk), lambda b, h, i: (b * nt + i, col + h))
    vg = lambda col: pl.BlockSpec((tc, hp * dv), lambda b, h, i: (b * nt + i, col + h))
    return pl.pallas_call(
        functools.partial(_gla_kernel, tc=tc),
        out_shape=jax.ShapeDtypeStruct((batch * seq, GLA_HEADS * dv), BF16),
        grid=(batch, GLA_HEADS // hp, nt),
        in_specs=[qk(q_col), qk(k_col), vg(v_col), vg(g_col), qk(0),
                  pl.BlockSpec((1, dv), lambda b, h, i: (0, 0))],
        out_specs=pl.BlockSpec((tc, hp * dv), lambda b, h, i: (b * nt + i, h)),
        scratch_shapes=[pltpu.VMEM((hp, dv, dk), F32)],
        compiler_params=_params("parallel", "parallel", "arbitrary"),
        name="gla",
    )(z, z, z, z, la, norm_g.reshape(1, dv))


def _pad_cols(w, n):
    return jnp.pad(w, ((0, 0), (0, n - w.shape[1])))


def _ffn_sublayer(xf, xb, wg, wu, wd_b, layer, ln_g, ln_b):
    gu = _ffn_gu(xb, wg, wu, layer)
    h = _matmul(gu, wd_b, F32, layer=layer, tm=512, tn=512)
    return _res_ln(xf, h, ln_g, ln_b, 0.5)


def _alibi_slopes(n):
    return jnp.exp2(-8.0 * jnp.arange(1, n + 1, dtype=F32) / n)


def _attn_sublayer(xf, xb, w_in, w_out, pos_k, w1_k, w2_k, pos_v, w1_v, w2_v, ln_g, ln_b, *, batch, seq):
    dh = HEAD_DIM
    nq = NSA_HEADS * dh
    kv = NSA_KV_GROUPS * dh
    nm = MOBA_HEADS * dh
    n_gate = NSA_HEADS * 3
    c_kc, c_ks, c_vs, c_kw, c_vw, c_gate = (nq + i * kv for i in (0, 2, 3, 4, 5, 6))
    c_mq = c_gate + n_gate
    c_mk, c_mv = c_mq + nm, c_mq + 2 * nm
    col = lambda a, n: w_in[:, a:a + n]
    w_n = jnp.concatenate([col(c_ks, kv), col(c_kw, kv), col(c_mk, nm)], axis=1).astype(BF16)
    w_t = jnp.concatenate([col(0, nq), col(c_vs, kv), col(c_vw, kv), col(c_mq, nm), col(c_mv, nm)],
                          axis=1).T.astype(BF16)
    w_f = col(c_kc, 2 * kv).astype(BF16)
    w_g = _pad_cols(col(c_gate, n_gate), LANE).T.astype(BF16)
    zn = _matmul(xb, w_n, BF16)
    zt = _matmul_nt(w_t, xb, BF16)
    zf = _matmul(xb, w_f, F32)
    gates_t = _matmul_nt(w_g, xb, F32, act="sigmoid")[:n_gate]
    gates_t = jnp.pad(gates_t.reshape(NSA_KV_GROUPS, 3 * NSA_GROUP_SIZE, batch * seq), ((0, 0), (0, 4), (0, 0)))

    slopes = _alibi_slopes(N_ATTN_HEADS)
    cmp_n, cmp_t = _nsa_compress(zf, jnp.stack([pos_k, pos_v]), jnp.stack([w1_k, w1_v]).astype(BF16),
                                 jnp.stack([w2_k, w2_v]).astype(BF16), batch=batch, seq=seq)
    u = nq // dh
    o_nsa = _nsa(zt, zn, cmp_n, cmp_t, gates_t, slopes[0::2], batch=batch, seq=seq,
                 qt_row=0, vst_row=u, vwt_row=u + 4, ks_col=0, kw_col=4)
    o_moba = _moba(zt, zn, slopes[1::2], batch=batch, seq=seq, qt_row=u + 8, vt_row=u + 24, k_col=8)
    w_o = w_out.astype(BF16)
    y = _matmul2(o_nsa, o_moba, w_o[:nq], w_o[nq:], F32)
    return _res_ln(xf, y, ln_g, ln_b, 1.0)


def _mix_sublayer(xf, xb, w_in, w_out, conv_w, w_a2, b_a, norm_g, ln_g, ln_b, *, batch, seq):
    cc = CONV_CHANNELS
    hk = GLA_HEADS * GLA_DK
    hv = GLA_HEADS * GLA_DV
    c_za = 3 * cc + 2 * hk + 2 * hv
    z = _matmul(xb, w_in[:, :c_za].astype(BF16), F32)
    w_za = _pad_cols(w_in[:, c_za:], LANE).astype(BF16)
    w_a2p = jnp.pad(w_a2, ((0, LANE - w_a2.shape[0]), (0, 0))).astype(BF16)
    la = _gate_decay(xb, w_za, w_a2p, b_a)
    y_conv = _short_conv(z, conv_w, seq=seq, gb_col=0, gc_col=cc // 512, h_col=2 * cc // 512)
    hp = 2
    y_gla = _gla(z, la, norm_g, batch=batch, seq=seq, q_col=3 * cc // (hp * GLA_DK),
                 k_col=(3 * cc + hk) // (hp * GLA_DK), v_col=(3 * cc + 2 * hk) // (hp * GLA_DV),
                 g_col=(3 * cc + 2 * hk + hv) // (hp * GLA_DV), hp=hp)
    w_o = w_out.astype(BF16)
    y = _matmul2(y_conv, y_gla, w_o[:cc], w_o[cc:], F32)
    return _res_ln(xf, y, ln_g, ln_b, 1.0)


def kernel(x, ln_g, ln_b, ffn_pre_wg, ffn_pre_wu, ffn_pre_wd, ffn_post_wg, ffn_post_wu, ffn_post_wd,
           att_w_in, att_w_out, nsa_pos_k, nsa_w1_k, nsa_w2_k, nsa_pos_v, nsa_w1_v, nsa_w2_v,
           mix_w_in, mix_w_out, conv_w, gla_w_a2, gla_b_a, gla_norm_g):
    batch, seq, d = x.shape
    xf = x.reshape(batch * seq, d)
    xb = xf.astype(BF16)
    pre_wd, post_wd = ffn_pre_wd.astype(BF16), ffn_post_wd.astype(BF16)
    for layer in range(DEPTH):
        xf, xb = _ffn_sublayer(xf, xb, ffn_pre_wg, ffn_pre_wu, pre_wd, layer, ln_g[layer, 0], ln_b[layer, 0])
        i = layer // 2
        if layer % 2 == 0:
            xf, xb = _attn_sublayer(xf, xb, att_w_in[i], att_w_out[i], nsa_pos_k[i], nsa_w1_k[i], nsa_w2_k[i],
                                    nsa_pos_v[i], nsa_w1_v[i], nsa_w2_v[i], ln_g[layer, 1], ln_b[layer, 1],
                                    batch=batch, seq=seq)
        else:
            xf, xb = _mix_sublayer(xf, xb, mix_w_in[i], mix_w_out[i], conv_w[i], gla_w_a2[i], gla_b_a[i],
                                   gla_norm_g[i], ln_g[layer, 1], ln_b[layer, 1], batch=batch, seq=seq)
        xf, xb = _ffn_sublayer(xf, xb, ffn_post_wg, ffn_post_wu, post_wd, layer, ln_g[layer, 2], ln_b[layer, 2])
    return xf.reshape(batch, seq, d)
```

```python
import functools
import math

import jax
import jax.numpy as jnp
from jax import lax
from jax.experimental import pallas as pl
from jax.experimental.pallas import tpu as pltpu

F32 = jnp.float32
BF16 = jnp.bfloat16

D_MODEL = 4096
DEPTH = 2
HEAD_DIM = 128
NSA_HEADS = 16
NSA_KV_GROUPS = 4
NSA_GROUP_SIZE = 4
NSA_CMP_STRIDE = 16
NSA_CMP_LEN = 32
NSA_SEL_LEN = 64
NSA_N_SEL = 16
NSA_WINDOW = 512
MOBA_HEADS = 16
MOBA_BLOCK = 256
MOBA_TOPK = 3
N_ATTN_HEADS = NSA_HEADS + MOBA_HEADS
CONV_CHANNELS = 2048
GLA_HEADS = 16
GLA_DK = 64
GLA_DV = 128
GLA_GATE_RANK = 16
GLA_GATE_TAU = 16.0
GLA_CHUNK = 64
D_FF = 11008
ALPHA = (2 * DEPTH) ** 0.25
LN_EPS = 1e-5
NORM_EPS = 1e-6
NEG_INF = -1e30
SEL_FORCE = 1e4
LOG2E = math.log2(math.e)

LANE = 128
ONES_ROWS = 16
VMEM_LIMIT = 56 * 1024 * 1024

NT_DIMS = (((1,), (1,)), ((), ()))
TN_DIMS = (((0,), (0,)), ((), ()))


def _round_up(n, m):
    return (n + m - 1) // m * m


def _params(*sem):
    return pltpu.CompilerParams(dimension_semantics=sem, vmem_limit_bytes=VMEM_LIMIT)


def _top_k_rows(vals, k):
    n = vals.shape[0]
    rid = lax.broadcasted_iota(jnp.int32, vals.shape, 0)
    sel = jnp.zeros(vals.shape, jnp.bool_)
    g = vals
    for _ in range(k):
        m = jnp.max(g, axis=0, keepdims=True)
        idx = jnp.min(jnp.where(g == m, rid, n), axis=0, keepdims=True)
        pick = rid == idx
        sel = sel | pick
        g = jnp.where(pick, -jnp.inf, g)
    return sel


def _pv_with_sums(vt, p):
    ones = jnp.ones((ONES_ROWS, vt.shape[1]), BF16)
    return jnp.dot(jnp.concatenate([vt, ones], axis=0), p.astype(BF16), preferred_element_type=F32)


def _mm_kernel(a_ref, b_ref, o_ref):
    o_ref[...] = jnp.dot(a_ref[...], b_ref[0], preferred_element_type=F32).astype(o_ref.dtype)


def _matmul(a, b, out_dtype, *, layer=0, tm=1024, tn=1024):
    if b.ndim == 2:
        b = b[None]
    m, k = a.shape
    n = b.shape[2]
    tm, tn = min(tm, m), min(tn, n)
    assert m % tm == 0 and n % tn == 0
    return pl.pallas_call(
        _mm_kernel,
        out_shape=jax.ShapeDtypeStruct((m, n), out_dtype),
        grid=(m // tm, n // tn),
        in_specs=[pl.BlockSpec((tm, k), lambda i, j: (i, 0)),
                  pl.BlockSpec((1, k, tn), lambda i, j: (layer, 0, j))],
        out_specs=pl.BlockSpec((tm, tn), lambda i, j: (i, j)),
        compiler_params=_params("parallel", "arbitrary"),
        name="matmul",
    )(a, b)


def _mm_nt_kernel(w_ref, x_ref, o_ref, *, act):
    r = lax.dot_general(w_ref[...], x_ref[...], NT_DIMS, preferred_element_type=F32)
    if act == "sigmoid":
        r = jax.nn.sigmoid(r)
    o_ref[...] = r.astype(o_ref.dtype)


def _matmul_nt(wt, x, out_dtype, *, tn=1024, tm=1024, act=None):
    n, k = wt.shape
    m = x.shape[0]
    tn, tm = min(tn, n), min(tm, m)
    assert m % tm == 0 and n % tn == 0
    return pl.pallas_call(
        functools.partial(_mm_nt_kernel, act=act),
        out_shape=jax.ShapeDtypeStruct((n, m), out_dtype),
        grid=(m // tm, n // tn),
        in_specs=[pl.BlockSpec((tn, k), lambda i, j: (j, 0)),
                  pl.BlockSpec((tm, k), lambda i, j: (i, 0))],
        out_specs=pl.BlockSpec((tn, tm), lambda i, j: (j, i)),
        compiler_params=_params("parallel", "arbitrary"),
        name="matmul_nt",
    )(wt, x)


def _mm2_kernel(a1_ref, a2_ref, b1_ref, b2_ref, o_ref):
    r = jnp.dot(a1_ref[...], b1_ref[...], preferred_element_type=F32)
    r = r + jnp.dot(a2_ref[...], b2_ref[...], preferred_element_type=F32)
    o_ref[...] = r.astype(o_ref.dtype)


def _matmul2(a1, a2, b1, b2, out_dtype, *, tm=1024, tn=1024):
    m, k1 = a1.shape
    k2 = a2.shape[1]
    n = b1.shape[1]
    tm, tn = min(tm, m), min(tn, n)
    assert m % tm == 0 and n % tn == 0
    return pl.pallas_call(
        _mm2_kernel,
        out_shape=jax.ShapeDtypeStruct((m, n), out_dtype),
        grid=(m // tm, n // tn),
        in_specs=[pl.BlockSpec((tm, k1), lambda i, j: (i, 0)),
                  pl.BlockSpec((tm, k2), lambda i, j: (i, 0)),
                  pl.BlockSpec((k1, tn), lambda i, j: (0, j)),
                  pl.BlockSpec((k2, tn), lambda i, j: (0, j))],
        out_specs=pl.BlockSpec((tm, tn), lambda i, j: (i, j)),
        compiler_params=_params("parallel", "arbitrary"),
        name="matmul2",
    )(a1, a2, b1, b2)


def _ffn_gu_kernel(x_ref, wg_ref, wu_ref, o_ref, wb_ref):
    tn = o_ref.shape[1]

    @pl.when(pl.program_id(1) == 0)
    def _():
        wb_ref[:, 0:tn] = wg_ref[0].astype(BF16)
        wb_ref[:, tn:2 * tn] = wu_ref[0].astype(BF16)

    r = jnp.dot(x_ref[...], wb_ref[...], preferred_element_type=F32)
    o_ref[...] = (jax.nn.silu(r[:, 0:tn]) * r[:, tn:2 * tn]).astype(o_ref.dtype)


def _ffn_gu(x, wg, wu, layer, *, tm=1024, tn=256):
    m, k = x.shape
    n = wg.shape[2]
    tm, tn = min(tm, m), min(tn, n)
    assert m % tm == 0 and n % tn == 0
    w_spec = pl.BlockSpec((1, k, tn), lambda j, i: (layer, 0, j))
    return pl.pallas_call(
        _ffn_gu_kernel,
        out_shape=jax.ShapeDtypeStruct((m, n), BF16),
        grid=(n // tn, m // tm),
        in_specs=[pl.BlockSpec((tm, k), lambda j, i: (i, 0)), w_spec, w_spec],
        out_specs=pl.BlockSpec((tm, tn), lambda j, i: (i, j)),
        scratch_shapes=[pltpu.VMEM((k, 2 * tn), BF16)],
        compiler_params=_params("parallel", "arbitrary"),
        name="ffn_gate_up",
    )(x, wg, wu)


def _res_ln_kernel(x_ref, h_ref, g_ref, b_ref, of_ref, ob_ref, *, coef):
    v = ALPHA * x_ref[...] + coef * h_ref[...]
    mu = jnp.mean(v, axis=-1, keepdims=True)
    d = v - mu
    var = jnp.mean(d * d, axis=-1, keepdims=True)
    y = d * lax.rsqrt(var + LN_EPS) * g_ref[...] + b_ref[...]
    of_ref[...] = y
    ob_ref[...] = y.astype(BF16)


def _res_ln(x, h, g, b, coef, *, tm=256):
    m, d = x.shape
    tm = min(tm, m)
    assert m % tm == 0
    row = pl.BlockSpec((tm, d), lambda i: (i, 0))
    vec = pl.BlockSpec((1, d), lambda i: (0, 0))
    return pl.pallas_call(
        functools.partial(_res_ln_kernel, coef=coef),
        out_shape=(jax.ShapeDtypeStruct((m, d), F32), jax.ShapeDtypeStruct((m, d), BF16)),
        grid=(m // tm,),
        in_specs=[row, row, vec, vec],
        out_specs=(row, row),
        compiler_params=_params("parallel"),
        name="residual_layernorm",
    )(x, h, g.reshape(1, d), b.reshape(1, d))


def _moba_kernel(slope_ref, qt_ref, k_ref, vt_ref, o_ref, kmean_ref, bias_ref, sel_ref,
                 *, nblk, blk, n_top, c2, hp):
    hg = pl.program_id(1)
    qi = pl.program_id(2)
    dh = HEAD_DIM
    kc = lax.broadcasted_iota(jnp.int32, (blk, blk), 0)
    qr = lax.broadcasted_iota(jnp.int32, (blk, blk), 1)
    slope2 = [slope_ref[hg * hp + h] * LOG2E for h in range(hp)]
    cols = [slice(h * dh, (h + 1) * dh) for h in range(hp)]

    @pl.when(qi == 0)
    def _():
        kc2 = lax.broadcasted_iota(jnp.int32, (2 * blk, blk), 0)
        qr2 = lax.broadcasted_iota(jnp.int32, (2 * blk, blk), 1)
        for h in range(hp):
            kf = k_ref[:, cols[h]].astype(F32).reshape(nblk, blk, dh)
            kmean_ref[h] = (jnp.sum(kf, axis=1) / blk).astype(BF16)
            bias_ref[h] = slope2[h] * (qr2 - kc2).astype(F32)

    qt = [qt_ref[cols[h], :] for h in range(hp)]
    gate = [jnp.dot(kmean_ref[h], qt[h], preferred_element_type=F32) for h in range(hp)]
    past = lax.broadcasted_iota(jnp.int32, (nblk, blk), 0) < qi
    for h in range(hp):
        sel = _top_k_rows(jnp.where(past, gate[h], NEG_INF), n_top) & past
        sel_ref[h] = sel.astype(F32)

    def scores(h, k0, nk):
        return (jnp.dot(k_ref[pl.ds(k0, nk), cols[h]], qt[h], preferred_element_type=F32) * c2
                - bias_ref[h, 0:nk, :])

    def pv(h, k0, nk, p):
        return _pv_with_sums(vt_ref[cols[h], pl.ds(k0, nk)], p)

    kq = pl.multiple_of(qi * blk, blk)
    s0 = [jnp.where(kc <= qr, scores(h, kq, blk), NEG_INF) for h in range(hp)]
    m0 = [jnp.max(s0[h], axis=0, keepdims=True) for h in range(hp)]
    acc0 = [pv(h, kq, blk, jnp.exp2(s0[h] - m0[h])) for h in range(hp)]
    init = []
    for h in range(hp):
        init += [m0[h], acc0[h]]

    def body(t, carry):
        j = qi - 1 - 2 * t
        pb = jnp.maximum(j - 1, 0)
        k0 = pl.multiple_of(pb * blk, blk)
        s = [scores(h, k0, 2 * blk) for h in range(hp)]
        valid = pb + lax.broadcasted_iota(jnp.int32, (2, 1), 0) <= j
        m_new, scale, p = [], [], []
        for h in range(hp):
            m_i = carry[2 * h]
            off = slope2[h] * ((qi - pb) * blk).astype(F32)
            rows = jnp.concatenate([sel_ref[h, pl.ds(pb, 1), :], sel_ref[h, pl.ds(pb + 1, 1), :]], axis=0)
            picked = (rows > 0.5) & valid
            blk_max = jnp.max(s[h].reshape(2, blk, blk), axis=1)
            m_pair = jnp.max(jnp.where(picked, blk_max, NEG_INF), axis=0, keepdims=True)
            m_h = jnp.maximum(m_i, m_pair - off)
            shift = jnp.where(picked, m_h + off, -NEG_INF)
            shift = jnp.broadcast_to(shift[:, None, :], (2, blk, blk)).reshape(2 * blk, blk)
            m_new.append(m_h)
            scale.append(jnp.exp2(m_i - m_h))
            p.append(jnp.exp2(s[h] - shift))
        upd = [pv(h, k0, 2 * blk, p[h]) for h in range(hp)]
        out = []
        for h in range(hp):
            out += [m_new[h], scale[h] * carry[2 * h + 1] + upd[h]]
        return tuple(out)

    fin = lax.fori_loop(0, (qi + 1) // 2, body, tuple(init))
    for h in range(hp):
        acc = fin[2 * h + 1]
        o_t = acc[0:dh] / jnp.maximum(acc[dh:dh + 1], 1e-30)
        o_ref[:, cols[h]] = o_t.T.astype(o_ref.dtype)


def _moba(zt, zn, slopes, *, batch, seq, qt_row, vt_row, k_col, hp=4):
    blk = MOBA_BLOCK
    assert seq % blk == 0 and MOBA_HEADS % hp == 0
    assert qt_row % hp == 0 and vt_row % hp == 0 and k_col % hp == 0
    nblk = seq // blk
    n_top = min(MOBA_TOPK, nblk - 1)
    assert n_top > 0
    dh = HEAD_DIM
    kern = functools.partial(_moba_kernel, nblk=nblk, blk=blk, n_top=n_top, c2=dh ** -0.5 * LOG2E, hp=hp)
    return pl.pallas_call(
        kern,
        out_shape=jax.ShapeDtypeStruct((batch * seq, MOBA_HEADS * dh), BF16),
        grid=(batch, MOBA_HEADS // hp, nblk),
        in_specs=[pl.BlockSpec(memory_space=pltpu.SMEM),
                  pl.BlockSpec((hp * dh, blk), lambda b, h, i: (qt_row // hp + h, b * nblk + i)),
                  pl.BlockSpec((seq, hp * dh), lambda b, h, i: (b, k_col // hp + h)),
                  pl.BlockSpec((hp * dh, seq), lambda b, h, i: (vt_row // hp + h, b))],
        out_specs=pl.BlockSpec((blk, hp * dh), lambda b, h, i: (b * nblk + i, h)),
        scratch_shapes=[pltpu.VMEM((hp, nblk, dh), BF16),
                        pltpu.VMEM((hp, 2 * blk, blk), F32),
                        pltpu.VMEM((hp, nblk, blk), F32)],
        compiler_params=_params("parallel", "parallel", "arbitrary"),
        name="moba_attention",
    )(slopes, zt, zn, zt)


def _compress_kernel(x_ref, pos_ref, w1_ref, w2_ref, o_ref, ot_ref, *, nc):
    dh = x_ref.shape[-1]
    half = NSA_CMP_STRIDE
    acc_lo = jnp.zeros((nc, dh), F32)
    acc_hi = jnp.zeros((nc, dh), F32)
    for p in range(half):
        xp = x_ref[pl.ds(p, nc, stride=half), :]
        lo = (xp + pos_ref[0, p:p + 1, :]).astype(BF16)
        hi = (xp + pos_ref[0, half + p:half + p + 1, :]).astype(BF16)
        acc_lo += jnp.dot(lo, w1_ref[0, p * dh:(p + 1) * dh, :], preferred_element_type=F32)
        acc_hi += jnp.dot(hi, w1_ref[0, (half + p) * dh:(half + p + 1) * dh, :], preferred_element_type=F32)
    pre = acc_lo + pltpu.roll(acc_hi, nc - 1, 0)
    hid = jax.nn.gelu(pre)
    out = jnp.dot(hid.astype(BF16), w2_ref[0], preferred_element_type=F32)
    o_ref[0, 0] = out.astype(o_ref.dtype)
    ot_ref[0, 0] = out.T.astype(ot_ref.dtype)


def _nsa_compress(zf, pos, w1, w2, *, batch, seq):
    dh = HEAD_DIM
    g = NSA_KV_GROUPS
    nc = seq // NSA_CMP_STRIDE
    return pl.pallas_call(
        functools.partial(_compress_kernel, nc=nc),
        out_shape=(jax.ShapeDtypeStruct((batch, 2 * g, nc, dh), BF16),
                   jax.ShapeDtypeStruct((batch, 2 * g, dh, nc), BF16)),
        grid=(batch, 2 * g),
        in_specs=[pl.BlockSpec((seq, dh), lambda b, c: (b, c)),
                  pl.BlockSpec((1, NSA_CMP_LEN, dh), lambda b, c: (c // g, 0, 0)),
                  pl.BlockSpec((1, NSA_CMP_LEN * dh, dh), lambda b, c: (c // g, 0, 0)),
                  pl.BlockSpec((1, dh, dh), lambda b, c: (c // g, 0, 0))],
        out_specs=(pl.BlockSpec((1, 1, nc, dh), lambda b, c: (b, c, 0, 0)),
                   pl.BlockSpec((1, 1, dh, nc), lambda b, c: (b, c, 0, 0))),
        compiler_params=_params("parallel", "arbitrary"),
        name="nsa_compress",
    )(zf, pos, w1, w2)


def _nsa_kernel(slope_ref, qt_ref, kc_ref, vct_ref, ks_ref, vst_ref, kw_ref, vwt_ref, gate_ref, o_ref,
                sel_ref, bias_ref, member_ref, ecmp_ref, flag_ref, *, tq, tk, nc, ns, k_top, c2):
    g = pl.program_id(1)
    qi = pl.program_id(2)
    dh = HEAD_DIM
    nr = NSA_GROUP_SIZE
    nl = nr * tq
    bpt = tk // NSA_SEL_LEN
    q0 = qi * tq

    lane = lax.broadcasted_iota(jnp.int32, (1, nl), 1)
    tl = q0 + lane % tq
    head = lane // tq
    slope2 = jnp.zeros((1, nl), F32)
    for r in range(nr):
        slope2 = jnp.where(head == r, slope_ref[g * nr + r] * LOG2E, slope2)

    @pl.when(qi == 0)
    def _():
        kc_i = lax.broadcasted_iota(jnp.int32, (2 * tk, nl), 0)
        bias_ref[...] = slope2 * (lane % tq - kc_i).astype(F32)
        sj = lax.broadcasted_iota(jnp.int32, (ns, nc), 0) * NSA_SEL_LEN
        ci = lax.broadcasted_iota(jnp.int32, (ns, nc), 1) * NSA_CMP_STRIDE
        member_ref[...] = ((ci < sj + NSA_SEL_LEN) & (ci + NSA_CMP_LEN > sj)).astype(BF16)
        ecmp_ref[...] = slope2 * (lax.broadcasted_iota(jnp.int32, (nc, nl), 0) * NSA_CMP_STRIDE
                                  + (NSA_CMP_LEN - 1)).astype(F32)

    qb = qt_ref[...]
    qt = jnp.concatenate([qb[r * dh:(r + 1) * dh, :] for r in range(nr)], axis=1)

    span = NSA_WINDOW + tq
    w0 = pl.multiple_of(jnp.maximum(q0 - NSA_WINDOW, 0), tq)
    raw_cmp = jnp.dot(kc_ref[0, 0], qt, preferred_element_type=F32)
    raw_win = jnp.dot(kw_ref[pl.ds(w0, span), :], qt, preferred_element_type=F32)

    cend = lax.broadcasted_iota(jnp.int32, (nc, 1), 0) * NSA_CMP_STRIDE + (NSA_CMP_LEN - 1)
    s = (raw_cmp * c2 + ecmp_ref[...]) - slope2 * tl.astype(F32)
    vis = cend <= tl
    s = jnp.where(vis, s, NEG_INF)
    e = jnp.where(vis, jnp.exp2(s - jnp.max(s, axis=0, keepdims=True)), 0.0)
    p_cmp = e * (1.0 / jnp.maximum(jnp.sum(e, axis=0, keepdims=True), 1e-30))
    o_cmp = jnp.dot(vct_ref[0, 0], p_cmp.astype(BF16), preferred_element_type=F32)

    p_sum = p_cmp[:, 0:tq]
    for r in range(1, nr):
        p_sum = p_sum + p_cmp[:, r * tq:(r + 1) * tq]
    p_hi = p_sum.astype(BF16)
    rest = p_sum - p_hi.astype(F32)
    p_mid = rest.astype(BF16)
    p_lo = (rest - p_mid.astype(F32)).astype(BF16)
    member = member_ref[...]
    imp = (jnp.dot(member, p_hi, preferred_element_type=F32) + jnp.dot(member, p_mid, preferred_element_type=F32)
           + jnp.dot(member, p_lo, preferred_element_type=F32))

    rel = (w0 + lax.broadcasted_iota(jnp.int32, (span, 1), 0)) - tl
    s = (raw_win * c2 - bias_ref[0:span, :]) + slope2 * (w0 - q0).astype(F32)
    s = jnp.where((rel <= 0) & (rel > -NSA_WINDOW), s, NEG_INF)
    acc = _pv_with_sums(vwt_ref[:, pl.ds(w0, span)], jnp.exp2(s - jnp.max(s, axis=0, keepdims=True)))
    o_win = acc[0:dh] / jnp.maximum(acc[dh:dh + 1], 1e-30)

    own = (q0 + lax.broadcasted_iota(jnp.int32, (1, tq), 1)) // NSA_SEL_LEN
    sid = lax.broadcasted_iota(jnp.int32, (ns, tq), 0)
    forced = (sid == 0) | (sid == own) | (sid == own - 1)
    imp = jnp.where(forced, imp + SEL_FORCE, jnp.where(sid > own, -1.0, imp))
    sel_ref[...] = _top_k_rows(imp, k_top).astype(F32)

    def sel_rows(b0, nb):
        rows = sel_ref[pl.ds(pl.multiple_of(b0, bpt), nb), :]
        return jnp.concatenate([rows] * nr, axis=1)

    def scores(k0, nk):
        return jnp.dot(ks_ref[pl.ds(k0, nk), :], qt, preferred_element_type=F32) * c2 - bias_ref[0:nk, :]

    def blockwise(x):
        nb = x.shape[0]
        return jnp.broadcast_to(x[:, None, :], (nb, NSA_SEL_LEN, nl)).reshape(nb * NSA_SEL_LEN, nl)

    jd = q0 // tk
    kd = pl.multiple_of(jd * tk, tk)
    kpos = kd + lax.broadcasted_iota(jnp.int32, (tk, 1), 0)
    off = slope2 * (q0 - kd).astype(F32)
    ok = (blockwise(sel_rows(jd * bpt, bpt)) > 0.5) & (kpos <= tl)
    s = jnp.where(ok, scores(kd, tk), NEG_INF)
    m0 = jnp.max(s, axis=0, keepdims=True) - off
    acc0 = _pv_with_sums(vst_ref[:, pl.ds(kd, tk)], jnp.exp2(s - (m0 + off)))

    def sweep(i, live, carry):
        m_i, acc = carry
        k0 = pl.multiple_of(i * (2 * tk), 2 * tk)
        s = scores(k0, 2 * tk)
        off = slope2 * (q0 - k0).astype(F32)
        rows = jnp.concatenate([live] * nr, axis=1) > 0.5
        blk_max = jnp.max(s.reshape(2 * bpt, NSA_SEL_LEN, nl), axis=1)
        m_tile = jnp.max(jnp.where(rows, blk_max, NEG_INF), axis=0, keepdims=True)
        m_new = jnp.maximum(m_i, m_tile - off)
        shift = jnp.where(rows, m_new + off, -NEG_INF)
        upd = _pv_with_sums(vst_ref[:, pl.ds(k0, 2 * tk)], jnp.exp2(s - blockwise(shift)))
        return m_new, jnp.exp2(m_i - m_new) * acc + upd

    vis_sel = jnp.where(sid < jd * bpt, sel_ref[...], 0.0)
    pair_any = jnp.max(jnp.max(vis_sel.reshape(ns // (2 * bpt), 2 * bpt, tq), axis=1), axis=1, keepdims=True)
    for i in range(ns // (2 * bpt)):
        flag_ref[i] = (pair_any[i, 0] > 0.5).astype(jnp.int32)

    def body(i, carry):
        b0 = pl.multiple_of(i * (2 * bpt), 2 * bpt)
        bid = b0 + lax.broadcasted_iota(jnp.int32, (2 * bpt, 1), 0)
        live = jnp.where(bid < jd * bpt, sel_ref[pl.ds(b0, 2 * bpt), :], 0.0)
        return lax.cond(flag_ref[i] > 0, lambda c: sweep(i, live, c), lambda c: c, carry)

    _, acc = lax.fori_loop(0, (jd + 1) // 2, body, (m0, acc0))
    o_slc = acc[0:dh] / jnp.maximum(acc[dh:dh + 1], 1e-30)

    gt = gate_ref[0]
    for r in range(nr):
        sl = slice(r * tq, (r + 1) * tq)
        o = (gt[3 * r:3 * r + 1] * o_cmp[:, sl] + gt[3 * r + 1:3 * r + 2] * o_slc[:, sl]
             + gt[3 * r + 2:3 * r + 3] * o_win[:, sl])
        o_ref[:, r * dh:(r + 1) * dh] = o.T.astype(o_ref.dtype)


def _nsa(zt, zn, cmp_n, cmp_t, gates_t, slopes, *, batch, seq, qt_row, vst_row, vwt_row, ks_col, kw_col,
         tq=128, tk=512):
    dh = HEAD_DIM
    g = NSA_KV_GROUPS
    nr = NSA_GROUP_SIZE
    tk = min(tk, seq)
    assert seq % (2 * tk) == 0 and tk % tq == 0 and NSA_WINDOW % tq == 0
    assert NSA_WINDOW + tq <= min(seq, 2 * tk)
    nq = seq // tq
    nc = seq // NSA_CMP_STRIDE
    ns = seq // NSA_SEL_LEN
    k_top = min(NSA_N_SEL, ns)
    kern = functools.partial(_nsa_kernel, tq=tq, tk=tk, nc=nc, ns=ns, k_top=k_top, c2=dh ** -0.5 * LOG2E)
    key_blk = lambda col: pl.BlockSpec((seq, dh), lambda b, gg, i: (b, col + gg))
    val_blk = lambda row: pl.BlockSpec((dh, seq), lambda b, gg, i: (row + gg, b))
    return pl.pallas_call(
        kern,
        out_shape=jax.ShapeDtypeStruct((batch * seq, NSA_HEADS * dh), BF16),
        grid=(batch, g, nq),
        in_specs=[pl.BlockSpec(memory_space=pltpu.SMEM),
                  pl.BlockSpec((nr * dh, tq), lambda b, gg, i: (qt_row // nr + gg, b * nq + i)),
                  pl.BlockSpec((1, 1, nc, dh), lambda b, gg, i: (b, gg, 0, 0)),
                  pl.BlockSpec((1, 1, dh, nc), lambda b, gg, i: (b, g + gg, 0, 0)),
                  key_blk(ks_col), val_blk(vst_row), key_blk(kw_col), val_blk(vwt_row),
                  pl.BlockSpec((1, gates_t.shape[1], tq), lambda b, gg, i: (gg, 0, b * nq + i))],
        out_specs=pl.BlockSpec((tq, nr * dh), lambda b, gg, i: (b * nq + i, gg)),
        scratch_shapes=[pltpu.VMEM((ns, tq), F32),
                        pltpu.VMEM((2 * tk, nr * tq), F32),
                        pltpu.VMEM((ns, nc), BF16),
                        pltpu.VMEM((nc, nr * tq), F32),
                        pltpu.SMEM((ns // (2 * tk // NSA_SEL_LEN),), jnp.int32)],
        compiler_params=_params("parallel", "parallel", "arbitrary"),
        name="nsa_attention",
    )(slopes, zt, cmp_n, cmp_t, zn, zt, zn, zt, gates_t)


def _conv_kernel(gb_ref, gc_ref, h_ref, gcp_ref, hp_ref, w_ref, o_ref, *, tiles_per_seq):
    i = pl.program_id(0)
    u = gc_ref[...] * h_ref[...]
    prev = gcp_ref[...] * hp_ref[...]
    prev = jnp.where(i % tiles_per_seq == 0, 0.0, prev)
    r = lax.broadcasted_iota(jnp.int32, u.shape, 0)
    u1 = jnp.where(r == 0, prev[7:8], pltpu.roll(u, 1, 0))
    u2 = jnp.where(r == 0, prev[6:7], jnp.where(r == 1, prev[7:8], pltpu.roll(u, 2, 0)))
    w = w_ref[...]
    y = gb_ref[...] * (w[0:1] * u2 + w[1:2] * u1 + w[2:3] * u)
    o_ref[...] = y.astype(o_ref.dtype)


def _short_conv(z, w, *, seq, gb_col, gc_col, h_col, tt=512, tc=512):
    m = z.shape[0]
    c = w.shape[1]
    tt = min(tt, seq)
    assert seq % tt == 0 and c % tc == 0 and tt % 8 == 0
    sub = tt // 8
    cur = lambda col: pl.BlockSpec((tt, tc), lambda i, j: (i, col + j))
    prv = lambda col: pl.BlockSpec((8, tc), lambda i, j: (jnp.maximum(i * sub - 1, 0), col + j))
    return pl.pallas_call(
        functools.partial(_conv_kernel, tiles_per_seq=seq // tt),
        out_shape=jax.ShapeDtypeStruct((m, c), BF16),
        grid=(m // tt, c // tc),
        in_specs=[cur(gb_col), cur(gc_col), cur(h_col), prv(gc_col), prv(h_col),
                  pl.BlockSpec((3, tc), lambda i, j: (0, j))],
        out_specs=pl.BlockSpec((tt, tc), lambda i, j: (i, j)),
        compiler_params=_params("parallel", "arbitrary"),
        name="short_conv",
    )(z, z, z, z, z, w)


def _gate_decay_kernel(x_ref, w1_ref, w2_ref, b_ref, o_ref):
    za = jnp.dot(x_ref[...], w1_ref[...], preferred_element_type=F32)
    pre = jnp.dot(za.astype(BF16), w2_ref[...], preferred_element_type=F32) + b_ref[...]
    ls = -(jnp.maximum(-pre, 0.0) + jnp.log1p(jnp.exp(-jnp.abs(pre))))
    o_ref[...] = ls / GLA_GATE_TAU


def _gate_decay(x, w1, w2, b, *, tm=1024):
    m, k = x.shape
    r = w1.shape[1]
    n = w2.shape[1]
    tm = min(tm, m)
    return pl.pallas_call(
        _gate_decay_kernel,
        out_shape=jax.ShapeDtypeStruct((m, n), F32),
        grid=(m // tm,),
        in_specs=[pl.BlockSpec((tm, k), lambda i: (i, 0)),
                  pl.BlockSpec((k, r), lambda i: (0, 0)),
                  pl.BlockSpec((r, n), lambda i: (0, 0)),
                  pl.BlockSpec((1, n), lambda i: (0, 0))],
        out_specs=pl.BlockSpec((tm, n), lambda i: (i, 0)),
        compiler_params=_params("parallel"),
        name="gla_gate_decay",
    )(x, w1, w2, b.reshape(1, n))


def _gla_kernel(q_ref, k_ref, v_ref, g_ref, la_ref, ng_ref, o_ref, st_ref, *, tc):
    L = GLA_CHUNK
    dk, dv = GLA_DK, GLA_DV
    hp = q_ref.shape[-1] // dk

    @pl.when(pl.program_id(2) == 0)
    def _():
        st_ref[...] = jnp.zeros_like(st_ref)

    nch = tc // L
    ri = lax.broadcasted_iota(jnp.int32, (tc, tc), 0)
    ci = lax.broadcasted_iota(jnp.int32, (tc, tc), 1)
    causal = (ci <= ri) & (ri // L == ci // L)
    tri = causal.astype(BF16)
    ng = ng_ref[...]

    la = la_ref[...]
    la_hi = la.astype(BF16)
    rest = la - la_hi.astype(F32)
    la_mid = rest.astype(BF16)
    la_lo = (rest - la_mid.astype(F32)).astype(BF16)
    b = (jnp.dot(tri, la_hi, preferred_element_type=F32) + jnp.dot(tri, la_mid, preferred_element_type=F32)
         + jnp.dot(tri, la_lo, preferred_element_type=F32))
    b3 = b.reshape(nch, L, hp * dk)
    b_last = b3[:, L - 1:L, :]
    q_t = ((q_ref[...] * dk ** -0.5) * jnp.exp(b)).astype(BF16)
    k_raw = k_ref[...]
    k_t = (k_raw * jnp.exp(-b)).astype(BF16)
    k_d = (k_raw.reshape(nch, L, hp * dk) * jnp.exp(b_last - b3)).astype(BF16)
    dec = jnp.exp(b_last)
    v = v_ref[...].astype(BF16)

    ks = [slice(h * dk, (h + 1) * dk) for h in range(hp)]
    vs = [slice(h * dv, (h + 1) * dv) for h in range(hp)]
    att = [lax.dot_general(q_t[:, ks[h]], k_t[:, ks[h]], NT_DIMS, preferred_element_type=F32) for h in range(hp)]
    u_t = [[lax.dot_general(v[c * L:(c + 1) * L, vs[h]], k_d[c][:, ks[h]], TN_DIMS, preferred_element_type=F32)
            for c in range(nch)] for h in range(hp)]
    o = [jnp.dot(jnp.where(causal, att[h], 0.0).astype(BF16), v[:, vs[h]], preferred_element_type=F32)
         for h in range(hp)]

    states = []
    for h in range(hp):
        st = st_ref[h]
        per_chunk = []
        for c in range(nch):
            per_chunk.append(st.astype(BF16))
            st = st * dec[c][:, ks[h]] + u_t[h][c]
        st_ref[h] = st
        states.append(per_chunk)
    inter = [[lax.dot_general(q_t[c * L:(c + 1) * L, ks[h]], states[h][c], NT_DIMS, preferred_element_type=F32)
              for c in range(nch)] for h in range(hp)]
    for h in range(hp):
        oh = o[h] + jnp.concatenate(inter[h], axis=0)
        oh = oh * lax.rsqrt(jnp.mean(oh * oh, axis=-1, keepdims=True) + NORM_EPS) * ng
        o_ref[:, vs[h]] = (oh * jax.nn.silu(g_ref[:, vs[h]])).astype(o_ref.dtype)


def _gla(z, la, norm_g, *, batch, seq, q_col, k_col, v_col, g_col, tc=256, hp=2):
    dk, dv = GLA_DK, GLA_DV
    tc = min(tc, seq)
    assert seq % tc == 0 and tc % GLA_CHUNK == 0 and GLA_HEADS % hp == 0
    nt = seq // tc
    qk = lambda col: pl.BlockSpec((tc, hp * dk), lambda b, h, i: (b * nt + i, col + h))
    vg = lambda col: pl.BlockSpec((tc, hp * dv), lambda b, h, i: (b * nt + i, col + h))
    return pl.pallas_call(
        functools.partial(_gla_kernel, tc=tc),
        out_shape=jax.ShapeDtypeStruct((batch * seq, GLA_HEADS * dv), BF16),
        grid=(batch, GLA_HEADS // hp, nt),
        in_specs=[qk(q_col), qk(k_col), vg(v_col), vg(g_col), qk(0),
                  pl.BlockSpec((1, dv), lambda b, h, i: (0, 0))],
        out_specs=pl.BlockSpec((tc, hp * dv), lambda b, h, i: (b * nt + i, h)),
        scratch_shapes=[pltpu.VMEM((hp, dv, dk), F32)],
        compiler_params=_params("parallel", "parallel", "arbitrary"),
        name="gla",
    )(z, z, z, z, la, norm_g.reshape(1, dv))


def _pad_cols(w, n):
    return jnp.pad(w, ((0, 0), (0, n - w.shape[1])))


def _ffn_sublayer(xf, xb, wg, wu, wd_b, layer, ln_g, ln_b):
    gu = _ffn_gu(xb, wg, wu, layer)
    h = _matmul(gu, wd_b, F32, layer=layer, tm=512, tn=512)
    return _res_ln(xf, h, ln_g, ln_b, 0.5)


def _alibi_slopes(n):
    return jnp.exp2(-8.0 * jnp.arange(1, n + 1, dtype=F32) / n)


def _attn_sublayer(xf, xb, w_in, w_out, pos_k, w1_k, w2_k, pos_v, w1_v, w2_v, ln_g, ln_b, *, batch, seq):
    dh = HEAD_DIM
    nq = NSA_HEADS * dh
    kv = NSA_KV_GROUPS * dh
    nm = MOBA_HEADS * dh
    n_gate = NSA_HEADS * 3
    c_kc, c_ks, c_vs, c_kw, c_vw, c_gate = (nq + i * kv for i in (0, 2, 3, 4, 5, 6))
    c_mq = c_gate + n_gate
    c_mk, c_mv = c_mq + nm, c_mq + 2 * nm
    col = lambda a, n: w_in[:, a:a + n]
    w_n = jnp.concatenate([col(c_ks, kv), col(c_kw, kv), col(c_mk, nm)], axis=1).astype(BF16)
    w_t = jnp.concatenate([col(0, nq), col(c_vs, kv), col(c_vw, kv), col(c_mq, nm), col(c_mv, nm)],
                          axis=1).T.astype(BF16)
    w_f = col(c_kc, 2 * kv).astype(BF16)
    w_g = _pad_cols(col(c_gate, n_gate), LANE).T.astype(BF16)
    zn = _matmul(xb, w_n, BF16)
    zt = _matmul_nt(w_t, xb, BF16)
    zf = _matmul(xb, w_f, F32)
    gates_t = _matmul_nt(w_g, xb, F32, act="sigmoid")[:n_gate]
    gates_t = jnp.pad(gates_t.reshape(NSA_KV_GROUPS, 3 * NSA_GROUP_SIZE, batch * seq), ((0, 0), (0, 4), (0, 0)))

    slopes = _alibi_slopes(N_ATTN_HEADS)
    cmp_n, cmp_t = _nsa_compress(zf, jnp.stack([pos_k, pos_v]), jnp.stack([w1_k, w1_v]).astype(BF16),
                                 jnp.stack([w2_k, w2_v]).astype(BF16), batch=batch, seq=seq)
    u = nq // dh
    o_nsa = _nsa(zt, zn, cmp_n, cmp_t, gates_t, slopes[0::2], batch=batch, seq=seq,
                 qt_row=0, vst_row=u, vwt_row=u + 4, ks_col=0, kw_col=4)
    o_moba = _moba(zt, zn, slopes[1::2], batch=batch, seq=seq, qt_row=u + 8, vt_row=u + 24, k_col=8)
    w_o = w_out.astype(BF16)
    y = _matmul2(o_nsa, o_moba, w_o[:nq], w_o[nq:], F32)
    return _res_ln(xf, y, ln_g, ln_b, 1.0)


def _mix_sublayer(xf, xb, w_in, w_out, conv_w, w_a2, b_a, norm_g, ln_g, ln_b, *, batch, seq):
    cc = CONV_CHANNELS
    hk = GLA_HEADS * GLA_DK
    hv = GLA_HEADS * GLA_DV
    c_za = 3 * cc + 2 * hk + 2 * hv
    z = _matmul(xb, w_in[:, :c_za].astype(BF16), F32)
    w_za = _pad_cols(w_in[:, c_za:], LANE).astype(BF16)
    w_a2p = jnp.pad(w_a2, ((0, LANE - w_a2.shape[0]), (0, 0))).astype(BF16)
    la = _gate_decay(xb, w_za, w_a2p, b_a)
    y_conv = _short_conv(z, conv_w, seq=seq, gb_col=0, gc_col=cc // 512, h_col=2 * cc // 512)
    hp = 2
    y_gla = _gla(z, la, norm_g, batch=batch, seq=seq, q_col=3 * cc // (hp * GLA_DK),
                 k_col=(3 * cc + hk) // (hp * GLA_DK), v_col=(3 * cc + 2 * hk) // (hp * GLA_DV),
                 g_col=(3 * cc + 2 * hk + hv) // (hp * GLA_DV), hp=hp)
    w_o = w_out.astype(BF16)
    y = _matmul2(y_conv, y_gla, w_o[:cc], w_o[cc:], F32)
    return _res_ln(xf, y, ln_g, ln_b, 1.0)


def kernel(x, ln_g, ln_b, ffn_pre_wg, ffn_pre_wu, ffn_pre_wd, ffn_post_wg, ffn_post_wu, ffn_post_wd,
           att_w_in, att_w_out, nsa_pos_k, nsa_w1_k, nsa_w2_k, nsa_pos_v, nsa_w1_v, nsa_w2_v,
           mix_w_in, mix_w_out, conv_w, gla_w_a2, gla_b_a, gla_norm_g):
    batch, seq, d = x.shape
    xf = x.reshape(batch * seq, d)
    xb = xf.astype(BF16)
    pre_wd, post_wd = ffn_pre_wd.astype(BF16), ffn_post_wd.astype(BF16)
    for layer in range(DEPTH):
        xf, xb = _ffn_sublayer(xf, xb, ffn_pre_wg, ffn_pre_wu, pre_wd, layer, ln_g[layer, 0], ln_b[layer, 0])
        i = layer // 2
        if layer % 2 == 0:
            xf, xb = _attn_sublayer(xf, xb, att_w_in[i], att_w_out[i], nsa_pos_k[i], nsa_w1_k[i], nsa_w2_k[i],
                                    nsa_pos_v[i], nsa_w1_v[i], nsa_w2_v[i], ln_g[layer, 1], ln_b[layer, 1],
                                    batch=batch, seq=seq)
        else:
            xf, xb = _mix_sublayer(xf, xb, mix_w_in[i], mix_w_out[i], conv_w[i], gla_w_a2[i], gla_b_a[i],
                                   gla_norm_g[i], ln_g[layer, 1], ln_b[layer, 1], batch=batch, seq=seq)
        xf, xb = _ffn_sublayer(xf, xb, ffn_post_wg, ffn_post_wu, post_wd, layer, ln_g[layer, 2], ln_b[layer, 2])
    return xf.reshape(batch, seq, d)
```

```python
import functools
import math

import jax
import jax.numpy as jnp
from jax import lax
from jax.experimental import pallas as pl
from jax.experimental.pallas import tpu as pltpu

F32 = jnp.float32
BF16 = jnp.bfloat16

D_MODEL = 4096
DEPTH = 2
HEAD_DIM = 128
NSA_HEADS = 16
NSA_KV_GROUPS = 4
NSA_GROUP_SIZE = 4
NSA_CMP_STRIDE = 16
NSA_CMP_LEN = 32
NSA_SEL_LEN = 64
NSA_N_SEL = 16
NSA_WINDOW = 512
MOBA_HEADS = 16
MOBA_BLOCK = 256
MOBA_TOPK = 3
N_ATTN_HEADS = NSA_HEADS + MOBA_HEADS
CONV_CHANNELS = 2048
GLA_HEADS = 16
GLA_DK = 64
GLA_DV = 128
GLA_GATE_RANK = 16
GLA_GATE_TAU = 16.0
GLA_CHUNK = 64
D_FF = 11008
ALPHA = (2 * DEPTH) ** 0.25
LN_EPS = 1e-5
NORM_EPS = 1e-6
NEG_INF = -1e30
SEL_FORCE = 1e4
LOG2E = math.log2(math.e)

LANE = 128
ONES_ROWS = 16
VMEM_LIMIT = 60 * 1024 * 1024

NT_DIMS = (((1,), (1,)), ((), ()))
TN_DIMS = (((0,), (0,)), ((), ()))


def _params(*sem):
    return pltpu.CompilerParams(dimension_semantics=sem, vmem_limit_bytes=VMEM_LIMIT)


def _top_k_rows(vals, k):
    n = vals.shape[0]
    rid = lax.broadcasted_iota(jnp.int32, vals.shape, 0)
    sel = jnp.zeros(vals.shape, jnp.bool_)
    g = vals
    for _ in range(k):
        m = jnp.max(g, axis=0, keepdims=True)
        idx = jnp.min(jnp.where(g == m, rid, n), axis=0, keepdims=True)
        pick = rid == idx
        sel = sel | pick
        g = jnp.where(pick, -jnp.inf, g)
    return sel


def _split3(x):
    hi = x.astype(BF16)
    rest = x - hi.astype(F32)
    mid = rest.astype(BF16)
    return hi, mid, (rest - mid.astype(F32)).astype(BF16)


def _pv_with_sums(vt, p):
    ones = jnp.ones((ONES_ROWS, vt.shape[1]), BF16)
    return jnp.dot(jnp.concatenate([vt, ones], axis=0), p.astype(BF16), preferred_element_type=F32)


def _mm_kernel(a_ref, b_ref, o_ref):
    o_ref[...] = jnp.dot(a_ref[...], b_ref[0], preferred_element_type=F32).astype(o_ref.dtype)


def _matmul(a, b, out_dtype, *, layer=0, tm=1024, tn=1024):
    if b.ndim == 2:
        b = b[None]
    m, k = a.shape
    n = b.shape[2]
    tm, tn = min(tm, m), min(tn, n)
    assert m % tm == 0 and n % tn == 0
    return pl.pallas_call(
        _mm_kernel,
        out_shape=jax.ShapeDtypeStruct((m, n), out_dtype),
        grid=(m // tm, n // tn),
        in_specs=[pl.BlockSpec((tm, k), lambda i, j: (i, 0)),
                  pl.BlockSpec((1, k, tn), lambda i, j: (layer, 0, j))],
        out_specs=pl.BlockSpec((tm, tn), lambda i, j: (i, j)),
        compiler_params=_params("parallel", "arbitrary"),
        name="matmul",
    )(a, b)


def _mm_nt_kernel(w_ref, x_ref, o_ref, *, act):
    r = lax.dot_general(w_ref[...], x_ref[...], NT_DIMS, preferred_element_type=F32)
    if act == "sigmoid":
        r = jax.nn.sigmoid(r)
    o_ref[...] = r.astype(o_ref.dtype)


def _matmul_nt(wt, x, out_dtype, *, tn=1024, tm=1024, act=None):
    n, k = wt.shape
    m = x.shape[0]
    tn, tm = min(tn, n), min(tm, m)
    assert m % tm == 0 and n % tn == 0
    return pl.pallas_call(
        functools.partial(_mm_nt_kernel, act=act),
        out_shape=jax.ShapeDtypeStruct((n, m), out_dtype),
        grid=(m // tm, n // tn),
        in_specs=[pl.BlockSpec((tn, k), lambda i, j: (j, 0)),
                  pl.BlockSpec((tm, k), lambda i, j: (i, 0))],
        out_specs=pl.BlockSpec((tn, tm), lambda i, j: (j, i)),
        compiler_params=_params("parallel", "arbitrary"),
        name="matmul_nt",
    )(wt, x)


def _mm2_kernel(a1_ref, a2_ref, b1_ref, b2_ref, o_ref):
    r = jnp.dot(a1_ref[...], b1_ref[...], preferred_element_type=F32)
    r = r + jnp.dot(a2_ref[...], b2_ref[...], preferred_element_type=F32)
    o_ref[...] = r.astype(o_ref.dtype)


def _matmul2(a1, a2, b1, b2, out_dtype, *, tm=1024, tn=1024):
    m, k1 = a1.shape
    k2 = a2.shape[1]
    n = b1.shape[1]
    tm, tn = min(tm, m), min(tn, n)
    assert m % tm == 0 and n % tn == 0
    return pl.pallas_call(
        _mm2_kernel,
        out_shape=jax.ShapeDtypeStruct((m, n), out_dtype),
        grid=(m // tm, n // tn),
        in_specs=[pl.BlockSpec((tm, k1), lambda i, j: (i, 0)),
                  pl.BlockSpec((tm, k2), lambda i, j: (i, 0)),
                  pl.BlockSpec((k1, tn), lambda i, j: (0, j)),
                  pl.BlockSpec((k2, tn), lambda i, j: (0, j))],
        out_specs=pl.BlockSpec((tm, tn), lambda i, j: (i, j)),
        compiler_params=_params("parallel", "arbitrary"),
        name="matmul2",
    )(a1, a2, b1, b2)


def _ffn_gu_kernel(x_ref, wg_ref, wu_ref, o_ref, wb_ref):
    tn = o_ref.shape[1]

    @pl.when(pl.program_id(1) == 0)
    def _():
        wb_ref[:, 0:tn] = wg_ref[0].astype(BF16)
        wb_ref[:, tn:2 * tn] = wu_ref[0].astype(BF16)

    r = jnp.dot(x_ref[...], wb_ref[...], preferred_element_type=F32)
    o_ref[...] = (jax.nn.silu(r[:, 0:tn]) * r[:, tn:2 * tn]).astype(o_ref.dtype)


def _ffn_gu(x, wg, wu, layer, *, tm=2048, tn=256):
    m, k = x.shape
    n = wg.shape[2]
    tm, tn = min(tm, m), min(tn, n)
    assert m % tm == 0 and n % tn == 0
    w_spec = pl.BlockSpec((1, k, tn), lambda j, i: (layer, 0, j))
    return pl.pallas_call(
        _ffn_gu_kernel,
        out_shape=jax.ShapeDtypeStruct((m, n), BF16),
        grid=(n // tn, m // tm),
        in_specs=[pl.BlockSpec((tm, k), lambda j, i: (i, 0)), w_spec, w_spec],
        out_specs=pl.BlockSpec((tm, tn), lambda j, i: (i, j)),
        scratch_shapes=[pltpu.VMEM((k, 2 * tn), BF16)],
        compiler_params=_params("parallel", "arbitrary"),
        name="ffn_gate_up",
    )(x, wg, wu)


def _res_ln_kernel(x_ref, h_ref, g_ref, b_ref, of_ref, ob_ref, *, coef):
    v = ALPHA * x_ref[...] + coef * h_ref[...]
    mu = jnp.mean(v, axis=-1, keepdims=True)
    d = v - mu
    var = jnp.mean(d * d, axis=-1, keepdims=True)
    y = d * lax.rsqrt(var + LN_EPS) * g_ref[...] + b_ref[...]
    of_ref[...] = y
    ob_ref[...] = y.astype(BF16)


def _res_ln(x, h, g, b, coef, *, tm=256):
    m, d = x.shape
    tm = min(tm, m)
    assert m % tm == 0
    row = pl.BlockSpec((tm, d), lambda i: (i, 0))
    vec = pl.BlockSpec((1, d), lambda i: (0, 0))
    return pl.pallas_call(
        functools.partial(_res_ln_kernel, coef=coef),
        out_shape=(jax.ShapeDtypeStruct((m, d), F32), jax.ShapeDtypeStruct((m, d), BF16)),
        grid=(m // tm,),
        in_specs=[row, row, vec, vec],
        out_specs=(row, row),
        compiler_params=_params("parallel"),
        name="residual_layernorm",
    )(x, h, g.reshape(1, d), b.reshape(1, d))


def _moba_kernel(slope_ref, qt_ref, k_ref, vt_ref, o_ref, kmean_ref, bias_ref, sel_ref,
                 *, nblk, blk, n_top, c2, hp):
    hg = pl.program_id(1)
    qi = pl.program_id(2)
    dh = HEAD_DIM
    kc = lax.broadcasted_iota(jnp.int32, (blk, blk), 0)
    qr = lax.broadcasted_iota(jnp.int32, (blk, blk), 1)
    slope2 = [slope_ref[hg * hp + h] * LOG2E for h in range(hp)]
    cols = [slice(h * dh, (h + 1) * dh) for h in range(hp)]

    @pl.when(qi == 0)
    def _():
        kc2 = lax.broadcasted_iota(jnp.int32, (2 * blk, blk), 0)
        qr2 = lax.broadcasted_iota(jnp.int32, (2 * blk, blk), 1)
        for h in range(hp):
            kf = k_ref[:, cols[h]].astype(F32).reshape(nblk, blk, dh)
            kmean_ref[h] = (jnp.sum(kf, axis=1) / blk).astype(BF16)
            bias_ref[h] = slope2[h] * (qr2 - kc2).astype(F32)

    qt = [qt_ref[cols[h], :] for h in range(hp)]
    gate = [jnp.dot(kmean_ref[h], qt[h], preferred_element_type=F32) for h in range(hp)]
    past = lax.broadcasted_iota(jnp.int32, (nblk, blk), 0) < qi
    for h in range(hp):
        sel = _top_k_rows(jnp.where(past, gate[h], NEG_INF), n_top) & past
        sel_ref[h] = sel.astype(F32)

    def scores(h, k0, nk):
        return (jnp.dot(k_ref[pl.ds(k0, nk), cols[h]], qt[h], preferred_element_type=F32) * c2
                - bias_ref[h, 0:nk, :])

    def pv(h, k0, nk, p):
        return _pv_with_sums(vt_ref[cols[h], pl.ds(k0, nk)], p)

    kq = pl.multiple_of(qi * blk, blk)
    s0 = [jnp.where(kc <= qr, scores(h, kq, blk), NEG_INF) for h in range(hp)]
    m0 = [jnp.max(s0[h], axis=0, keepdims=True) for h in range(hp)]
    acc0 = [pv(h, kq, blk, jnp.exp2(s0[h] - m0[h])) for h in range(hp)]
    init = []
    for h in range(hp):
        init += [m0[h], acc0[h]]

    def body(t, carry):
        j = qi - 1 - 2 * t
        pb = jnp.maximum(j - 1, 0)
        k0 = pl.multiple_of(pb * blk, blk)
        s = [scores(h, k0, 2 * blk) for h in range(hp)]
        valid = pb + lax.broadcasted_iota(jnp.int32, (2, 1), 0) <= j
        m_new, scale, p = [], [], []
        for h in range(hp):
            m_i = carry[2 * h]
            off = slope2[h] * ((qi - pb) * blk).astype(F32)
            rows = jnp.concatenate([sel_ref[h, pl.ds(pb, 1), :], sel_ref[h, pl.ds(pb + 1, 1), :]], axis=0)
            picked = (rows > 0.5) & valid
            blk_max = jnp.max(s[h].reshape(2, blk, blk), axis=1)
            m_pair = jnp.max(jnp.where(picked, blk_max, NEG_INF), axis=0, keepdims=True)
            m_h = jnp.maximum(m_i, m_pair - off)
            shift = jnp.where(picked, m_h + off, -NEG_INF)
            shift = jnp.broadcast_to(shift[:, None, :], (2, blk, blk)).reshape(2 * blk, blk)
            m_new.append(m_h)
            scale.append(jnp.exp2(m_i - m_h))
            p.append(jnp.exp2(s[h] - shift))
        upd = [pv(h, k0, 2 * blk, p[h]) for h in range(hp)]
        out = []
        for h in range(hp):
            out += [m_new[h], scale[h] * carry[2 * h + 1] + upd[h]]
        return tuple(out)

    fin = lax.fori_loop(0, (qi + 1) // 2, body, tuple(init))
    for h in range(hp):
        acc = fin[2 * h + 1]
        o_t = acc[0:dh] / jnp.maximum(acc[dh:dh + 1], 1e-30)
        o_ref[:, cols[h]] = o_t.T.astype(o_ref.dtype)


def _moba(zt, zn, slopes, *, batch, seq, qt_row, vt_row, k_col, hp=4):
    blk = MOBA_BLOCK
    assert seq % blk == 0 and MOBA_HEADS % hp == 0
    assert qt_row % hp == 0 and vt_row % hp == 0 and k_col % hp == 0
    nblk = seq // blk
    n_top = min(MOBA_TOPK, nblk - 1)
    assert n_top > 0
    dh = HEAD_DIM
    kern = functools.partial(_moba_kernel, nblk=nblk, blk=blk, n_top=n_top, c2=dh ** -0.5 * LOG2E, hp=hp)
    return pl.pallas_call(
        kern,
        out_shape=jax.ShapeDtypeStruct((batch * seq, MOBA_HEADS * dh), BF16),
        grid=(batch, MOBA_HEADS // hp, nblk),
        in_specs=[pl.BlockSpec(memory_space=pltpu.SMEM),
                  pl.BlockSpec((hp * dh, blk), lambda b, h, i: (qt_row // hp + h, b * nblk + i)),
                  pl.BlockSpec((seq, hp * dh), lambda b, h, i: (b, k_col // hp + h)),
                  pl.BlockSpec((hp * dh, seq), lambda b, h, i: (vt_row // hp + h, b))],
        out_specs=pl.BlockSpec((blk, hp * dh), lambda b, h, i: (b * nblk + i, h)),
        scratch_shapes=[pltpu.VMEM((hp, nblk, dh), BF16),
                        pltpu.VMEM((hp, 2 * blk, blk), F32),
                        pltpu.VMEM((hp, nblk, blk), F32)],
        compiler_params=_params("parallel", "parallel", "arbitrary"),
        name="moba_attention",
    )(slopes, zt, zn, zt)


def _compress_kernel(x_ref, pos_ref, w1_ref, w2_ref, o_ref, ot_ref, *, nc):
    dh = x_ref.shape[-1]
    half = NSA_CMP_STRIDE
    acc_lo = jnp.zeros((nc, dh), F32)
    acc_hi = jnp.zeros((nc, dh), F32)
    for p in range(half):
        xp = x_ref[pl.ds(p, nc, stride=half), :]
        lo = (xp + pos_ref[0, p:p + 1, :]).astype(BF16)
        hi = (xp + pos_ref[0, half + p:half + p + 1, :]).astype(BF16)
        acc_lo += jnp.dot(lo, w1_ref[0, p * dh:(p + 1) * dh, :], preferred_element_type=F32)
        acc_hi += jnp.dot(hi, w1_ref[0, (half + p) * dh:(half + p + 1) * dh, :], preferred_element_type=F32)
    pre = acc_lo + pltpu.roll(acc_hi, nc - 1, 0)
    hid = jax.nn.gelu(pre)
    out = jnp.dot(hid.astype(BF16), w2_ref[0], preferred_element_type=F32)
    o_ref[0, 0] = out.astype(o_ref.dtype)
    ot_ref[0, 0] = out.T.astype(ot_ref.dtype)


def _nsa_compress(zf, pos, w1, w2, *, batch, seq):
    dh = HEAD_DIM
    g = NSA_KV_GROUPS
    nc = seq // NSA_CMP_STRIDE
    return pl.pallas_call(
        functools.partial(_compress_kernel, nc=nc),
        out_shape=(jax.ShapeDtypeStruct((batch, 2 * g, nc, dh), BF16),
                   jax.ShapeDtypeStruct((batch, 2 * g, dh, nc), BF16)),
        grid=(batch, 2 * g),
        in_specs=[pl.BlockSpec((seq, dh), lambda b, c: (b, c)),
                  pl.BlockSpec((1, NSA_CMP_LEN, dh), lambda b, c: (c // g, 0, 0)),
                  pl.BlockSpec((1, NSA_CMP_LEN * dh, dh), lambda b, c: (c // g, 0, 0)),
                  pl.BlockSpec((1, dh, dh), lambda b, c: (c // g, 0, 0))],
        out_specs=(pl.BlockSpec((1, 1, nc, dh), lambda b, c: (b, c, 0, 0)),
                   pl.BlockSpec((1, 1, dh, nc), lambda b, c: (b, c, 0, 0))),
        compiler_params=_params("parallel", "arbitrary"),
        name="nsa_compress",
    )(zf, pos, w1, w2)


def _nsa_kernel(slope_ref, qt_ref, kc_ref, vct_ref, ks_ref, vst_ref, kw_ref, vwt_ref, gate_ref, o_ref,
                sel_ref, bias_ref, member_ref, ecmp_ref, flag_ref, *, tq, tk, nc, ns, k_top, c2):
    g = pl.program_id(1)
    qi = pl.program_id(2)
    dh = HEAD_DIM
    nr = NSA_GROUP_SIZE
    nl = nr * tq
    bpt = tk // NSA_SEL_LEN
    q0 = qi * tq

    lane = lax.broadcasted_iota(jnp.int32, (1, nl), 1)
    tl = q0 + lane % tq
    head = lane // tq
    slope2 = jnp.zeros((1, nl), F32)
    for r in range(nr):
        slope2 = jnp.where(head == r, slope_ref[g * nr + r] * LOG2E, slope2)

    @pl.when(qi == 0)
    def _():
        kc_i = lax.broadcasted_iota(jnp.int32, (2 * tk, nl), 0)
        bias_ref[...] = slope2 * (lane % tq - kc_i).astype(F32)
        sj = lax.broadcasted_iota(jnp.int32, (ns, nc), 0) * NSA_SEL_LEN
        ci = lax.broadcasted_iota(jnp.int32, (ns, nc), 1) * NSA_CMP_STRIDE
        member_ref[...] = ((ci < sj + NSA_SEL_LEN) & (ci + NSA_CMP_LEN > sj)).astype(BF16)
        ecmp_ref[...] = slope2 * (lax.broadcasted_iota(jnp.int32, (nc, nl), 0) * NSA_CMP_STRIDE
                                  + (NSA_CMP_LEN - 1)).astype(F32)

    qb = qt_ref[...]
    qt = jnp.concatenate([qb[r * dh:(r + 1) * dh, :] for r in range(nr)], axis=1)

    span = NSA_WINDOW + tq
    w0 = pl.multiple_of(jnp.maximum(q0 - NSA_WINDOW, 0), tq)
    raw_cmp = jnp.dot(kc_ref[0, 0], qt, preferred_element_type=F32)
    raw_win = jnp.dot(kw_ref[pl.ds(w0, span), :], qt, preferred_element_type=F32)

    cend = lax.broadcasted_iota(jnp.int32, (nc, 1), 0) * NSA_CMP_STRIDE + (NSA_CMP_LEN - 1)
    s = (raw_cmp * c2 + ecmp_ref[...]) - slope2 * tl.astype(F32)
    vis = cend <= tl
    s = jnp.where(vis, s, NEG_INF)
    e = jnp.where(vis, jnp.exp2(s - jnp.max(s, axis=0, keepdims=True)), 0.0)
    p_cmp = e * (1.0 / jnp.maximum(jnp.sum(e, axis=0, keepdims=True), 1e-30))
    o_cmp = jnp.dot(vct_ref[0, 0], p_cmp.astype(BF16), preferred_element_type=F32)

    p_sum = p_cmp[:, 0:tq]
    for r in range(1, nr):
        p_sum = p_sum + p_cmp[:, r * tq:(r + 1) * tq]
    member = member_ref[...]
    imp = sum(jnp.dot(member, part, preferred_element_type=F32) for part in _split3(p_sum))

    rel = (w0 + lax.broadcasted_iota(jnp.int32, (span, 1), 0)) - tl
    s = (raw_win * c2 - bias_ref[0:span, :]) + slope2 * (w0 - q0).astype(F32)
    s = jnp.where((rel <= 0) & (rel > -NSA_WINDOW), s, NEG_INF)
    acc = _pv_with_sums(vwt_ref[:, pl.ds(w0, span)], jnp.exp2(s - jnp.max(s, axis=0, keepdims=True)))
    o_win = acc[0:dh] / jnp.maximum(acc[dh:dh + 1], 1e-30)

    own = (q0 + lax.broadcasted_iota(jnp.int32, (1, tq), 1)) // NSA_SEL_LEN
    sid = lax.broadcasted_iota(jnp.int32, (ns, tq), 0)
    forced = (sid == 0) | (sid == own) | (sid == own - 1)
    imp = jnp.where(forced, imp + SEL_FORCE, jnp.where(sid > own, -1.0, imp))
    sel_ref[...] = _top_k_rows(imp, k_top).astype(F32)

    def sel_rows(b0, nb):
        rows = sel_ref[pl.ds(pl.multiple_of(b0, bpt), nb), :]
        return jnp.concatenate([rows] * nr, axis=1)

    def scores(k0, nk):
        return jnp.dot(ks_ref[pl.ds(k0, nk), :], qt, preferred_element_type=F32) * c2 - bias_ref[0:nk, :]

    def blockwise(x):
        nb = x.shape[0]
        return jnp.broadcast_to(x[:, None, :], (nb, NSA_SEL_LEN, nl)).reshape(nb * NSA_SEL_LEN, nl)

    jd = q0 // tk
    kd = pl.multiple_of(jd * tk, tk)
    kpos = kd + lax.broadcasted_iota(jnp.int32, (tk, 1), 0)
    off = slope2 * (q0 - kd).astype(F32)
    ok = (blockwise(sel_rows(jd * bpt, bpt)) > 0.5) & (kpos <= tl)
    s = jnp.where(ok, scores(kd, tk), NEG_INF)
    m0 = jnp.max(s, axis=0, keepdims=True) - off
    acc0 = _pv_with_sums(vst_ref[:, pl.ds(kd, tk)], jnp.exp2(s - (m0 + off)))

    def sweep(i, live, carry):
        m_i, acc = carry
        k0 = pl.multiple_of(i * (2 * tk), 2 * tk)
        s = scores(k0, 2 * tk)
        off = slope2 * (q0 - k0).astype(F32)
        rows = jnp.concatenate([live] * nr, axis=1) > 0.5
        blk_max = jnp.max(s.reshape(2 * bpt, NSA_SEL_LEN, nl), axis=1)
        m_tile = jnp.max(jnp.where(rows, blk_max, NEG_INF), axis=0, keepdims=True)
        m_new = jnp.maximum(m_i, m_tile - off)
        shift = jnp.where(rows, m_new + off, -NEG_INF)
        upd = _pv_with_sums(vst_ref[:, pl.ds(k0, 2 * tk)], jnp.exp2(s - blockwise(shift)))
        return m_new, jnp.exp2(m_i - m_new) * acc + upd

    vis_sel = jnp.where(sid < jd * bpt, sel_ref[...], 0.0)
    pair_any = jnp.max(jnp.max(vis_sel.reshape(ns // (2 * bpt), 2 * bpt, tq), axis=1), axis=1, keepdims=True)
    for i in range(ns // (2 * bpt)):
        flag_ref[i] = (pair_any[i, 0] > 0.5).astype(jnp.int32)

    def body(i, carry):
        b0 = pl.multiple_of(i * (2 * bpt), 2 * bpt)
        bid = b0 + lax.broadcasted_iota(jnp.int32, (2 * bpt, 1), 0)
        live = jnp.where(bid < jd * bpt, sel_ref[pl.ds(b0, 2 * bpt), :], 0.0)
        return lax.cond(flag_ref[i] > 0, lambda c: sweep(i, live, c), lambda c: c, carry)

    _, acc = lax.fori_loop(0, (jd + 1) // 2, body, (m0, acc0))
    o_slc = acc[0:dh] / jnp.maximum(acc[dh:dh + 1], 1e-30)

    gt = gate_ref[0]
    for r in range(nr):
        sl = slice(r * tq, (r + 1) * tq)
        o = (gt[3 * r:3 * r + 1] * o_cmp[:, sl] + gt[3 * r + 1:3 * r + 2] * o_slc[:, sl]
             + gt[3 * r + 2:3 * r + 3] * o_win[:, sl])
        o_ref[:, r * dh:(r + 1) * dh] = o.T.astype(o_ref.dtype)


def _nsa(zt, zn, cmp_n, cmp_t, gates_t, slopes, *, batch, seq, qt_row, vst_row, vwt_row, ks_col, kw_col,
         tq=128, tk=512):
    dh = HEAD_DIM
    g = NSA_KV_GROUPS
    nr = NSA_GROUP_SIZE
    tk = min(tk, seq)
    assert seq % (2 * tk) == 0 and tk % tq == 0 and NSA_WINDOW % tq == 0
    assert NSA_WINDOW + tq <= min(seq, 2 * tk)
    nq = seq // tq
    nc = seq // NSA_CMP_STRIDE
    ns = seq // NSA_SEL_LEN
    k_top = min(NSA_N_SEL, ns)
    kern = functools.partial(_nsa_kernel, tq=tq, tk=tk, nc=nc, ns=ns, k_top=k_top, c2=dh ** -0.5 * LOG2E)
    key_blk = lambda col: pl.BlockSpec((seq, dh), lambda b, gg, i: (b, col + gg))
    val_blk = lambda row: pl.BlockSpec((dh, seq), lambda b, gg, i: (row + gg, b))
    return pl.pallas_call(
        kern,
        out_shape=jax.ShapeDtypeStruct((batch * seq, NSA_HEADS * dh), BF16),
        grid=(batch, g, nq),
        in_specs=[pl.BlockSpec(memory_space=pltpu.SMEM),
                  pl.BlockSpec((nr * dh, tq), lambda b, gg, i: (qt_row // nr + gg, b * nq + i)),
                  pl.BlockSpec((1, 1, nc, dh), lambda b, gg, i: (b, gg, 0, 0)),
                  pl.BlockSpec((1, 1, dh, nc), lambda b, gg, i: (b, g + gg, 0, 0)),
                  key_blk(ks_col), val_blk(vst_row), key_blk(kw_col), val_blk(vwt_row),
                  pl.BlockSpec((1, gates_t.shape[1], tq), lambda b, gg, i: (gg, 0, b * nq + i))],
        out_specs=pl.BlockSpec((tq, nr * dh), lambda b, gg, i: (b * nq + i, gg)),
        scratch_shapes=[pltpu.VMEM((ns, tq), F32),
                        pltpu.VMEM((2 * tk, nr * tq), F32),
                        pltpu.VMEM((ns, nc), BF16),
                        pltpu.VMEM((nc, nr * tq), F32),
                        pltpu.SMEM((ns // (2 * tk // NSA_SEL_LEN),), jnp.int32)],
        compiler_params=_params("parallel", "parallel", "arbitrary"),
        name="nsa_attention",
    )(slopes, zt, cmp_n, cmp_t, zn, zt, zn, zt, gates_t)


def _conv_kernel(gb_ref, gc_ref, h_ref, gcp_ref, hp_ref, w_ref, o_ref, *, tiles_per_seq):
    i = pl.program_id(0)
    u = gc_ref[...] * h_ref[...]
    prev = gcp_ref[...] * hp_ref[...]
    prev = jnp.where(i % tiles_per_seq == 0, 0.0, prev)
    r = lax.broadcasted_iota(jnp.int32, u.shape, 0)
    u1 = jnp.where(r == 0, prev[7:8], pltpu.roll(u, 1, 0))
    u2 = jnp.where(r == 0, prev[6:7], jnp.where(r == 1, prev[7:8], pltpu.roll(u, 2, 0)))
    w = w_ref[...]
    y = gb_ref[...] * (w[0:1] * u2 + w[1:2] * u1 + w[2:3] * u)
    o_ref[...] = y.astype(o_ref.dtype)


def _short_conv(z, w, *, seq, gb_col, gc_col, h_col, tt=512, tc=512):
    m = z.shape[0]
    c = w.shape[1]
    tt = min(tt, seq)
    assert seq % tt == 0 and c % tc == 0 and tt % 8 == 0
    sub = tt // 8
    cur = lambda col: pl.BlockSpec((tt, tc), lambda i, j: (i, col + j))
    prv = lambda col: pl.BlockSpec((8, tc), lambda i, j: (jnp.maximum(i * sub - 1, 0), col + j))
    return pl.pallas_call(
        functools.partial(_conv_kernel, tiles_per_seq=seq // tt),
        out_shape=jax.ShapeDtypeStruct((m, c), BF16),
        grid=(m // tt, c // tc),
        in_specs=[cur(gb_col), cur(gc_col), cur(h_col), prv(gc_col), prv(h_col),
                  pl.BlockSpec((3, tc), lambda i, j: (0, j))],
        out_specs=pl.BlockSpec((tt, tc), lambda i, j: (i, j)),
        compiler_params=_params("parallel", "arbitrary"),
        name="short_conv",
    )(z, z, z, z, z, w)


def _gate_decay_kernel(x_ref, w1_ref, w2_ref, b_ref, o_ref):
    za = jnp.dot(x_ref[...], w1_ref[...], preferred_element_type=F32)
    pre = jnp.dot(za.astype(BF16), w2_ref[...], preferred_element_type=F32) + b_ref[...]
    ls = -(jnp.maximum(-pre, 0.0) + jnp.log1p(jnp.exp(-jnp.abs(pre))))
    o_ref[...] = ls / GLA_GATE_TAU


def _gate_decay(x, w1, w2, b, *, tm=1024):
    m, k = x.shape
    r = w1.shape[1]
    n = w2.shape[1]
    tm = min(tm, m)
    return pl.pallas_call(
        _gate_decay_kernel,
        out_shape=jax.ShapeDtypeStruct((m, n), F32),
        grid=(m // tm,),
        in_specs=[pl.BlockSpec((tm, k), lambda i: (i, 0)),
                  pl.BlockSpec((k, r), lambda i: (0, 0)),
                  pl.BlockSpec((r, n), lambda i: (0, 0)),
                  pl.BlockSpec((1, n), lambda i: (0, 0))],
        out_specs=pl.BlockSpec((tm, n), lambda i: (i, 0)),
        compiler_params=_params("parallel"),
        name="gla_gate_decay",
    )(x, w1, w2, b.reshape(1, n))


def _gla_kernel(q_ref, k_ref, v_ref, g_ref, la_ref, ng_ref, o_ref, st_ref, *, tc):
    L = GLA_CHUNK
    dk, dv = GLA_DK, GLA_DV
    hp = q_ref.shape[-1] // dk

    @pl.when(pl.program_id(2) == 0)
    def _():
        st_ref[...] = jnp.zeros_like(st_ref)

    nch = tc // L
    ri = lax.broadcasted_iota(jnp.int32, (tc, tc), 0)
    ci = lax.broadcasted_iota(jnp.int32, (tc, tc), 1)
    causal = (ci <= ri) & (ri // L == ci // L)
    tri = causal.astype(BF16)
    ng = ng_ref[...]

    b = sum(jnp.dot(tri, part, preferred_element_type=F32) for part in _split3(la_ref[...]))
    b3 = b.reshape(nch, L, hp * dk)
    b_last = b3[:, L - 1:L, :]
    q_t = ((q_ref[...] * dk ** -0.5) * jnp.exp(b)).astype(BF16)
    k_raw = k_ref[...]
    k_t = (k_raw * jnp.exp(-b)).astype(BF16)
    k_d = (k_raw.reshape(nch, L, hp * dk) * jnp.exp(b_last - b3)).astype(BF16)
    dec = jnp.exp(b_last)
    v = v_ref[...].astype(BF16)

    ks = [slice(h * dk, (h + 1) * dk) for h in range(hp)]
    vs = [slice(h * dv, (h + 1) * dv) for h in range(hp)]
    att = [lax.dot_general(q_t[:, ks[h]], k_t[:, ks[h]], NT_DIMS, preferred_element_type=F32) for h in range(hp)]
    u_t = [[lax.dot_general(v[c * L:(c + 1) * L, vs[h]], k_d[c][:, ks[h]], TN_DIMS, preferred_element_type=F32)
            for c in range(nch)] for h in range(hp)]
    o = [jnp.dot(jnp.where(causal, att[h], 0.0).astype(BF16), v[:, vs[h]], preferred_element_type=F32)
         for h in range(hp)]

    states = []
    for h in range(hp):
        st = st_ref[h]
        per_chunk = []
        for c in range(nch):
            per_chunk.append(st.astype(BF16))
            st = st * dec[c][:, ks[h]] + u_t[h][c]
        st_ref[h] = st
        states.append(per_chunk)
    inter = [[lax.dot_general(q_t[c * L:(c + 1) * L, ks[h]], states[h][c], NT_DIMS, preferred_element_type=F32)
              for c in range(nch)] for h in range(hp)]
    for h in range(hp):
        oh = o[h] + jnp.concatenate(inter[h], axis=0)
        oh = oh * lax.rsqrt(jnp.mean(oh * oh, axis=-1, keepdims=True) + NORM_EPS) * ng
        o_ref[:, vs[h]] = (oh * jax.nn.silu(g_ref[:, vs[h]])).astype(o_ref.dtype)


def _gla(z, la, norm_g, *, batch, seq, q_col, k_col, v_col, g_col, tc=256, hp=2):
    dk, dv = GLA_DK, GLA_DV
    tc = min(tc, seq)
    assert seq % tc == 0 and tc % GLA_CHUNK == 0 and GLA_HEADS % hp == 0
    nt = seq // tc
    qk = lambda col: pl.BlockSpec((tc, hp * dk), lambda b, h, i: (b * nt + i, col + h))
    vg = lambda col: pl.BlockSpec((tc, hp * dv), lambda b, h, i: (b * nt + i, col + h))
    return pl.pallas_call(
        functools.partial(_gla_kernel, tc=tc),
        out_shape=jax.ShapeDtypeStruct((batch * seq, GLA_HEADS * dv), BF16),
        grid=(batch, GLA_HEADS // hp, nt),
        in_specs=[qk(q_col), qk(k_col), vg(v_col), vg(g_col), qk(0),
                  pl.BlockSpec((1, dv), lambda b, h, i: (0, 0))],
        out_specs=pl.BlockSpec((tc, hp * dv), lambda b, h, i: (b * nt + i, h)),
        scratch_shapes=[pltpu.VMEM((hp, dv, dk), F32)],
        compiler_params=_params("parallel", "parallel", "arbitrary"),
        name="gla",
    )(z, z, z, z, la, norm_g.reshape(1, dv))


def _pad_cols(w, n):
    return jnp.pad(w, ((0, 0), (0, n - w.shape[1])))


def _ffn_sublayer(xf, xb, wg, wu, wd_b, layer, ln_g, ln_b):
    gu = _ffn_gu(xb, wg, wu, layer)
    h = _matmul(gu, wd_b, F32, layer=layer, tm=512, tn=512)
    return _res_ln(xf, h, ln_g, ln_b, 0.5)


def _alibi_slopes(n):
    return jnp.exp2(-8.0 * jnp.arange(1, n + 1, dtype=F32) / n)


def _attn_sublayer(xf, xb, w_in, w_out, pos_k, w1_k, w2_k, pos_v, w1_v, w2_v, ln_g, ln_b, *, batch, seq):
    dh = HEAD_DIM
    nq = NSA_HEADS * dh
    kv = NSA_KV_GROUPS * dh
    nm = MOBA_HEADS * dh
    n_gate = NSA_HEADS * 3
    c_kc, c_ks, c_vs, c_kw, c_vw, c_gate = (nq + i * kv for i in (0, 2, 3, 4, 5, 6))
    c_mq = c_gate + n_gate
    c_mk, c_mv = c_mq + nm, c_mq + 2 * nm
    col = lambda a, n: w_in[:, a:a + n]
    w_n = jnp.concatenate([col(c_ks, kv), col(c_kw, kv), col(c_mk, nm)], axis=1).astype(BF16)
    w_t = jnp.concatenate([col(0, nq), col(c_vs, kv), col(c_vw, kv), col(c_mq, nm), col(c_mv, nm)],
                          axis=1).T.astype(BF16)
    w_f = col(c_kc, 2 * kv).astype(BF16)
    w_g = _pad_cols(col(c_gate, n_gate), LANE).T.astype(BF16)
    zn = _matmul(xb, w_n, BF16)
    zt = _matmul_nt(w_t, xb, BF16)
    zf = _matmul(xb, w_f, F32)
    gates_t = _matmul_nt(w_g, xb, F32, act="sigmoid")[:n_gate]
    gates_t = jnp.pad(gates_t.reshape(NSA_KV_GROUPS, 3 * NSA_GROUP_SIZE, batch * seq), ((0, 0), (0, 4), (0, 0)))

    slopes = _alibi_slopes(N_ATTN_HEADS)
    cmp_n, cmp_t = _nsa_compress(zf, jnp.stack([pos_k, pos_v]), jnp.stack([w1_k, w1_v]).astype(BF16),
                                 jnp.stack([w2_k, w2_v]).astype(BF16), batch=batch, seq=seq)
    u = nq // dh
    o_nsa = _nsa(zt, zn, cmp_n, cmp_t, gates_t, slopes[0::2], batch=batch, seq=seq,
                 qt_row=0, vst_row=u, vwt_row=u + 4, ks_col=0, kw_col=4)
    o_moba = _moba(zt, zn, slopes[1::2], batch=batch, seq=seq, qt_row=u + 8, vt_row=u + 24, k_col=8)
    w_o = w_out.astype(BF16)
    y = _matmul2(o_nsa, o_moba, w_o[:nq], w_o[nq:], F32)
    return _res_ln(xf, y, ln_g, ln_b, 1.0)


def _mix_sublayer(xf, xb, w_in, w_out, conv_w, w_a2, b_a, norm_g, ln_g, ln_b, *, batch, seq):
    cc = CONV_CHANNELS
    hk = GLA_HEADS * GLA_DK
    hv = GLA_HEADS * GLA_DV
    c_za = 3 * cc + 2 * hk + 2 * hv
    z = _matmul(xb, w_in[:, :c_za].astype(BF16), F32)
    w_za = _pad_cols(w_in[:, c_za:], LANE).astype(BF16)
    w_a2p = jnp.pad(w_a2, ((0, LANE - w_a2.shape[0]), (0, 0))).astype(BF16)
    la = _gate_decay(xb, w_za, w_a2p, b_a)
    y_conv = _short_conv(z, conv_w, seq=seq, gb_col=0, gc_col=cc // 512, h_col=2 * cc // 512)
    hp = 2
    y_gla = _gla(z, la, norm_g, batch=batch, seq=seq, q_col=3 * cc // (hp * GLA_DK),
                 k_col=(3 * cc + hk) // (hp * GLA_DK), v_col=(3 * cc + 2 * hk) // (hp * GLA_DV),
                 g_col=(3 * cc + 2 * hk + hv) // (hp * GLA_DV), hp=hp)
    w_o = w_out.astype(BF16)
    y = _matmul2(y_conv, y_gla, w_o[:cc], w_o[cc:], F32)
    return _res_ln(xf, y, ln_g, ln_b, 1.0)


def kernel(x, ln_g, ln_b, ffn_pre_wg, ffn_pre_wu, ffn_pre_wd, ffn_post_wg, ffn_post_wu, ffn_post_wd,
           att_w_in, att_w_out, nsa_pos_k, nsa_w1_k, nsa_w2_k, nsa_pos_v, nsa_w1_v, nsa_w2_v,
           mix_w_in, mix_w_out, conv_w, gla_w_a2, gla_b_a, gla_norm_g):
    batch, seq, d = x.shape
    xf = x.reshape(batch * seq, d)
    xb = xf.astype(BF16)
    pre_wd, post_wd = ffn_pre_wd.astype(BF16), ffn_post_wd.astype(BF16)
    for layer in range(DEPTH):
        xf, xb = _ffn_sublayer(xf, xb, ffn_pre_wg, ffn_pre_wu, pre_wd, layer, ln_g[layer, 0], ln_b[layer, 0])
        i = layer // 2
        if layer % 2 == 0:
            xf, xb = _attn_sublayer(xf, xb, att_w_in[i], att_w_out[i], nsa_pos_k[i], nsa_w1_k[i], nsa_w2_k[i],
                                    nsa_pos_v[i], nsa_w1_v[i], nsa_w2_v[i], ln_g[layer, 1], ln_b[layer, 1],
                                    batch=batch, seq=seq)
        else:
            xf, xb = _mix_sublayer(xf, xb, mix_w_in[i], mix_w_out[i], conv_w[i], gla_w_a2[i], gla_b_a[i],
                                   gla_norm_g[i], ln_g[layer, 1], ln_b[layer, 1], batch=batch, seq=seq)
        xf, xb = _ffn_sublayer(xf, xb, ffn_post_wg, ffn_post_wu, post_wd, layer, ln_g[layer, 2], ln_b[layer, 2])
    return xf.reshape(batch, seq, d)
```

```python
import functools
import math

import jax
import jax.numpy as jnp
from jax import lax
from jax.experimental import pallas as pl
from jax.experimental.pallas import tpu as pltpu

F32 = jnp.float32
BF16 = jnp.bfloat16

D_MODEL = 4096
DEPTH = 2
HEAD_DIM = 128
NSA_HEADS = 16
NSA_KV_GROUPS = 4
NSA_GROUP_SIZE = 4
NSA_CMP_STRIDE = 16
NSA_CMP_LEN = 32
NSA_SEL_LEN = 64
NSA_N_SEL = 16
NSA_WINDOW = 512
MOBA_HEADS = 16
MOBA_BLOCK = 256
MOBA_TOPK = 3
N_ATTN_HEADS = NSA_HEADS + MOBA_HEADS
CONV_CHANNELS = 2048
GLA_HEADS = 16
GLA_DK = 64
GLA_DV = 128
GLA_GATE_RANK = 16
GLA_GATE_TAU = 16.0
GLA_CHUNK = 64
D_FF = 11008
ALPHA = (2 * DEPTH) ** 0.25
LN_EPS = 1e-5
NORM_EPS = 1e-6
NEG_INF = -1e30
SEL_FORCE = 1e4
LOG2E = math.log2(math.e)

LANE = 128
ONES_ROWS = 16
VMEM_LIMIT = 60 * 1024 * 1024

NT_DIMS = (((1,), (1,)), ((), ()))
TN_DIMS = (((0,), (0,)), ((), ()))


def _params(*sem):
    return pltpu.CompilerParams(dimension_semantics=sem, vmem_limit_bytes=VMEM_LIMIT)


def _top_k_rows(vals, k):
    n = vals.shape[0]
    rid = lax.broadcasted_iota(jnp.int32, vals.shape, 0)
    sel = jnp.zeros(vals.shape, jnp.bool_)
    g = vals
    for _ in range(k):
        m = jnp.max(g, axis=0, keepdims=True)
        idx = jnp.min(jnp.where(g == m, rid, n), axis=0, keepdims=True)
        pick = rid == idx
        sel = sel | pick
        g = jnp.where(pick, -jnp.inf, g)
    return sel


def _split3(x):
    hi = x.astype(BF16)
    rest = x - hi.astype(F32)
    mid = rest.astype(BF16)
    return hi, mid, (rest - mid.astype(F32)).astype(BF16)


def _pv_with_sums(vt, p):
    ones = jnp.ones((ONES_ROWS, vt.shape[1]), BF16)
    return jnp.dot(jnp.concatenate([vt, ones], axis=0), p.astype(BF16), preferred_element_type=F32)


def _mm_kernel(a_ref, b_ref, *rest, res_coef):
    r = jnp.dot(a_ref[...], b_ref[0], preferred_element_type=F32)
    if res_coef is not None:
        r = ALPHA * rest[0][...] + res_coef * r
    o_ref = rest[-1]
    o_ref[...] = r.astype(o_ref.dtype)


def _matmul(a, b, out_dtype, *, layer=0, res=None, res_coef=None, tm=1024, tn=1024):
    if b.ndim == 2:
        b = b[None]
    m, k = a.shape
    n = b.shape[2]
    tm, tn = min(tm, m), min(tn, n)
    assert m % tm == 0 and n % tn == 0 and (res is None) == (res_coef is None)
    tile = pl.BlockSpec((tm, tn), lambda i, j: (i, j))
    return pl.pallas_call(
        functools.partial(_mm_kernel, res_coef=res_coef),
        out_shape=jax.ShapeDtypeStruct((m, n), out_dtype),
        grid=(m // tm, n // tn),
        in_specs=[pl.BlockSpec((tm, k), lambda i, j: (i, 0)),
                  pl.BlockSpec((1, k, tn), lambda i, j: (layer, 0, j))] + ([] if res is None else [tile]),
        out_specs=tile,
        compiler_params=_params("parallel", "arbitrary"),
        name="matmul",
    )(a, b, *([] if res is None else [res]))


def _mm_nt_kernel(w_ref, x_ref, o_ref, *, act):
    r = lax.dot_general(w_ref[...], x_ref[...], NT_DIMS, preferred_element_type=F32)
    if act == "sigmoid":
        r = jax.nn.sigmoid(r)
    o_ref[...] = r.astype(o_ref.dtype)


def _matmul_nt(wt, x, out_dtype, *, tn=1024, tm=1024, act=None):
    n, k = wt.shape
    m = x.shape[0]
    tn, tm = min(tn, n), min(tm, m)
    assert m % tm == 0 and n % tn == 0
    return pl.pallas_call(
        functools.partial(_mm_nt_kernel, act=act),
        out_shape=jax.ShapeDtypeStruct((n, m), out_dtype),
        grid=(m // tm, n // tn),
        in_specs=[pl.BlockSpec((tn, k), lambda i, j: (j, 0)),
                  pl.BlockSpec((tm, k), lambda i, j: (i, 0))],
        out_specs=pl.BlockSpec((tn, tm), lambda i, j: (j, i)),
        compiler_params=_params("parallel", "arbitrary"),
        name="matmul_nt",
    )(wt, x)


def _mm2_kernel(a1_ref, a2_ref, b1_ref, b2_ref, x_ref, o_ref, *, res_coef):
    r = jnp.dot(a1_ref[...], b1_ref[...], preferred_element_type=F32)
    r = r + jnp.dot(a2_ref[...], b2_ref[...], preferred_element_type=F32)
    o_ref[...] = ALPHA * x_ref[...] + res_coef * r


def _matmul2(a1, a2, b1, b2, res, res_coef, *, tm=1024, tn=1024):
    m, k1 = a1.shape
    k2 = a2.shape[1]
    n = b1.shape[1]
    tm, tn = min(tm, m), min(tn, n)
    assert m % tm == 0 and n % tn == 0
    tile = pl.BlockSpec((tm, tn), lambda i, j: (i, j))
    return pl.pallas_call(
        functools.partial(_mm2_kernel, res_coef=res_coef),
        out_shape=jax.ShapeDtypeStruct((m, n), F32),
        grid=(m // tm, n // tn),
        in_specs=[pl.BlockSpec((tm, k1), lambda i, j: (i, 0)),
                  pl.BlockSpec((tm, k2), lambda i, j: (i, 0)),
                  pl.BlockSpec((k1, tn), lambda i, j: (0, j)),
                  pl.BlockSpec((k2, tn), lambda i, j: (0, j)),
                  tile],
        out_specs=tile,
        compiler_params=_params("parallel", "arbitrary"),
        name="matmul2",
    )(a1, a2, b1, b2, res)


def _ffn_gu_kernel(x_ref, wg_ref, wu_ref, o_ref, wb_ref):
    tn = o_ref.shape[1]

    @pl.when(pl.program_id(1) == 0)
    def _():
        wb_ref[:, 0:tn] = wg_ref[0].astype(BF16)
        wb_ref[:, tn:2 * tn] = wu_ref[0].astype(BF16)

    r = jnp.dot(x_ref[...], wb_ref[...], preferred_element_type=F32)
    o_ref[...] = (jax.nn.silu(r[:, 0:tn]) * r[:, tn:2 * tn]).astype(o_ref.dtype)


def _ffn_gu(x, wg, wu, layer, *, tm=2048, tn=256):
    m, k = x.shape
    n = wg.shape[2]
    tm, tn = min(tm, m), min(tn, n)
    assert m % tm == 0 and n % tn == 0
    w_spec = pl.BlockSpec((1, k, tn), lambda j, i: (layer, 0, j))
    return pl.pallas_call(
        _ffn_gu_kernel,
        out_shape=jax.ShapeDtypeStruct((m, n), BF16),
        grid=(n // tn, m // tm),
        in_specs=[pl.BlockSpec((tm, k), lambda j, i: (i, 0)), w_spec, w_spec],
        out_specs=pl.BlockSpec((tm, tn), lambda j, i: (i, j)),
        scratch_shapes=[pltpu.VMEM((k, 2 * tn), BF16)],
        compiler_params=_params("parallel", "arbitrary"),
        name="ffn_gate_up",
    )(x, wg, wu)


def _ln_kernel(v_ref, g_ref, b_ref, of_ref, *maybe_ob_ref):
    v = v_ref[...]
    mu = jnp.mean(v, axis=-1, keepdims=True)
    d = v - mu
    var = jnp.mean(d * d, axis=-1, keepdims=True)
    y = d * lax.rsqrt(var + LN_EPS) * g_ref[...] + b_ref[...]
    of_ref[...] = y
    for ob_ref in maybe_ob_ref:
        ob_ref[...] = y.astype(BF16)


def _layer_norm(v, g, b, *, want_bf16=True, tm=256):
    m, d = v.shape
    tm = min(tm, m)
    assert m % tm == 0
    row = pl.BlockSpec((tm, d), lambda i: (i, 0))
    vec = pl.BlockSpec((1, d), lambda i: (0, 0))
    out_shape = [jax.ShapeDtypeStruct((m, d), F32)] + [jax.ShapeDtypeStruct((m, d), BF16)] * want_bf16
    out = pl.pallas_call(
        _ln_kernel,
        out_shape=tuple(out_shape),
        grid=(m // tm,),
        in_specs=[row, vec, vec],
        out_specs=tuple([row] * len(out_shape)),
        compiler_params=_params("parallel"),
        name="layernorm",
    )(v, g.reshape(1, d), b.reshape(1, d))
    return (out[0], out[1]) if want_bf16 else (out[0], None)


def _moba_kernel(slope_ref, qt_ref, k_ref, vt_ref, o_ref, kmean_ref, bias_ref, sel_ref,
                 *, nblk, blk, n_top, c2, hp):
    hg = pl.program_id(1)
    qi = pl.program_id(2)
    dh = HEAD_DIM
    kc = lax.broadcasted_iota(jnp.int32, (blk, blk), 0)
    qr = lax.broadcasted_iota(jnp.int32, (blk, blk), 1)
    slope2 = [slope_ref[hg * hp + h] * LOG2E for h in range(hp)]
    cols = [slice(h * dh, (h + 1) * dh) for h in range(hp)]

    @pl.when(qi == 0)
    def _():
        kc2 = lax.broadcasted_iota(jnp.int32, (2 * blk, blk), 0)
        qr2 = lax.broadcasted_iota(jnp.int32, (2 * blk, blk), 1)
        for h in range(hp):
            kf = k_ref[:, cols[h]].astype(F32).reshape(nblk, blk, dh)
            kmean_ref[h] = (jnp.sum(kf, axis=1) / blk).astype(BF16)
            bias_ref[h] = slope2[h] * (qr2 - kc2).astype(F32)

    qt = [qt_ref[cols[h], :] for h in range(hp)]
    gate = [jnp.dot(kmean_ref[h], qt[h], preferred_element_type=F32) for h in range(hp)]
    past = lax.broadcasted_iota(jnp.int32, (nblk, blk), 0) < qi
    for h in range(hp):
        sel = _top_k_rows(jnp.where(past, gate[h], NEG_INF), n_top) & past
        sel_ref[h] = sel.astype(F32)

    def scores(h, k0, nk):
        return (jnp.dot(k_ref[pl.ds(k0, nk), cols[h]], qt[h], preferred_element_type=F32) * c2
                - bias_ref[h, 0:nk, :])

    def pv(h, k0, nk, p):
        return _pv_with_sums(vt_ref[cols[h], pl.ds(k0, nk)], p)

    kq = pl.multiple_of(qi * blk, blk)
    s0 = [jnp.where(kc <= qr, scores(h, kq, blk), NEG_INF) for h in range(hp)]
    m0 = [jnp.max(s0[h], axis=0, keepdims=True) for h in range(hp)]
    acc0 = [pv(h, kq, blk, jnp.exp2(s0[h] - m0[h])) for h in range(hp)]
    init = []
    for h in range(hp):
        init += [m0[h], acc0[h]]

    def body(t, carry):
        j = qi - 1 - 2 * t
        pb = jnp.maximum(j - 1, 0)
        k0 = pl.multiple_of(pb * blk, blk)
        s = [scores(h, k0, 2 * blk) for h in range(hp)]
        valid = pb + lax.broadcasted_iota(jnp.int32, (2, 1), 0) <= j
        m_new, scale, p = [], [], []
        for h in range(hp):
            m_i = carry[2 * h]
            off = slope2[h] * ((qi - pb) * blk).astype(F32)
            rows = jnp.concatenate([sel_ref[h, pl.ds(pb, 1), :], sel_ref[h, pl.ds(pb + 1, 1), :]], axis=0)
            picked = (rows > 0.5) & valid
            blk_max = jnp.max(s[h].reshape(2, blk, blk), axis=1)
            m_pair = jnp.max(jnp.where(picked, blk_max, NEG_INF), axis=0, keepdims=True)
            m_h = jnp.maximum(m_i, m_pair - off)
            shift = jnp.where(picked, m_h + off, -NEG_INF)
            shift = jnp.broadcast_to(shift[:, None, :], (2, blk, blk)).reshape(2 * blk, blk)
            m_new.append(m_h)
            scale.append(jnp.exp2(m_i - m_h))
            p.append(jnp.exp2(s[h] - shift))
        upd = [pv(h, k0, 2 * blk, p[h]) for h in range(hp)]
        out = []
        for h in range(hp):
            out += [m_new[h], scale[h] * carry[2 * h + 1] + upd[h]]
        return tuple(out)

    fin = lax.fori_loop(0, (qi + 1) // 2, body, tuple(init))
    for h in range(hp):
        acc = fin[2 * h + 1]
        o_t = acc[0:dh] / jnp.maximum(acc[dh:dh + 1], 1e-30)
        o_ref[:, cols[h]] = o_t.T.astype(o_ref.dtype)


def _moba(zt, zn, slopes, *, batch, seq, qt_row, vt_row, k_col, hp=4):
    blk = MOBA_BLOCK
    assert seq % blk == 0 and MOBA_HEADS % hp == 0
    assert qt_row % hp == 0 and vt_row % hp == 0 and k_col % hp == 0
    nblk = seq // blk
    n_top = min(MOBA_TOPK, nblk - 1)
    assert n_top > 0
    dh = HEAD_DIM
    kern = functools.partial(_moba_kernel, nblk=nblk, blk=blk, n_top=n_top, c2=dh ** -0.5 * LOG2E, hp=hp)
    return pl.pallas_call(
        kern,
        out_shape=jax.ShapeDtypeStruct((batch * seq, MOBA_HEADS * dh), BF16),
        grid=(batch, MOBA_HEADS // hp, nblk),
        in_specs=[pl.BlockSpec(memory_space=pltpu.SMEM),
                  pl.BlockSpec((hp * dh, blk), lambda b, h, i: (qt_row // hp + h, b * nblk + i)),
                  pl.BlockSpec((seq, hp * dh), lambda b, h, i: (b, k_col // hp + h)),
                  pl.BlockSpec((hp * dh, seq), lambda b, h, i: (vt_row // hp + h, b))],
        out_specs=pl.BlockSpec((blk, hp * dh), lambda b, h, i: (b * nblk + i, h)),
        scratch_shapes=[pltpu.VMEM((hp, nblk, dh), BF16),
                        pltpu.VMEM((hp, 2 * blk, blk), F32),
                        pltpu.VMEM((hp, nblk, blk), F32)],
        compiler_params=_params("parallel", "parallel", "arbitrary"),
        name="moba_attention",
    )(slopes, zt, zn, zt)


def _compress_kernel(x_ref, pos_ref, w1_ref, w2_ref, o_ref, ot_ref, *, nc):
    dh = x_ref.shape[-1]
    half = NSA_CMP_STRIDE
    acc_lo = jnp.zeros((nc, dh), F32)
    acc_hi = jnp.zeros((nc, dh), F32)
    for p in range(half):
        xp = x_ref[pl.ds(p, nc, stride=half), :]
        lo = (xp + pos_ref[0, p:p + 1, :]).astype(BF16)
        hi = (xp + pos_ref[0, half + p:half + p + 1, :]).astype(BF16)
        acc_lo += jnp.dot(lo, w1_ref[0, p * dh:(p + 1) * dh, :], preferred_element_type=F32)
        acc_hi += jnp.dot(hi, w1_ref[0, (half + p) * dh:(half + p + 1) * dh, :], preferred_element_type=F32)
    pre = acc_lo + pltpu.roll(acc_hi, nc - 1, 0)
    hid = jax.nn.gelu(pre)
    out = jnp.dot(hid.astype(BF16), w2_ref[0], preferred_element_type=F32)
    o_ref[0, 0] = out.astype(o_ref.dtype)
    ot_ref[0, 0] = out.T.astype(ot_ref.dtype)


def _nsa_compress(zf, pos, w1, w2, *, batch, seq):
    dh = HEAD_DIM
    g = NSA_KV_GROUPS
    nc = seq // NSA_CMP_STRIDE
    return pl.pallas_call(
        functools.partial(_compress_kernel, nc=nc),
        out_shape=(jax.ShapeDtypeStruct((batch, 2 * g, nc, dh), BF16),
                   jax.ShapeDtypeStruct((batch, 2 * g, dh, nc), BF16)),
        grid=(batch, 2 * g),
        in_specs=[pl.BlockSpec((seq, dh), lambda b, c: (b, c)),
                  pl.BlockSpec((1, NSA_CMP_LEN, dh), lambda b, c: (c // g, 0, 0)),
                  pl.BlockSpec((1, NSA_CMP_LEN * dh, dh), lambda b, c: (c // g, 0, 0)),
                  pl.BlockSpec((1, dh, dh), lambda b, c: (c // g, 0, 0))],
        out_specs=(pl.BlockSpec((1, 1, nc, dh), lambda b, c: (b, c, 0, 0)),
                   pl.BlockSpec((1, 1, dh, nc), lambda b, c: (b, c, 0, 0))),
        compiler_params=_params("parallel", "arbitrary"),
        name="nsa_compress",
    )(zf, pos, w1, w2)


def _nsa_kernel(slope_ref, qt_ref, kc_ref, vct_ref, ks_ref, vst_ref, kw_ref, vwt_ref, gate_ref, o_ref,
                sel_ref, bias_ref, member_ref, ecmp_ref, flag_ref, *, tq, tk, nc, ns, k_top, c2):
    g = pl.program_id(1)
    qi = pl.program_id(2)
    dh = HEAD_DIM
    nr = NSA_GROUP_SIZE
    nl = nr * tq
    bpt = tk // NSA_SEL_LEN
    q0 = qi * tq

    lane = lax.broadcasted_iota(jnp.int32, (1, nl), 1)
    tl = q0 + lane % tq
    head = lane // tq
    slope2 = jnp.zeros((1, nl), F32)
    for r in range(nr):
        slope2 = jnp.where(head == r, slope_ref[g * nr + r] * LOG2E, slope2)

    @pl.when(qi == 0)
    def _():
        kc_i = lax.broadcasted_iota(jnp.int32, (2 * tk, nl), 0)
        bias_ref[...] = slope2 * (lane % tq - kc_i).astype(F32)
        sj = lax.broadcasted_iota(jnp.int32, (ns, nc), 0) * NSA_SEL_LEN
        ci = lax.broadcasted_iota(jnp.int32, (ns, nc), 1) * NSA_CMP_STRIDE
        member_ref[...] = ((ci < sj + NSA_SEL_LEN) & (ci + NSA_CMP_LEN > sj)).astype(BF16)
        ecmp_ref[...] = slope2 * (lax.broadcasted_iota(jnp.int32, (nc, nl), 0) * NSA_CMP_STRIDE
                                  + (NSA_CMP_LEN - 1)).astype(F32)

    qb = qt_ref[...]
    qt = jnp.concatenate([qb[r * dh:(r + 1) * dh, :] for r in range(nr)], axis=1)

    span = NSA_WINDOW + tq
    w0 = pl.multiple_of(jnp.maximum(q0 - NSA_WINDOW, 0), tq)
    raw_cmp = jnp.dot(kc_ref[0, 0], qt, preferred_element_type=F32)
    raw_win = jnp.dot(kw_ref[pl.ds(w0, span), :], qt, preferred_element_type=F32)

    cend = lax.broadcasted_iota(jnp.int32, (nc, 1), 0) * NSA_CMP_STRIDE + (NSA_CMP_LEN - 1)
    s = (raw_cmp * c2 + ecmp_ref[...]) - slope2 * tl.astype(F32)
    vis = cend <= tl
    s = jnp.where(vis, s, NEG_INF)
    e = jnp.where(vis, jnp.exp2(s - jnp.max(s, axis=0, keepdims=True)), 0.0)
    p_cmp = e * (1.0 / jnp.maximum(jnp.sum(e, axis=0, keepdims=True), 1e-30))
    o_cmp = jnp.dot(vct_ref[0, 0], p_cmp.astype(BF16), preferred_element_type=F32)

    p_sum = p_cmp[:, 0:tq]
    for r in range(1, nr):
        p_sum = p_sum + p_cmp[:, r * tq:(r + 1) * tq]
    member = member_ref[...]
    imp = sum(jnp.dot(member, part, preferred_element_type=F32) for part in _split3(p_sum))

    rel = (w0 + lax.broadcasted_iota(jnp.int32, (span, 1), 0)) - tl
    s = (raw_win * c2 - bias_ref[0:span, :]) + slope2 * (w0 - q0).astype(F32)
    s = jnp.where((rel <= 0) & (rel > -NSA_WINDOW), s, NEG_INF)
    acc = _pv_with_sums(vwt_ref[:, pl.ds(w0, span)], jnp.exp2(s - jnp.max(s, axis=0, keepdims=True)))
    o_win = acc[0:dh] / jnp.maximum(acc[dh:dh + 1], 1e-30)

    own = (q0 + lax.broadcasted_iota(jnp.int32, (1, tq), 1)) // NSA_SEL_LEN
    sid = lax.broadcasted_iota(jnp.int32, (ns, tq), 0)
    forced = (sid == 0) | (sid == own) | (sid == own - 1)
    imp = jnp.where(forced, imp + SEL_FORCE, jnp.where(sid > own, -1.0, imp))
    sel_ref[...] = _top_k_rows(imp, k_top).astype(F32)

    def sel_rows(b0, nb):
        rows = sel_ref[pl.ds(pl.multiple_of(b0, bpt), nb), :]
        return jnp.concatenate([rows] * nr, axis=1)

    def scores(k0, nk):
        return jnp.dot(ks_ref[pl.ds(k0, nk), :], qt, preferred_element_type=F32) * c2 - bias_ref[0:nk, :]

    def blockwise(x):
        nb = x.shape[0]
        return jnp.broadcast_to(x[:, None, :], (nb, NSA_SEL_LEN, nl)).reshape(nb * NSA_SEL_LEN, nl)

    jd = q0 // tk
    kd = pl.multiple_of(jd * tk, tk)
    kpos = kd + lax.broadcasted_iota(jnp.int32, (tk, 1), 0)
    off = slope2 * (q0 - kd).astype(F32)
    ok = (blockwise(sel_rows(jd * bpt, bpt)) > 0.5) & (kpos <= tl)
    s = jnp.where(ok, scores(kd, tk), NEG_INF)
    m0 = jnp.max(s, axis=0, keepdims=True) - off
    acc0 = _pv_with_sums(vst_ref[:, pl.ds(kd, tk)], jnp.exp2(s - (m0 + off)))

    def sweep(i, live, carry):
        m_i, acc = carry
        k0 = pl.multiple_of(i * (2 * tk), 2 * tk)
        s = scores(k0, 2 * tk)
        off = slope2 * (q0 - k0).astype(F32)
        rows = jnp.concatenate([live] * nr, axis=1) > 0.5
        blk_max = jnp.max(s.reshape(2 * bpt, NSA_SEL_LEN, nl), axis=1)
        m_tile = jnp.max(jnp.where(rows, blk_max, NEG_INF), axis=0, keepdims=True)
        m_new = jnp.maximum(m_i, m_tile - off)
        shift = jnp.where(rows, m_new + off, -NEG_INF)
        upd = _pv_with_sums(vst_ref[:, pl.ds(k0, 2 * tk)], jnp.exp2(s - blockwise(shift)))
        return m_new, jnp.exp2(m_i - m_new) * acc + upd

    vis_sel = jnp.where(sid < jd * bpt, sel_ref[...], 0.0)
    pair_any = jnp.max(jnp.max(vis_sel.reshape(ns // (2 * bpt), 2 * bpt, tq), axis=1), axis=1, keepdims=True)
    for i in range(ns // (2 * bpt)):
        flag_ref[i] = (pair_any[i, 0] > 0.5).astype(jnp.int32)

    def body(i, carry):
        b0 = pl.multiple_of(i * (2 * bpt), 2 * bpt)
        bid = b0 + lax.broadcasted_iota(jnp.int32, (2 * bpt, 1), 0)
        live = jnp.where(bid < jd * bpt, sel_ref[pl.ds(b0, 2 * bpt), :], 0.0)
        return lax.cond(flag_ref[i] > 0, lambda c: sweep(i, live, c), lambda c: c, carry)

    _, acc = lax.fori_loop(0, (jd + 1) // 2, body, (m0, acc0))
    o_slc = acc[0:dh] / jnp.maximum(acc[dh:dh + 1], 1e-30)

    gt = gate_ref[0]
    for r in range(nr):
        sl = slice(r * tq, (r + 1) * tq)
        o = (gt[3 * r:3 * r + 1] * o_cmp[:, sl] + gt[3 * r + 1:3 * r + 2] * o_slc[:, sl]
             + gt[3 * r + 2:3 * r + 3] * o_win[:, sl])
        o_ref[:, r * dh:(r + 1) * dh] = o.T.astype(o_ref.dtype)


def _nsa(zt, zn, cmp_n, cmp_t, gates_t, slopes, *, batch, seq, qt_row, vst_row, vwt_row, ks_col, kw_col,
         tq=128, tk=512):
    dh = HEAD_DIM
    g = NSA_KV_GROUPS
    nr = NSA_GROUP_SIZE
    tk = min(tk, seq)
    assert seq % (2 * tk) == 0 and tk % tq == 0 and NSA_WINDOW % tq == 0
    assert NSA_WINDOW + tq <= min(seq, 2 * tk)
    nq = seq // tq
    nc = seq // NSA_CMP_STRIDE
    ns = seq // NSA_SEL_LEN
    k_top = min(NSA_N_SEL, ns)
    kern = functools.partial(_nsa_kernel, tq=tq, tk=tk, nc=nc, ns=ns, k_top=k_top, c2=dh ** -0.5 * LOG2E)
    key_blk = lambda col: pl.BlockSpec((seq, dh), lambda b, gg, i: (b, col + gg))
    val_blk = lambda row: pl.BlockSpec((dh, seq), lambda b, gg, i: (row + gg, b))
    return pl.pallas_call(
        kern,
        out_shape=jax.ShapeDtypeStruct((batch * seq, NSA_HEADS * dh), BF16),
        grid=(batch, g, nq),
        in_specs=[pl.BlockSpec(memory_space=pltpu.SMEM),
                  pl.BlockSpec((nr * dh, tq), lambda b, gg, i: (qt_row // nr + gg, b * nq + i)),
                  pl.BlockSpec((1, 1, nc, dh), lambda b, gg, i: (b, gg, 0, 0)),
                  pl.BlockSpec((1, 1, dh, nc), lambda b, gg, i: (b, g + gg, 0, 0)),
                  key_blk(ks_col), val_blk(vst_row), key_blk(kw_col), val_blk(vwt_row),
                  pl.BlockSpec((1, gates_t.shape[1], tq), lambda b, gg, i: (gg, 0, b * nq + i))],
        out_specs=pl.BlockSpec((tq, nr * dh), lambda b, gg, i: (b * nq + i, gg)),
        scratch_shapes=[pltpu.VMEM((ns, tq), F32),
                        pltpu.VMEM((2 * tk, nr * tq), F32),
                        pltpu.VMEM((ns, nc), BF16),
                        pltpu.VMEM((nc, nr * tq), F32),
                        pltpu.SMEM((ns // (2 * tk // NSA_SEL_LEN),), jnp.int32)],
        compiler_params=_params("parallel", "parallel", "arbitrary"),
        name="nsa_attention",
    )(slopes, zt, cmp_n, cmp_t, zn, zt, zn, zt, gates_t)


def _conv_kernel(gb_ref, gc_ref, h_ref, gcp_ref, hp_ref, w_ref, o_ref, *, tiles_per_seq):
    i = pl.program_id(0)
    u = gc_ref[...] * h_ref[...]
    prev = gcp_ref[...] * hp_ref[...]
    prev = jnp.where(i % tiles_per_seq == 0, 0.0, prev)
    r = lax.broadcasted_iota(jnp.int32, u.shape, 0)
    u1 = jnp.where(r == 0, prev[7:8], pltpu.roll(u, 1, 0))
    u2 = jnp.where(r == 0, prev[6:7], jnp.where(r == 1, prev[7:8], pltpu.roll(u, 2, 0)))
    w = w_ref[...]
    y = gb_ref[...] * (w[0:1] * u2 + w[1:2] * u1 + w[2:3] * u)
    o_ref[...] = y.astype(o_ref.dtype)


def _short_conv(z, w, *, seq, gb_col, gc_col, h_col, tt=512, tc=512):
    m = z.shape[0]
    c = w.shape[1]
    tt = min(tt, seq)
    assert seq % tt == 0 and c % tc == 0 and tt % 8 == 0
    sub = tt // 8
    cur = lambda col: pl.BlockSpec((tt, tc), lambda i, j: (i, col + j))
    prv = lambda col: pl.BlockSpec((8, tc), lambda i, j: (jnp.maximum(i * sub - 1, 0), col + j))
    return pl.pallas_call(
        functools.partial(_conv_kernel, tiles_per_seq=seq // tt),
        out_shape=jax.ShapeDtypeStruct((m, c), BF16),
        grid=(m // tt, c // tc),
        in_specs=[cur(gb_col), cur(gc_col), cur(h_col), prv(gc_col), prv(h_col),
                  pl.BlockSpec((3, tc), lambda i, j: (0, j))],
        out_specs=pl.BlockSpec((tt, tc), lambda i, j: (i, j)),
        compiler_params=_params("parallel", "arbitrary"),
        name="short_conv",
    )(z, z, z, z, z, w)


def _gate_decay_kernel(x_ref, w1_ref, w2_ref, b_ref, o_ref):
    za = jnp.dot(x_ref[...], w1_ref[...], preferred_element_type=F32)
    pre = jnp.dot(za.astype(BF16), w2_ref[...], preferred_element_type=F32) + b_ref[...]
    ls = -(jnp.maximum(-pre, 0.0) + jnp.log1p(jnp.exp(-jnp.abs(pre))))
    o_ref[...] = ls / GLA_GATE_TAU


def _gate_decay(x, w1, w2, b, *, tm=1024):
    m, k = x.shape
    r = w1.shape[1]
    n = w2.shape[1]
    tm = min(tm, m)
    return pl.pallas_call(
        _gate_decay_kernel,
        out_shape=jax.ShapeDtypeStruct((m, n), F32),
        grid=(m // tm,),
        in_specs=[pl.BlockSpec((tm, k), lambda i: (i, 0)),
                  pl.BlockSpec((k, r), lambda i: (0, 0)),
                  pl.BlockSpec((r, n), lambda i: (0, 0)),
                  pl.BlockSpec((1, n), lambda i: (0, 0))],
        out_specs=pl.BlockSpec((tm, n), lambda i: (i, 0)),
        compiler_params=_params("parallel"),
        name="gla_gate_decay",
    )(x, w1, w2, b.reshape(1, n))


def _gla_kernel(q_ref, k_ref, v_ref, g_ref, la_ref, ng_ref, o_ref, st_ref, *, tc):
    L = GLA_CHUNK
    dk, dv = GLA_DK, GLA_DV
    hp = q_ref.shape[-1] // dk

    @pl.when(pl.program_id(2) == 0)
    def _():
        st_ref[...] = jnp.zeros_like(st_ref)

    nch = tc // L
    ri = lax.broadcasted_iota(jnp.int32, (tc, tc), 0)
    ci = lax.broadcasted_iota(jnp.int32, (tc, tc), 1)
    causal = (ci <= ri) & (ri // L == ci // L)
    tri = causal.astype(BF16)
    ng = ng_ref[...]

    b = sum(jnp.dot(tri, part, preferred_element_type=F32) for part in _split3(la_ref[...]))
    b3 = b.reshape(nch, L, hp * dk)
    b_last = b3[:, L - 1:L, :]
    q_t = ((q_ref[...] * dk ** -0.5) * jnp.exp(b)).astype(BF16)
    k_raw = k_ref[...]
    k_t = (k_raw * jnp.exp(-b)).astype(BF16)
    k_d = (k_raw.reshape(nch, L, hp * dk) * jnp.exp(b_last - b3)).astype(BF16)
    dec = jnp.exp(b_last)
    v = v_ref[...].astype(BF16)

    ks = [slice(h * dk, (h + 1) * dk) for h in range(hp)]
    vs = [slice(h * dv, (h + 1) * dv) for h in range(hp)]
    att = [lax.dot_general(q_t[:, ks[h]], k_t[:, ks[h]], NT_DIMS, preferred_element_type=F32) for h in range(hp)]
    u_t = [[lax.dot_general(v[c * L:(c + 1) * L, vs[h]], k_d[c][:, ks[h]], TN_DIMS, preferred_element_type=F32)
            for c in range(nch)] for h in range(hp)]
    o = [jnp.dot(jnp.where(causal, att[h], 0.0).astype(BF16), v[:, vs[h]], preferred_element_type=F32)
         for h in range(hp)]

    states = []
    for h in range(hp):
        st = st_ref[h]
        per_chunk = []
        for c in range(nch):
            per_chunk.append(st.astype(BF16))
            st = st * dec[c][:, ks[h]] + u_t[h][c]
        st_ref[h] = st
        states.append(per_chunk)
    inter = [[lax.dot_general(q_t[c * L:(c + 1) * L, ks[h]], states[h][c], NT_DIMS, preferred_element_type=F32)
              for c in range(nch)] for h in range(hp)]
    for h in range(hp):
        oh = o[h] + jnp.concatenate(inter[h], axis=0)
        oh = oh * lax.rsqrt(jnp.mean(oh * oh, axis=-1, keepdims=True) + NORM_EPS) * ng
        o_ref[:, vs[h]] = (oh * jax.nn.silu(g_ref[:, vs[h]])).astype(o_ref.dtype)


def _gla(z, la, norm_g, *, batch, seq, q_col, k_col, v_col, g_col, tc=256, hp=2):
    dk, dv = GLA_DK, GLA_DV
    tc = min(tc, seq)
    assert seq % tc == 0 and tc % GLA_CHUNK == 0 and GLA_HEADS % hp == 0
    nt = seq // tc
    qk = lambda col: pl.BlockSpec((tc, hp * dk), lambda b, h, i: (b * nt + i, col + h))
    vg = lambda col: pl.BlockSpec((tc, hp * dv), lambda b, h, i: (b * nt + i, col + h))
    return pl.pallas_call(
        functools.partial(_gla_kernel, tc=tc),
        out_shape=jax.ShapeDtypeStruct((batch * seq, GLA_HEADS * dv), BF16),
        grid=(batch, GLA_HEADS // hp, nt),
        in_specs=[qk(q_col), qk(k_col), vg(v_col), vg(g_col), qk(0),
                  pl.BlockSpec((1, dv), lambda b, h, i: (0, 0))],
        out_specs=pl.BlockSpec((tc, hp * dv), lambda b, h, i: (b * nt + i, h)),
        scratch_shapes=[pltpu.VMEM((hp, dv, dk), F32)],
        compiler_params=_params("parallel", "parallel", "arbitrary"),
        name="gla",
    )(z, z, z, z, la, norm_g.reshape(1, dv))


def _pad_cols(w, n):
    return jnp.pad(w, ((0, 0), (0, n - w.shape[1])))


def _ffn_sublayer(xf, xb, wg, wu, wd_b, layer, ln_g, ln_b, want_bf16=True):
    gu = _ffn_gu(xb, wg, wu, layer)
    v = _matmul(gu, wd_b, F32, layer=layer, res=xf, res_coef=0.5, tm=512, tn=512)
    return _layer_norm(v, ln_g, ln_b, want_bf16=want_bf16)


def _alibi_slopes(n):
    return jnp.exp2(-8.0 * jnp.arange(1, n + 1, dtype=F32) / n)


def _attn_sublayer(xf, xb, w_in, w_out, pos_k, w1_k, w2_k, pos_v, w1_v, w2_v, ln_g, ln_b, *, batch, seq):
    dh = HEAD_DIM
    nq = NSA_HEADS * dh
    kv = NSA_KV_GROUPS * dh
    nm = MOBA_HEADS * dh
    n_gate = NSA_HEADS * 3
    c_kc, c_ks, c_vs, c_kw, c_vw, c_gate = (nq + i * kv for i in (0, 2, 3, 4, 5, 6))
    c_mq = c_gate + n_gate
    c_mk, c_mv = c_mq + nm, c_mq + 2 * nm
    col = lambda a, n: w_in[:, a:a + n]
    w_n = jnp.concatenate([col(c_ks, kv), col(c_kw, kv), col(c_mk, nm)], axis=1).astype(BF16)
    w_t = jnp.concatenate([col(0, nq), col(c_vs, kv), col(c_vw, kv), col(c_mq, nm), col(c_mv, nm)],
                          axis=1).T.astype(BF16)
    w_f = col(c_kc, 2 * kv).astype(BF16)
    w_g = _pad_cols(col(c_gate, n_gate), LANE).T.astype(BF16)
    zn = _matmul(xb, w_n, BF16)
    zt = _matmul_nt(w_t, xb, BF16)
    zf = _matmul(xb, w_f, F32)
    gates_t = _matmul_nt(w_g, xb, F32, act="sigmoid")[:n_gate]
    gates_t = jnp.pad(gates_t.reshape(NSA_KV_GROUPS, 3 * NSA_GROUP_SIZE, batch * seq), ((0, 0), (0, 4), (0, 0)))

    slopes = _alibi_slopes(N_ATTN_HEADS)
    cmp_n, cmp_t = _nsa_compress(zf, jnp.stack([pos_k, pos_v]), jnp.stack([w1_k, w1_v]).astype(BF16),
                                 jnp.stack([w2_k, w2_v]).astype(BF16), batch=batch, seq=seq)
    u = nq // dh
    o_nsa = _nsa(zt, zn, cmp_n, cmp_t, gates_t, slopes[0::2], batch=batch, seq=seq,
                 qt_row=0, vst_row=u, vwt_row=u + 4, ks_col=0, kw_col=4)
    o_moba = _moba(zt, zn, slopes[1::2], batch=batch, seq=seq, qt_row=u + 8, vt_row=u + 24, k_col=8)
    w_o = w_out.astype(BF16)
    v = _matmul2(o_nsa, o_moba, w_o[:nq], w_o[nq:], xf, 1.0)
    return _layer_norm(v, ln_g, ln_b)


def _mix_sublayer(xf, xb, w_in, w_out, conv_w, w_a2, b_a, norm_g, ln_g, ln_b, *, batch, seq):
    cc = CONV_CHANNELS
    hk = GLA_HEADS * GLA_DK
    hv = GLA_HEADS * GLA_DV
    c_za = 3 * cc + 2 * hk + 2 * hv
    z = _matmul(xb, w_in[:, :c_za].astype(BF16), F32)
    w_za = _pad_cols(w_in[:, c_za:], LANE).astype(BF16)
    w_a2p = jnp.pad(w_a2, ((0, LANE - w_a2.shape[0]), (0, 0))).astype(BF16)
    la = _gate_decay(xb, w_za, w_a2p, b_a)
    y_conv = _short_conv(z, conv_w, seq=seq, gb_col=0, gc_col=cc // 512, h_col=2 * cc // 512)
    hp = 2
    y_gla = _gla(z, la, norm_g, batch=batch, seq=seq, q_col=3 * cc // (hp * GLA_DK),
                 k_col=(3 * cc + hk) // (hp * GLA_DK), v_col=(3 * cc + 2 * hk) // (hp * GLA_DV),
                 g_col=(3 * cc + 2 * hk + hv) // (hp * GLA_DV), hp=hp)
    w_o = w_out.astype(BF16)
    v = _matmul2(y_conv, y_gla, w_o[:cc], w_o[cc:], xf, 1.0)
    return _layer_norm(v, ln_g, ln_b)


def kernel(x, ln_g, ln_b, ffn_pre_wg, ffn_pre_wu, ffn_pre_wd, ffn_post_wg, ffn_post_wu, ffn_post_wd,
           att_w_in, att_w_out, nsa_pos_k, nsa_w1_k, nsa_w2_k, nsa_pos_v, nsa_w1_v, nsa_w2_v,
           mix_w_in, mix_w_out, conv_w, gla_w_a2, gla_b_a, gla_norm_g):
    batch, seq, d = x.shape
    xf = x.reshape(batch * seq, d)
    xb = xf.astype(BF16)
    pre_wd, post_wd = ffn_pre_wd.astype(BF16), ffn_post_wd.astype(BF16)
    for layer in range(DEPTH):
        xf, xb = _ffn_sublayer(xf, xb, ffn_pre_wg, ffn_pre_wu, pre_wd, layer, ln_g[layer, 0], ln_b[layer, 0])
        i = layer // 2
        if layer % 2 == 0:
            xf, xb = _attn_sublayer(xf, xb, att_w_in[i], att_w_out[i], nsa_pos_k[i], nsa_w1_k[i], nsa_w2_k[i],
                                    nsa_pos_v[i], nsa_w1_v[i], nsa_w2_v[i], ln_g[layer, 1], ln_b[layer, 1],
                                    batch=batch, seq=seq)
        else:
            xf, xb = _mix_sublayer(xf, xb, mix_w_in[i], mix_w_out[i], conv_w[i], gla_w_a2[i], gla_b_a[i],
                                   gla_norm_g[i], ln_g[layer, 1], ln_b[layer, 1], batch=batch, seq=seq)
        xf, xb = _ffn_sublayer(xf, xb, ffn_post_wg, ffn_post_wu, post_wd, layer, ln_g[layer, 2], ln_b[layer, 2],
                               want_bf16=layer + 1 < DEPTH)
    return xf.reshape(batch, seq, d)
```

```python
import functools
import math

import jax
import jax.numpy as jnp
from jax import lax
from jax.experimental import pallas as pl
from jax.experimental.pallas import tpu as pltpu

F32 = jnp.float32
BF16 = jnp.bfloat16

D_MODEL = 4096
DEPTH = 2
HEAD_DIM = 128
NSA_HEADS = 16
NSA_KV_GROUPS = 4
NSA_GROUP_SIZE = 4
NSA_CMP_STRIDE = 16
NSA_CMP_LEN = 32
NSA_SEL_LEN = 64
NSA_N_SEL = 16
NSA_WINDOW = 512
MOBA_HEADS = 16
MOBA_BLOCK = 256
MOBA_TOPK = 3
N_ATTN_HEADS = NSA_HEADS + MOBA_HEADS
CONV_CHANNELS = 2048
GLA_HEADS = 16
GLA_DK = 64
GLA_DV = 128
GLA_GATE_RANK = 16
GLA_GATE_TAU = 16.0
GLA_CHUNK = 64
D_FF = 11008
ALPHA = (2 * DEPTH) ** 0.25
LN_EPS = 1e-5
NORM_EPS = 1e-6
NEG_INF = -1e30
SEL_FORCE = 1e4
LOG2E = math.log2(math.e)

LANE = 128
ONES_ROWS = 16
VMEM_LIMIT = 60 * 1024 * 1024

NT_DIMS = (((1,), (1,)), ((), ()))
TN_DIMS = (((0,), (0,)), ((), ()))


def _params(*sem):
    return pltpu.CompilerParams(dimension_semantics=sem, vmem_limit_bytes=VMEM_LIMIT)


def _top_k_rows(vals, k):
    n = vals.shape[0]
    rid = lax.broadcasted_iota(jnp.int32, vals.shape, 0)
    sel = jnp.zeros(vals.shape, jnp.bool_)
    g = vals
    for _ in range(k):
        m = jnp.max(g, axis=0, keepdims=True)
        idx = jnp.min(jnp.where(g == m, rid, n), axis=0, keepdims=True)
        pick = rid == idx
        sel = sel | pick
        g = jnp.where(pick, -jnp.inf, g)
    return sel


def _split3(x):
    hi = x.astype(BF16)
    rest = x - hi.astype(F32)
    mid = rest.astype(BF16)
    return hi, mid, (rest - mid.astype(F32)).astype(BF16)


def _pv_with_sums(vt, p):
    ones = jnp.ones((ONES_ROWS, vt.shape[1]), BF16)
    return jnp.dot(jnp.concatenate([vt, ones], axis=0), p.astype(BF16), preferred_element_type=F32)


def _mm_kernel(a_ref, b_ref, *rest, res_coef):
    r = jnp.dot(a_ref[...], b_ref[0], preferred_element_type=F32)
    if res_coef is not None:
        r = ALPHA * rest[0][...] + res_coef * r
    o_ref = rest[-1]
    o_ref[...] = r.astype(o_ref.dtype)


def _matmul(a, b, out_dtype, *, layer=0, res=None, res_coef=None, tm=1024, tn=1024):
    if b.ndim == 2:
        b = b[None]
    m, k = a.shape
    n = b.shape[2]
    tm, tn = min(tm, m), min(tn, n)
    assert m % tm == 0 and n % tn == 0 and (res is None) == (res_coef is None)
    tile = pl.BlockSpec((tm, tn), lambda i, j: (i, j))
    return pl.pallas_call(
        functools.partial(_mm_kernel, res_coef=res_coef),
        out_shape=jax.ShapeDtypeStruct((m, n), out_dtype),
        grid=(m // tm, n // tn),
        in_specs=[pl.BlockSpec((tm, k), lambda i, j: (i, 0)),
                  pl.BlockSpec((1, k, tn), lambda i, j: (layer, 0, j))] + ([] if res is None else [tile]),
        out_specs=tile,
        compiler_params=_params("parallel", "arbitrary"),
        name="matmul",
    )(a, b, *([] if res is None else [res]))


def _mm_nt_kernel(w_ref, x_ref, o_ref, *, act):
    r = lax.dot_general(w_ref[...], x_ref[...], NT_DIMS, preferred_element_type=F32)
    if act == "sigmoid":
        r = jax.nn.sigmoid(r)
    o_ref[...] = r.astype(o_ref.dtype)


def _matmul_nt(wt, x, out_dtype, *, tn=1024, tm=1024, act=None):
    n, k = wt.shape
    m = x.shape[0]
    tn, tm = min(tn, n), min(tm, m)
    assert m % tm == 0 and n % tn == 0
    return pl.pallas_call(
        functools.partial(_mm_nt_kernel, act=act),
        out_shape=jax.ShapeDtypeStruct((n, m), out_dtype),
        grid=(m // tm, n // tn),
        in_specs=[pl.BlockSpec((tn, k), lambda i, j: (j, 0)),
                  pl.BlockSpec((tm, k), lambda i, j: (i, 0))],
        out_specs=pl.BlockSpec((tn, tm), lambda i, j: (j, i)),
        compiler_params=_params("parallel", "arbitrary"),
        name="matmul_nt",
    )(wt, x)


def _mm2_kernel(a1_ref, a2_ref, b1_ref, b2_ref, x_ref, o_ref, *, res_coef):
    r = jnp.dot(a1_ref[...], b1_ref[...], preferred_element_type=F32)
    r = r + jnp.dot(a2_ref[...], b2_ref[...], preferred_element_type=F32)
    o_ref[...] = ALPHA * x_ref[...] + res_coef * r


def _matmul2(a1, a2, b1, b2, res, res_coef, *, tm=1024, tn=1024):
    m, k1 = a1.shape
    k2 = a2.shape[1]
    n = b1.shape[1]
    tm, tn = min(tm, m), min(tn, n)
    assert m % tm == 0 and n % tn == 0
    tile = pl.BlockSpec((tm, tn), lambda i, j: (i, j))
    return pl.pallas_call(
        functools.partial(_mm2_kernel, res_coef=res_coef),
        out_shape=jax.ShapeDtypeStruct((m, n), F32),
        grid=(m // tm, n // tn),
        in_specs=[pl.BlockSpec((tm, k1), lambda i, j: (i, 0)),
                  pl.BlockSpec((tm, k2), lambda i, j: (i, 0)),
                  pl.BlockSpec((k1, tn), lambda i, j: (0, j)),
                  pl.BlockSpec((k2, tn), lambda i, j: (0, j)),
                  tile],
        out_specs=tile,
        compiler_params=_params("parallel", "arbitrary"),
        name="matmul2",
    )(a1, a2, b1, b2, res)


def _ffn_gu_kernel(x_ref, wg_ref, wu_ref, o_ref, wb_ref):
    tn = o_ref.shape[1]

    @pl.when(pl.program_id(1) == 0)
    def _():
        wb_ref[:, 0:tn] = wg_ref[0].astype(BF16)
        wb_ref[:, tn:2 * tn] = wu_ref[0].astype(BF16)

    r = jnp.dot(x_ref[...], wb_ref[...], preferred_element_type=F32)
    o_ref[...] = (jax.nn.silu(r[:, 0:tn]) * r[:, tn:2 * tn]).astype(o_ref.dtype)


def _ffn_gu(x, wg, wu, layer, *, tm=2048, tn=256):
    m, k = x.shape
    n = wg.shape[2]
    tm, tn = min(tm, m), min(tn, n)
    assert m % tm == 0 and n % tn == 0
    w_spec = pl.BlockSpec((1, k, tn), lambda j, i: (layer, 0, j))
    return pl.pallas_call(
        _ffn_gu_kernel,
        out_shape=jax.ShapeDtypeStruct((m, n), BF16),
        grid=(n // tn, m // tm),
        in_specs=[pl.BlockSpec((tm, k), lambda j, i: (i, 0)), w_spec, w_spec],
        out_specs=pl.BlockSpec((tm, tn), lambda j, i: (i, j)),
        scratch_shapes=[pltpu.VMEM((k, 2 * tn), BF16)],
        compiler_params=_params("parallel", "arbitrary"),
        name="ffn_gate_up",
    )(x, wg, wu)


def _ln_kernel(v_ref, g_ref, b_ref, of_ref, *maybe_ob_ref):
    v = v_ref[...]
    mu = jnp.mean(v, axis=-1, keepdims=True)
    d = v - mu
    var = jnp.mean(d * d, axis=-1, keepdims=True)
    y = d * lax.rsqrt(var + LN_EPS) * g_ref[...] + b_ref[...]
    of_ref[...] = y
    for ob_ref in maybe_ob_ref:
        ob_ref[...] = y.astype(BF16)


def _layer_norm(v, g, b, *, want_bf16=True, tm=256):
    m, d = v.shape
    tm = min(tm, m)
    assert m % tm == 0
    row = pl.BlockSpec((tm, d), lambda i: (i, 0))
    vec = pl.BlockSpec((1, d), lambda i: (0, 0))
    out_shape = [jax.ShapeDtypeStruct((m, d), F32)] + [jax.ShapeDtypeStruct((m, d), BF16)] * want_bf16
    out = pl.pallas_call(
        _ln_kernel,
        out_shape=tuple(out_shape),
        grid=(m // tm,),
        in_specs=[row, vec, vec],
        out_specs=tuple([row] * len(out_shape)),
        compiler_params=_params("parallel"),
        name="layernorm",
    )(v, g.reshape(1, d), b.reshape(1, d))
    return (out[0], out[1]) if want_bf16 else (out[0], None)


def _moba_kernel(slope_ref, qt_ref, k_ref, vt_ref, o_ref, kmean_ref, bias_ref, sel_ref,
                 *, nblk, blk, n_top, c2, hp):
    hg = pl.program_id(1)
    qi = pl.program_id(2)
    dh = HEAD_DIM
    kc = lax.broadcasted_iota(jnp.int32, (blk, blk), 0)
    qr = lax.broadcasted_iota(jnp.int32, (blk, blk), 1)
    slope2 = [slope_ref[hg * hp + h] * LOG2E for h in range(hp)]
    cols = [slice(h * dh, (h + 1) * dh) for h in range(hp)]

    @pl.when(qi == 0)
    def _():
        kc2 = lax.broadcasted_iota(jnp.int32, (2 * blk, blk), 0)
        qr2 = lax.broadcasted_iota(jnp.int32, (2 * blk, blk), 1)
        for h in range(hp):
            kf = k_ref[:, cols[h]].astype(F32).reshape(nblk, blk, dh)
            kmean_ref[h] = (jnp.sum(kf, axis=1) / blk).astype(BF16)
            bias_ref[h] = slope2[h] * (qr2 - kc2).astype(F32)

    qt = [qt_ref[cols[h], :] for h in range(hp)]
    gate = [jnp.dot(kmean_ref[h], qt[h], preferred_element_type=F32) for h in range(hp)]
    past = lax.broadcasted_iota(jnp.int32, (nblk, blk), 0) < qi
    for h in range(hp):
        sel = _top_k_rows(jnp.where(past, gate[h], NEG_INF), n_top) & past
        sel_ref[h] = sel.astype(F32)

    def scores(h, k0, nk):
        return (jnp.dot(k_ref[pl.ds(k0, nk), cols[h]], qt[h], preferred_element_type=F32) * c2
                - bias_ref[h, 0:nk, :])

    def pv(h, k0, nk, p):
        return _pv_with_sums(vt_ref[cols[h], pl.ds(k0, nk)], p)

    kq = pl.multiple_of(qi * blk, blk)
    s0 = [jnp.where(kc <= qr, scores(h, kq, blk), NEG_INF) for h in range(hp)]
    m0 = [jnp.max(s0[h], axis=0, keepdims=True) for h in range(hp)]
    acc0 = [pv(h, kq, blk, jnp.exp2(s0[h] - m0[h])) for h in range(hp)]
    init = []
    for h in range(hp):
        init += [m0[h], acc0[h]]

    def body(t, carry):
        j = qi - 1 - 2 * t
        pb = jnp.maximum(j - 1, 0)
        k0 = pl.multiple_of(pb * blk, blk)
        s = [scores(h, k0, 2 * blk) for h in range(hp)]
        valid = pb + lax.broadcasted_iota(jnp.int32, (2, 1), 0) <= j
        m_new, scale, p = [], [], []
        for h in range(hp):
            m_i = carry[2 * h]
            off = slope2[h] * ((qi - pb) * blk).astype(F32)
            rows = jnp.concatenate([sel_ref[h, pl.ds(pb, 1), :], sel_ref[h, pl.ds(pb + 1, 1), :]], axis=0)
            picked = (rows > 0.5) & valid
            blk_max = jnp.max(s[h].reshape(2, blk, blk), axis=1)
            m_pair = jnp.max(jnp.where(picked, blk_max, NEG_INF), axis=0, keepdims=True)
            m_h = jnp.maximum(m_i, m_pair - off)
            shift = jnp.where(picked, m_h + off, -NEG_INF)
            shift = jnp.broadcast_to(shift[:, None, :], (2, blk, blk)).reshape(2 * blk, blk)
            m_new.append(m_h)
            scale.append(jnp.exp2(m_i - m_h))
            p.append(jnp.exp2(s[h] - shift))
        upd = [pv(h, k0, 2 * blk, p[h]) for h in range(hp)]
        out = []
        for h in range(hp):
            out += [m_new[h], scale[h] * carry[2 * h + 1] + upd[h]]
        return tuple(out)

    fin = lax.fori_loop(0, (qi + 1) // 2, body, tuple(init))
    for h in range(hp):
        acc = fin[2 * h + 1]
        o_t = acc[0:dh] / jnp.maximum(acc[dh:dh + 1], 1e-30)
        o_ref[:, cols[h]] = o_t.T.astype(o_ref.dtype)


def _moba(zt, zn, slopes, *, batch, seq, qt_row, vt_row, k_col, hp=4):
    blk = MOBA_BLOCK
    assert seq % blk == 0 and MOBA_HEADS % hp == 0
    assert qt_row % hp == 0 and vt_row % hp == 0 and k_col % hp == 0
    nblk = seq // blk
    n_top = min(MOBA_TOPK, nblk - 1)
    assert n_top > 0
    dh = HEAD_DIM
    kern = functools.partial(_moba_kernel, nblk=nblk, blk=blk, n_top=n_top, c2=dh ** -0.5 * LOG2E, hp=hp)
    return pl.pallas_call(
        kern,
        out_shape=jax.ShapeDtypeStruct((batch * seq, MOBA_HEADS * dh), BF16),
        grid=(batch, MOBA_HEADS // hp, nblk),
        in_specs=[pl.BlockSpec(memory_space=pltpu.SMEM),
                  pl.BlockSpec((hp * dh, blk), lambda b, h, i: (qt_row // hp + h, b * nblk + i)),
                  pl.BlockSpec((seq, hp * dh), lambda b, h, i: (b, k_col // hp + h)),
                  pl.BlockSpec((hp * dh, seq), lambda b, h, i: (vt_row // hp + h, b))],
        out_specs=pl.BlockSpec((blk, hp * dh), lambda b, h, i: (b * nblk + i, h)),
        scratch_shapes=[pltpu.VMEM((hp, nblk, dh), BF16),
                        pltpu.VMEM((hp, 2 * blk, blk), F32),
                        pltpu.VMEM((hp, nblk, blk), F32)],
        compiler_params=_params("parallel", "parallel", "arbitrary"),
        name="moba_attention",
    )(slopes, zt, zn, zt)


def _compress_kernel(x_ref, pos_ref, w1_ref, w2_ref, o_ref, ot_ref, *, nc):
    dh = x_ref.shape[-1]
    half = NSA_CMP_STRIDE
    acc_lo = jnp.zeros((nc, dh), F32)
    acc_hi = jnp.zeros((nc, dh), F32)
    for p in range(half):
        xp = x_ref[pl.ds(p, nc, stride=half), :]
        lo = (xp + pos_ref[0, p:p + 1, :]).astype(BF16)
        hi = (xp + pos_ref[0, half + p:half + p + 1, :]).astype(BF16)
        acc_lo += jnp.dot(lo, w1_ref[0, p * dh:(p + 1) * dh, :], preferred_element_type=F32)
        acc_hi += jnp.dot(hi, w1_ref[0, (half + p) * dh:(half + p + 1) * dh, :], preferred_element_type=F32)
    pre = acc_lo + pltpu.roll(acc_hi, nc - 1, 0)
    hid = jax.nn.gelu(pre)
    out = jnp.dot(hid.astype(BF16), w2_ref[0], preferred_element_type=F32)
    o_ref[0, 0] = out.astype(o_ref.dtype)
    ot_ref[0, 0] = out.T.astype(ot_ref.dtype)


def _nsa_compress(zf, pos, w1, w2, *, batch, seq):
    dh = HEAD_DIM
    g = NSA_KV_GROUPS
    nc = seq // NSA_CMP_STRIDE
    return pl.pallas_call(
        functools.partial(_compress_kernel, nc=nc),
        out_shape=(jax.ShapeDtypeStruct((batch, 2 * g, nc, dh), BF16),
                   jax.ShapeDtypeStruct((batch, 2 * g, dh, nc), BF16)),
        grid=(batch, 2 * g),
        in_specs=[pl.BlockSpec((seq, dh), lambda b, c: (b, c)),
                  pl.BlockSpec((1, NSA_CMP_LEN, dh), lambda b, c: (c // g, 0, 0)),
                  pl.BlockSpec((1, NSA_CMP_LEN * dh, dh), lambda b, c: (c // g, 0, 0)),
                  pl.BlockSpec((1, dh, dh), lambda b, c: (c // g, 0, 0))],
        out_specs=(pl.BlockSpec((1, 1, nc, dh), lambda b, c: (b, c, 0, 0)),
                   pl.BlockSpec((1, 1, dh, nc), lambda b, c: (b, c, 0, 0))),
        compiler_params=_params("parallel", "arbitrary"),
        name="nsa_compress",
    )(zf, pos, w1, w2)


def _nsa_kernel(slope_ref, qt_ref, kc_ref, vct_ref, ks_ref, vst_ref, kw_ref, vwt_ref, gate_ref, o_ref,
                sel_ref, bias_ref, member_ref, ecmp_ref, flag_ref, *, tq, tk, nc, ns, k_top, c2):
    g = pl.program_id(1)
    qi = pl.program_id(2)
    dh = HEAD_DIM
    nr = NSA_GROUP_SIZE
    nl = nr * tq
    bpt = tk // NSA_SEL_LEN
    q0 = qi * tq

    lane = lax.broadcasted_iota(jnp.int32, (1, nl), 1)
    tl = q0 + lane % tq
    head = lane // tq
    slope2 = jnp.zeros((1, nl), F32)
    for r in range(nr):
        slope2 = jnp.where(head == r, slope_ref[g * nr + r] * LOG2E, slope2)

    @pl.when(qi == 0)
    def _():
        kc_i = lax.broadcasted_iota(jnp.int32, (2 * tk, nl), 0)
        bias_ref[...] = slope2 * (lane % tq - kc_i).astype(F32)
        sj = lax.broadcasted_iota(jnp.int32, (ns, nc), 0) * NSA_SEL_LEN
        ci = lax.broadcasted_iota(jnp.int32, (ns, nc), 1) * NSA_CMP_STRIDE
        member_ref[...] = ((ci < sj + NSA_SEL_LEN) & (ci + NSA_CMP_LEN > sj)).astype(BF16)
        ecmp_ref[...] = slope2 * (lax.broadcasted_iota(jnp.int32, (nc, nl), 0) * NSA_CMP_STRIDE
                                  + (NSA_CMP_LEN - 1)).astype(F32)

    qb = qt_ref[...]
    qt = jnp.concatenate([qb[r * dh:(r + 1) * dh, :] for r in range(nr)], axis=1)

    span = NSA_WINDOW + tq
    w0 = pl.multiple_of(jnp.maximum(q0 - NSA_WINDOW, 0), tq)
    raw_cmp = jnp.dot(kc_ref[0, 0], qt, preferred_element_type=F32)
    raw_win = jnp.dot(kw_ref[pl.ds(w0, span), :], qt, preferred_element_type=F32)

    cend = lax.broadcasted_iota(jnp.int32, (nc, 1), 0) * NSA_CMP_STRIDE + (NSA_CMP_LEN - 1)
    s = (raw_cmp * c2 + ecmp_ref[...]) - slope2 * tl.astype(F32)
    vis = cend <= tl
    s = jnp.where(vis, s, NEG_INF)
    e = jnp.where(vis, jnp.exp2(s - jnp.max(s, axis=0, keepdims=True)), 0.0)
    p_cmp = e * (1.0 / jnp.maximum(jnp.sum(e, axis=0, keepdims=True), 1e-30))
    o_cmp = jnp.dot(vct_ref[0, 0], p_cmp.astype(BF16), preferred_element_type=F32)

    p_sum = p_cmp[:, 0:tq]
    for r in range(1, nr):
        p_sum = p_sum + p_cmp[:, r * tq:(r + 1) * tq]
    member = member_ref[...]
    imp = sum(jnp.dot(member, part, preferred_element_type=F32) for part in _split3(p_sum))

    rel = (w0 + lax.broadcasted_iota(jnp.int32, (span, 1), 0)) - tl
    s = (raw_win * c2 - bias_ref[0:span, :]) + slope2 * (w0 - q0).astype(F32)
    s = jnp.where((rel <= 0) & (rel > -NSA_WINDOW), s, NEG_INF)
    acc = _pv_with_sums(vwt_ref[:, pl.ds(w0, span)], jnp.exp2(s - jnp.max(s, axis=0, keepdims=True)))
    o_win = acc[0:dh] / jnp.maximum(acc[dh:dh + 1], 1e-30)

    own = (q0 + lax.broadcasted_iota(jnp.int32, (1, tq), 1)) // NSA_SEL_LEN
    sid = lax.broadcasted_iota(jnp.int32, (ns, tq), 0)
    forced = (sid == 0) | (sid == own) | (sid == own - 1)
    imp = jnp.where(forced, imp + SEL_FORCE, jnp.where(sid > own, -1.0, imp))
    sel_ref[...] = _top_k_rows(imp, k_top).astype(F32)

    def sel_rows(b0, nb):
        rows = sel_ref[pl.ds(pl.multiple_of(b0, bpt), nb), :]
        return jnp.concatenate([rows] * nr, axis=1)

    def scores(k0, nk):
        return jnp.dot(ks_ref[pl.ds(k0, nk), :], qt, preferred_element_type=F32) * c2 - bias_ref[0:nk, :]

    def blockwise(x):
        nb = x.shape[0]
        return jnp.broadcast_to(x[:, None, :], (nb, NSA_SEL_LEN, nl)).reshape(nb * NSA_SEL_LEN, nl)

    jd = q0 // tk
    kd = pl.multiple_of(jd * tk, tk)
    kpos = kd + lax.broadcasted_iota(jnp.int32, (tk, 1), 0)
    off = slope2 * (q0 - kd).astype(F32)
    ok = (blockwise(sel_rows(jd * bpt, bpt)) > 0.5) & (kpos <= tl)
    s = jnp.where(ok, scores(kd, tk), NEG_INF)
    m0 = jnp.max(s, axis=0, keepdims=True) - off
    acc0 = _pv_with_sums(vst_ref[:, pl.ds(kd, tk)], jnp.exp2(s - (m0 + off)))

    def sweep(i, live, carry):
        m_i, acc = carry
        k0 = pl.multiple_of(i * (2 * tk), 2 * tk)
        s = scores(k0, 2 * tk)
        off = slope2 * (q0 - k0).astype(F32)
        rows = jnp.concatenate([live] * nr, axis=1) > 0.5
        blk_max = jnp.max(s.reshape(2 * bpt, NSA_SEL_LEN, nl), axis=1)
        m_tile = jnp.max(jnp.where(rows, blk_max, NEG_INF), axis=0, keepdims=True)
        m_new = jnp.maximum(m_i, m_tile - off)
        shift = jnp.where(rows, m_new + off, -NEG_INF)
        upd = _pv_with_sums(vst_ref[:, pl.ds(k0, 2 * tk)], jnp.exp2(s - blockwise(shift)))
        return m_new, jnp.exp2(m_i - m_new) * acc + upd

    vis_sel = jnp.where(sid < jd * bpt, sel_ref[...], 0.0)
    pair_any = jnp.max(jnp.max(vis_sel.reshape(ns // (2 * bpt), 2 * bpt, tq), axis=1), axis=1, keepdims=True)
    for i in range(ns // (2 * bpt)):
        flag_ref[i] = (pair_any[i, 0] > 0.5).astype(jnp.int32)

    def body(i, carry):
        b0 = pl.multiple_of(i * (2 * bpt), 2 * bpt)
        bid = b0 + lax.broadcasted_iota(jnp.int32, (2 * bpt, 1), 0)
        live = jnp.where(bid < jd * bpt, sel_ref[pl.ds(b0, 2 * bpt), :], 0.0)
        return lax.cond(flag_ref[i] > 0, lambda c: sweep(i, live, c), lambda c: c, carry)

    _, acc = lax.fori_loop(0, (jd + 1) // 2, body, (m0, acc0))
    o_slc = acc[0:dh] / jnp.maximum(acc[dh:dh + 1], 1e-30)

    gt = gate_ref[0]
    for r in range(nr):
        sl = slice(r * tq, (r + 1) * tq)
        o = (gt[3 * r:3 * r + 1] * o_cmp[:, sl] + gt[3 * r + 1:3 * r + 2] * o_slc[:, sl]
             + gt[3 * r + 2:3 * r + 3] * o_win[:, sl])
        o_ref[:, r * dh:(r + 1) * dh] = o.T.astype(o_ref.dtype)


def _nsa(zt, zn, cmp_n, cmp_t, gates_t, slopes, *, batch, seq, qt_row, vst_row, vwt_row, ks_col, kw_col,
         tq=256, tk=512):
    dh = HEAD_DIM
    g = NSA_KV_GROUPS
    nr = NSA_GROUP_SIZE
    tk = min(tk, seq)
    assert seq % (2 * tk) == 0 and tk % tq == 0 and NSA_WINDOW % tq == 0
    assert NSA_WINDOW + tq <= min(seq, 2 * tk)
    nq = seq // tq
    nc = seq // NSA_CMP_STRIDE
    ns = seq // NSA_SEL_LEN
    k_top = min(NSA_N_SEL, ns)
    kern = functools.partial(_nsa_kernel, tq=tq, tk=tk, nc=nc, ns=ns, k_top=k_top, c2=dh ** -0.5 * LOG2E)
    key_blk = lambda col: pl.BlockSpec((seq, dh), lambda b, gg, i: (b, col + gg))
    val_blk = lambda row: pl.BlockSpec((dh, seq), lambda b, gg, i: (row + gg, b))
    return pl.pallas_call(
        kern,
        out_shape=jax.ShapeDtypeStruct((batch * seq, NSA_HEADS * dh), BF16),
        grid=(batch, g, nq),
        in_specs=[pl.BlockSpec(memory_space=pltpu.SMEM),
                  pl.BlockSpec((nr * dh, tq), lambda b, gg, i: (qt_row // nr + gg, b * nq + i)),
                  pl.BlockSpec((1, 1, nc, dh), lambda b, gg, i: (b, gg, 0, 0)),
                  pl.BlockSpec((1, 1, dh, nc), lambda b, gg, i: (b, g + gg, 0, 0)),
                  key_blk(ks_col), val_blk(vst_row), key_blk(kw_col), val_blk(vwt_row),
                  pl.BlockSpec((1, gates_t.shape[1], tq), lambda b, gg, i: (gg, 0, b * nq + i))],
        out_specs=pl.BlockSpec((tq, nr * dh), lambda b, gg, i: (b * nq + i, gg)),
        scratch_shapes=[pltpu.VMEM((ns, tq), F32),
                        pltpu.VMEM((2 * tk, nr * tq), F32),
                        pltpu.VMEM((ns, nc), BF16),
                        pltpu.VMEM((nc, nr * tq), F32),
                        pltpu.SMEM((ns // (2 * tk // NSA_SEL_LEN),), jnp.int32)],
        compiler_params=_params("parallel", "parallel", "arbitrary"),
        name="nsa_attention",
    )(slopes, zt, cmp_n, cmp_t, zn, zt, zn, zt, gates_t)


def _conv_kernel(gb_ref, gc_ref, h_ref, gcp_ref, hp_ref, w_ref, o_ref, *, tiles_per_seq):
    i = pl.program_id(0)
    u = gc_ref[...] * h_ref[...]
    prev = gcp_ref[...] * hp_ref[...]
    prev = jnp.where(i % tiles_per_seq == 0, 0.0, prev)
    r = lax.broadcasted_iota(jnp.int32, u.shape, 0)
    u1 = jnp.where(r == 0, prev[7:8], pltpu.roll(u, 1, 0))
    u2 = jnp.where(r == 0, prev[6:7], jnp.where(r == 1, prev[7:8], pltpu.roll(u, 2, 0)))
    w = w_ref[...]
    y = gb_ref[...] * (w[0:1] * u2 + w[1:2] * u1 + w[2:3] * u)
    o_ref[...] = y.astype(o_ref.dtype)


def _short_conv(z, w, *, seq, gb_col, gc_col, h_col, tt=512, tc=512):
    m = z.shape[0]
    c = w.shape[1]
    tt = min(tt, seq)
    assert seq % tt == 0 and c % tc == 0 and tt % 8 == 0
    sub = tt // 8
    cur = lambda col: pl.BlockSpec((tt, tc), lambda i, j: (i, col + j))
    prv = lambda col: pl.BlockSpec((8, tc), lambda i, j: (jnp.maximum(i * sub - 1, 0), col + j))
    return pl.pallas_call(
        functools.partial(_conv_kernel, tiles_per_seq=seq // tt),
        out_shape=jax.ShapeDtypeStruct((m, c), BF16),
        grid=(m // tt, c // tc),
        in_specs=[cur(gb_col), cur(gc_col), cur(h_col), prv(gc_col), prv(h_col),
                  pl.BlockSpec((3, tc), lambda i, j: (0, j))],
        out_specs=pl.BlockSpec((tt, tc), lambda i, j: (i, j)),
        compiler_params=_params("parallel", "arbitrary"),
        name="short_conv",
    )(z, z, z, z, z, w)


def _gate_decay_kernel(x_ref, w1_ref, w2_ref, b_ref, o_ref):
    za = jnp.dot(x_ref[...], w1_ref[...], preferred_element_type=F32)
    pre = jnp.dot(za.astype(BF16), w2_ref[...], preferred_element_type=F32) + b_ref[...]
    ls = -(jnp.maximum(-pre, 0.0) + jnp.log1p(jnp.exp(-jnp.abs(pre))))
    o_ref[...] = ls / GLA_GATE_TAU


def _gate_decay(x, w1, w2, b, *, tm=1024):
    m, k = x.shape
    r = w1.shape[1]
    n = w2.shape[1]
    tm = min(tm, m)
    return pl.pallas_call(
        _gate_decay_kernel,
        out_shape=jax.ShapeDtypeStruct((m, n), F32),
        grid=(m // tm,),
        in_specs=[pl.BlockSpec((tm, k), lambda i: (i, 0)),
                  pl.BlockSpec((k, r), lambda i: (0, 0)),
                  pl.BlockSpec((r, n), lambda i: (0, 0)),
                  pl.BlockSpec((1, n), lambda i: (0, 0))],
        out_specs=pl.BlockSpec((tm, n), lambda i: (i, 0)),
        compiler_params=_params("parallel"),
        name="gla_gate_decay",
    )(x, w1, w2, b.reshape(1, n))


def _gla_kernel(q_ref, k_ref, v_ref, g_ref, la_ref, ng_ref, o_ref, st_ref, *, tc):
    L = GLA_CHUNK
    dk, dv = GLA_DK, GLA_DV
    hp = q_ref.shape[-1] // dk

    @pl.when(pl.program_id(2) == 0)
    def _():
        st_ref[...] = jnp.zeros_like(st_ref)

    nch = tc // L
    ri = lax.broadcasted_iota(jnp.int32, (tc, tc), 0)
    ci = lax.broadcasted_iota(jnp.int32, (tc, tc), 1)
    causal = (ci <= ri) & (ri // L == ci // L)
    tri = causal.astype(BF16)
    ng = ng_ref[...]

    b = sum(jnp.dot(tri, part, preferred_element_type=F32) for part in _split3(la_ref[...]))
    b3 = b.reshape(nch, L, hp * dk)
    b_last = b3[:, L - 1:L, :]
    q_t = ((q_ref[...] * dk ** -0.5) * jnp.exp(b)).astype(BF16)
    k_raw = k_ref[...]
    k_t = (k_raw * jnp.exp(-b)).astype(BF16)
    k_d = (k_raw.reshape(nch, L, hp * dk) * jnp.exp(b_last - b3)).astype(BF16)
    dec = jnp.exp(b_last)
    v = v_ref[...].astype(BF16)

    ks = [slice(h * dk, (h + 1) * dk) for h in range(hp)]
    vs = [slice(h * dv, (h + 1) * dv) for h in range(hp)]
    att = [lax.dot_general(q_t[:, ks[h]], k_t[:, ks[h]], NT_DIMS, preferred_element_type=F32) for h in range(hp)]
    u_t = [[lax.dot_general(v[c * L:(c + 1) * L, vs[h]], k_d[c][:, ks[h]], TN_DIMS, preferred_element_type=F32)
            for c in range(nch)] for h in range(hp)]
    o = [jnp.dot(jnp.where(causal, att[h], 0.0).astype(BF16), v[:, vs[h]], preferred_element_type=F32)
         for h in range(hp)]

    states = []
    for h in range(hp):
        st = st_ref[h]
        per_chunk = []
        for c in range(nch):
            per_chunk.append(st.astype(BF16))
            st = st * dec[c][:, ks[h]] + u_t[h][c]
        st_ref[h] = st
        states.append(per_chunk)
    inter = [[lax.dot_general(q_t[c * L:(c + 1) * L, ks[h]], states[h][c], NT_DIMS, preferred_element_type=F32)
              for c in range(nch)] for h in range(hp)]
    for h in range(hp):
        oh = o[h] + jnp.concatenate(inter[h], axis=0)
        oh = oh * lax.rsqrt(jnp.mean(oh * oh, axis=-1, keepdims=True) + NORM_EPS) * ng
        o_ref[:, vs[h]] = (oh * jax.nn.silu(g_ref[:, vs[h]])).astype(o_ref.dtype)


def _gla(z, la, norm_g, *, batch, seq, q_col, k_col, v_col, g_col, tc=256, hp=2):
    dk, dv = GLA_DK, GLA_DV
    tc = min(tc, seq)
    assert seq % tc == 0 and tc % GLA_CHUNK == 0 and GLA_HEADS % hp == 0
    nt = seq // tc
    qk = lambda col: pl.BlockSpec((tc, hp * dk), lambda b, h, i: (b * nt + i, col + h))
    vg = lambda col: pl.BlockSpec((tc, hp * dv), lambda b, h, i: (b * nt + i, col + h))
    return pl.pallas_call(
        functools.partial(_gla_kernel, tc=tc),
        out_shape=jax.ShapeDtypeStruct((batch * seq, GLA_HEADS * dv), BF16),
        grid=(batch, GLA_HEADS // hp, nt),
        in_specs=[qk(q_col), qk(k_col), vg(v_col), vg(g_col), qk(0),
                  pl.BlockSpec((1, dv), lambda b, h, i: (0, 0))],
        out_specs=pl.BlockSpec((tc, hp * dv), lambda b, h, i: (b * nt + i, h)),
        scratch_shapes=[pltpu.VMEM((hp, dv, dk), F32)],
        compiler_params=_params("parallel", "parallel", "arbitrary"),
        name="gla",
    )(z, z, z, z, la, norm_g.reshape(1, dv))


def _pad_cols(w, n):
    return jnp.pad(w, ((0, 0), (0, n - w.shape[1])))


def _ffn_sublayer(xf, xb, wg, wu, wd_b, layer, ln_g, ln_b, want_bf16=True):
    gu = _ffn_gu(xb, wg, wu, layer)
    v = _matmul(gu, wd_b, F32, layer=layer, res=xf, res_coef=0.5, tm=512, tn=512)
    return _layer_norm(v, ln_g, ln_b, want_bf16=want_bf16)


def _alibi_slopes(n):
    return jnp.exp2(-8.0 * jnp.arange(1, n + 1, dtype=F32) / n)


def _attn_sublayer(xf, xb, w_in, w_out, pos_k, w1_k, w2_k, pos_v, w1_v, w2_v, ln_g, ln_b, *, batch, seq):
    dh = HEAD_DIM
    nq = NSA_HEADS * dh
    kv = NSA_KV_GROUPS * dh
    nm = MOBA_HEADS * dh
    n_gate = NSA_HEADS * 3
    c_kc, c_ks, c_vs, c_kw, c_vw, c_gate = (nq + i * kv for i in (0, 2, 3, 4, 5, 6))
    c_mq = c_gate + n_gate
    c_mk, c_mv = c_mq + nm, c_mq + 2 * nm
    col = lambda a, n: w_in[:, a:a + n]
    w_n = jnp.concatenate([col(c_ks, kv), col(c_kw, kv), col(c_mk, nm)], axis=1).astype(BF16)
    w_t = jnp.concatenate([col(0, nq), col(c_vs, kv), col(c_vw, kv), col(c_mq, nm), col(c_mv, nm)],
                          axis=1).T.astype(BF16)
    w_f = col(c_kc, 2 * kv).astype(BF16)
    w_g = _pad_cols(col(c_gate, n_gate), LANE).T.astype(BF16)
    zn = _matmul(xb, w_n, BF16)
    zt = _matmul_nt(w_t, xb, BF16)
    zf = _matmul(xb, w_f, F32)
    gates_t = _matmul_nt(w_g, xb, F32, act="sigmoid")[:n_gate]
    gates_t = jnp.pad(gates_t.reshape(NSA_KV_GROUPS, 3 * NSA_GROUP_SIZE, batch * seq), ((0, 0), (0, 4), (0, 0)))

    slopes = _alibi_slopes(N_ATTN_HEADS)
    cmp_n, cmp_t = _nsa_compress(zf, jnp.stack([pos_k, pos_v]), jnp.stack([w1_k, w1_v]).astype(BF16),
                                 jnp.stack([w2_k, w2_v]).astype(BF16), batch=batch, seq=seq)
    u = nq // dh
    o_nsa = _nsa(zt, zn, cmp_n, cmp_t, gates_t, slopes[0::2], batch=batch, seq=seq,
                 qt_row=0, vst_row=u, vwt_row=u + 4, ks_col=0, kw_col=4)
    o_moba = _moba(zt, zn, slopes[1::2], batch=batch, seq=seq, qt_row=u + 8, vt_row=u + 24, k_col=8)
    w_o = w_out.astype(BF16)
    v = _matmul2(o_nsa, o_moba, w_o[:nq], w_o[nq:], xf, 1.0)
    return _layer_norm(v, ln_g, ln_b)


def _mix_sublayer(xf, xb, w_in, w_out, conv_w, w_a2, b_a, norm_g, ln_g, ln_b, *, batch, seq):
    cc = CONV_CHANNELS
    hk = GLA_HEADS * GLA_DK
    hv = GLA_HEADS * GLA_DV
    c_za = 3 * cc + 2 * hk + 2 * hv
    z = _matmul(xb, w_in[:, :c_za].astype(BF16), F32)
    w_za = _pad_cols(w_in[:, c_za:], LANE).astype(BF16)
    w_a2p = jnp.pad(w_a2, ((0, LANE - w_a2.shape[0]), (0, 0))).astype(BF16)
    la = _gate_decay(xb, w_za, w_a2p, b_a)
    y_conv = _short_conv(z, conv_w, seq=seq, gb_col=0, gc_col=cc // 512, h_col=2 * cc // 512)
    hp = 2
    y_gla = _gla(z, la, norm_g, batch=batch, seq=seq, q_col=3 * cc // (hp * GLA_DK),
                 k_col=(3 * cc + hk) // (hp * GLA_DK), v_col=(3 * cc + 2 * hk) // (hp * GLA_DV),
                 g_col=(3 * cc + 2 * hk + hv) // (hp * GLA_DV), hp=hp)
    w_o = w_out.astype(BF16)
    v = _matmul2(y_conv, y_gla, w_o[:cc], w_o[cc:], xf, 1.0)
    return _layer_norm(v, ln_g, ln_b)


def kernel(x, ln_g, ln_b, ffn_pre_wg, ffn_pre_wu, ffn_pre_wd, ffn_post_wg, ffn_post_wu, ffn_post_wd,
           att_w_in, att_w_out, nsa_pos_k, nsa_w1_k, nsa_w2_k, nsa_pos_v, nsa_w1_v, nsa_w2_v,
           mix_w_in, mix_w_out, conv_w, gla_w_a2, gla_b_a, gla_norm_g):
    batch, seq, d = x.shape
    xf = x.reshape(batch * seq, d)
    xb = xf.astype(BF16)
    pre_wd, post_wd = ffn_pre_wd.astype(BF16), ffn_post_wd.astype(BF16)
    for layer in range(DEPTH):
        xf, xb = _ffn_sublayer(xf, xb, ffn_pre_wg, ffn_pre_wu, pre_wd, layer, ln_g[layer, 0], ln_b[layer, 0])
        i = layer // 2
        if layer % 2 == 0:
            xf, xb = _attn_sublayer(xf, xb, att_w_in[i], att_w_out[i], nsa_pos_k[i], nsa_w1_k[i], nsa_w2_k[i],
                                    nsa_pos_v[i], nsa_w1_v[i], nsa_w2_v[i], ln_g[layer, 1], ln_b[layer, 1],
                                    batch=batch, seq=seq)
        else:
            xf, xb = _mix_sublayer(xf, xb, mix_w_in[i], mix_w_out[i], conv_w[i], gla_w_a2[i], gla_b_a[i],
                                   gla_norm_g[i], ln_g[layer, 1], ln_b[layer, 1], batch=batch, seq=seq)
        xf, xb = _ffn_sublayer(xf, xb, ffn_post_wg, ffn_post_wu, post_wd, layer, ln_g[layer, 2], ln_b[layer, 2],
                               want_bf16=layer + 1 < DEPTH)
    return xf.reshape(batch, seq, d)
```

```python
import functools
import math

import jax
import jax.numpy as jnp
from jax import lax
from jax.experimental import pallas as pl
from jax.experimental.pallas import tpu as pltpu

F32 = jnp.float32
BF16 = jnp.bfloat16

D_MODEL = 4096
DEPTH = 2
HEAD_DIM = 128
NSA_HEADS = 16
NSA_KV_GROUPS = 4
NSA_GROUP_SIZE = 4
NSA_CMP_STRIDE = 16
NSA_CMP_LEN = 32
NSA_SEL_LEN = 64
NSA_N_SEL = 16
NSA_WINDOW = 512
MOBA_HEADS = 16
MOBA_BLOCK = 256
MOBA_TOPK = 3
N_ATTN_HEADS = NSA_HEADS + MOBA_HEADS
CONV_CHANNELS = 2048
GLA_HEADS = 16
GLA_DK = 64
GLA_DV = 128
GLA_GATE_RANK = 16
GLA_GATE_TAU = 16.0
GLA_CHUNK = 64
D_FF = 11008
ALPHA = (2 * DEPTH) ** 0.25
LN_EPS = 1e-5
NORM_EPS = 1e-6
NEG_INF = -1e30
SEL_FORCE = 1e4
LOG2E = math.log2(math.e)

LANE = 128
ONES_ROWS = 16
VMEM_LIMIT = 60 * 1024 * 1024

NT_DIMS = (((1,), (1,)), ((), ()))
TN_DIMS = (((0,), (0,)), ((), ()))


def _params(*sem):
    return pltpu.CompilerParams(dimension_semantics=sem, vmem_limit_bytes=VMEM_LIMIT)


def _top_k_rows(vals, k):
    n = vals.shape[0]
    rid = lax.broadcasted_iota(jnp.int32, vals.shape, 0)
    sel = jnp.zeros(vals.shape, jnp.bool_)
    g = vals
    for _ in range(k):
        m = jnp.max(g, axis=0, keepdims=True)
        idx = jnp.min(jnp.where(g == m, rid, n), axis=0, keepdims=True)
        pick = rid == idx
        sel = sel | pick
        g = jnp.where(pick, -jnp.inf, g)
    return sel


def _split3(x):
    hi = x.astype(BF16)
    rest = x - hi.astype(F32)
    mid = rest.astype(BF16)
    return hi, mid, (rest - mid.astype(F32)).astype(BF16)


def _pv_with_sums(vt, p):
    ones = jnp.ones((ONES_ROWS, vt.shape[1]), BF16)
    return jnp.dot(jnp.concatenate([vt, ones], axis=0), p.astype(BF16), preferred_element_type=F32)


def _mm_kernel(a_ref, b_ref, *rest, res_coef):
    r = jnp.dot(a_ref[...], b_ref[0], preferred_element_type=F32)
    if res_coef is not None:
        r = ALPHA * rest[0][...] + res_coef * r
    o_ref = rest[-1]
    o_ref[...] = r.astype(o_ref.dtype)


def _matmul(a, b, out_dtype, *, layer=0, res=None, res_coef=None, tm=1024, tn=1024):
    if b.ndim == 2:
        b = b[None]
    m, k = a.shape
    n = b.shape[2]
    tm, tn = min(tm, m), min(tn, n)
    assert m % tm == 0 and n % tn == 0 and (res is None) == (res_coef is None)
    tile = pl.BlockSpec((tm, tn), lambda i, j: (i, j))
    return pl.pallas_call(
        functools.partial(_mm_kernel, res_coef=res_coef),
        out_shape=jax.ShapeDtypeStruct((m, n), out_dtype),
        grid=(m // tm, n // tn),
        in_specs=[pl.BlockSpec((tm, k), lambda i, j: (i, 0)),
                  pl.BlockSpec((1, k, tn), lambda i, j: (layer, 0, j))] + ([] if res is None else [tile]),
        out_specs=tile,
        compiler_params=_params("parallel", "arbitrary"),
        name="matmul",
    )(a, b, *([] if res is None else [res]))


def _mm_nt_kernel(w_ref, x_ref, o_ref, *, act):
    r = lax.dot_general(w_ref[...], x_ref[...], NT_DIMS, preferred_element_type=F32)
    if act == "sigmoid":
        r = jax.nn.sigmoid(r)
    o_ref[...] = r.astype(o_ref.dtype)


def _matmul_nt(wt, x, out_dtype, *, tn=1024, tm=1024, act=None):
    n, k = wt.shape
    m = x.shape[0]
    tn, tm = min(tn, n), min(tm, m)
    assert m % tm == 0 and n % tn == 0
    return pl.pallas_call(
        functools.partial(_mm_nt_kernel, act=act),
        out_shape=jax.ShapeDtypeStruct((n, m), out_dtype),
        grid=(m // tm, n // tn),
        in_specs=[pl.BlockSpec((tn, k), lambda i, j: (j, 0)),
                  pl.BlockSpec((tm, k), lambda i, j: (i, 0))],
        out_specs=pl.BlockSpec((tn, tm), lambda i, j: (j, i)),
        compiler_params=_params("parallel", "arbitrary"),
        name="matmul_nt",
    )(wt, x)


def _mm2_kernel(a1_ref, a2_ref, b1_ref, b2_ref, x_ref, o_ref, *, res_coef):
    r = jnp.dot(a1_ref[...], b1_ref[...], preferred_element_type=F32)
    r = r + jnp.dot(a2_ref[...], b2_ref[...], preferred_element_type=F32)
    o_ref[...] = ALPHA * x_ref[...] + res_coef * r


def _matmul2(a1, a2, b1, b2, res, res_coef, *, tm=1024, tn=1024):
    m, k1 = a1.shape
    k2 = a2.shape[1]
    n = b1.shape[1]
    tm, tn = min(tm, m), min(tn, n)
    assert m % tm == 0 and n % tn == 0
    tile = pl.BlockSpec((tm, tn), lambda i, j: (i, j))
    return pl.pallas_call(
        functools.partial(_mm2_kernel, res_coef=res_coef),
        out_shape=jax.ShapeDtypeStruct((m, n), F32),
        grid=(m // tm, n // tn),
        in_specs=[pl.BlockSpec((tm, k1), lambda i, j: (i, 0)),
                  pl.BlockSpec((tm, k2), lambda i, j: (i, 0)),
                  pl.BlockSpec((k1, tn), lambda i, j: (0, j)),
                  pl.BlockSpec((k2, tn), lambda i, j: (0, j)),
                  tile],
        out_specs=tile,
        compiler_params=_params("parallel", "arbitrary"),
        name="matmul2",
    )(a1, a2, b1, b2, res)


def _ffn_gu_kernel(x_ref, wg_ref, wu_ref, o_ref, wb_ref):
    tn = o_ref.shape[1]

    @pl.when(pl.program_id(1) == 0)
    def _():
        wb_ref[:, 0:tn] = wg_ref[0].astype(BF16)
        wb_ref[:, tn:2 * tn] = wu_ref[0].astype(BF16)

    r = jnp.dot(x_ref[...], wb_ref[...], preferred_element_type=F32)
    o_ref[...] = (jax.nn.silu(r[:, 0:tn]) * r[:, tn:2 * tn]).astype(o_ref.dtype)


def _ffn_gu(x, wg, wu, layer, *, tm=2048, tn=256):
    m, k = x.shape
    n = wg.shape[2]
    tm, tn = min(tm, m), min(tn, n)
    assert m % tm == 0 and n % tn == 0
    w_spec = pl.BlockSpec((1, k, tn), lambda j, i: (layer, 0, j))
    return pl.pallas_call(
        _ffn_gu_kernel,
        out_shape=jax.ShapeDtypeStruct((m, n), BF16),
        grid=(n // tn, m // tm),
        in_specs=[pl.BlockSpec((tm, k), lambda j, i: (i, 0)), w_spec, w_spec],
        out_specs=pl.BlockSpec((tm, tn), lambda j, i: (i, j)),
        scratch_shapes=[pltpu.VMEM((k, 2 * tn), BF16)],
        compiler_params=_params("parallel", "arbitrary"),
        name="ffn_gate_up",
    )(x, wg, wu)


def _ln_kernel(v_ref, g_ref, b_ref, of_ref, *maybe_ob_ref):
    v = v_ref[...]
    mu = jnp.mean(v, axis=-1, keepdims=True)
    d = v - mu
    var = jnp.mean(d * d, axis=-1, keepdims=True)
    y = d * lax.rsqrt(var + LN_EPS) * g_ref[...] + b_ref[...]
    of_ref[...] = y
    for ob_ref in maybe_ob_ref:
        ob_ref[...] = y.astype(BF16)


def _layer_norm(v, g, b, *, want_bf16=True, tm=256):
    m, d = v.shape
    tm = min(tm, m)
    assert m % tm == 0
    row = pl.BlockSpec((tm, d), lambda i: (i, 0))
    vec = pl.BlockSpec((1, d), lambda i: (0, 0))
    out_shape = [jax.ShapeDtypeStruct((m, d), F32)] + [jax.ShapeDtypeStruct((m, d), BF16)] * want_bf16
    out = pl.pallas_call(
        _ln_kernel,
        out_shape=tuple(out_shape),
        grid=(m // tm,),
        in_specs=[row, vec, vec],
        out_specs=tuple([row] * len(out_shape)),
        compiler_params=_params("parallel"),
        name="layernorm",
    )(v, g.reshape(1, d), b.reshape(1, d))
    return (out[0], out[1]) if want_bf16 else (out[0], None)


def _moba_kernel(slope_ref, qt_ref, k_ref, vt_ref, o_ref, kmean_ref, bias_ref, sel_ref, s0_ref, s1_ref,
                 *, nblk, blk, n_top, c2, hp):
    hg = pl.program_id(1)
    qi = pl.program_id(2)
    dh = HEAD_DIM
    kc = lax.broadcasted_iota(jnp.int32, (blk, blk), 0)
    qr = lax.broadcasted_iota(jnp.int32, (blk, blk), 1)
    slope2 = [slope_ref[hg * hp + h] * LOG2E for h in range(hp)]
    cols = [slice(h * dh, (h + 1) * dh) for h in range(hp)]

    @pl.when(qi == 0)
    def _():
        kc2 = lax.broadcasted_iota(jnp.int32, (2 * blk, blk), 0)
        qr2 = lax.broadcasted_iota(jnp.int32, (2 * blk, blk), 1)
        for h in range(hp):
            kf = k_ref[:, cols[h]].astype(F32).reshape(nblk, blk, dh)
            kmean_ref[h] = (jnp.sum(kf, axis=1) / blk).astype(BF16)
            bias_ref[h] = slope2[h] * (qr2 - kc2).astype(F32)

    qt = [qt_ref[cols[h], :] for h in range(hp)]
    gate = [jnp.dot(kmean_ref[h], qt[h], preferred_element_type=F32) for h in range(hp)]
    past = lax.broadcasted_iota(jnp.int32, (nblk, blk), 0) < qi
    for h in range(hp):
        sel = _top_k_rows(jnp.where(past, gate[h], NEG_INF), n_top) & past
        sel_ref[h] = sel.astype(F32)

    def scores(h, k0, nk):
        return (jnp.dot(k_ref[pl.ds(k0, nk), cols[h]], qt[h], preferred_element_type=F32) * c2
                - bias_ref[h, 0:nk, :])

    def pv(h, k0, nk, p):
        return _pv_with_sums(vt_ref[cols[h], pl.ds(k0, nk)], p)

    kq = pl.multiple_of(qi * blk, blk)
    s0 = [jnp.where(kc <= qr, scores(h, kq, blk), NEG_INF) for h in range(hp)]
    m0 = [jnp.max(s0[h], axis=0, keepdims=True) for h in range(hp)]
    acc0 = [pv(h, kq, blk, jnp.exp2(s0[h] - m0[h])) for h in range(hp)]
    init = []
    for h in range(hp):
        init += [m0[h], acc0[h]]

    def pair_pos(t):
        j = qi - 1 - 2 * t
        pb = jnp.maximum(j - 1, 0)
        return j, pb, pl.multiple_of(pb * blk, blk)

    def fill(slot_ref, t):
        k0 = pair_pos(t)[2]
        for h in range(hp):
            slot_ref[h] = scores(h, k0, 2 * blk)

    def consume(slot_ref, t, carry):
        j, pb, k0 = pair_pos(t)
        s = [slot_ref[h] for h in range(hp)]
        valid = pb + lax.broadcasted_iota(jnp.int32, (2, 1), 0) <= j
        m_new, scale, p = [], [], []
        for h in range(hp):
            m_i = carry[2 * h]
            off = slope2[h] * ((qi - pb) * blk).astype(F32)
            rows = jnp.concatenate([sel_ref[h, pl.ds(pb, 1), :], sel_ref[h, pl.ds(pb + 1, 1), :]], axis=0)
            picked = (rows > 0.5) & valid
            blk_max = jnp.max(s[h].reshape(2, blk, blk), axis=1)
            m_pair = jnp.max(jnp.where(picked, blk_max, NEG_INF), axis=0, keepdims=True)
            m_h = jnp.maximum(m_i, m_pair - off)
            shift = jnp.where(picked, m_h + off, -NEG_INF)
            shift = jnp.broadcast_to(shift[:, None, :], (2, blk, blk)).reshape(2 * blk, blk)
            m_new.append(m_h)
            scale.append(jnp.exp2(m_i - m_h))
            p.append(jnp.exp2(s[h] - shift))
        upd = [pv(h, k0, 2 * blk, p[h]) for h in range(hp)]
        out = []
        for h in range(hp):
            out += [m_new[h], scale[h] * carry[2 * h + 1] + upd[h]]
        return tuple(out)

    def body(u, carry):
        fill(s1_ref, 2 * u + 1)
        carry = consume(s0_ref, 2 * u, carry)
        fill(s0_ref, 2 * u + 2)
        return consume(s1_ref, 2 * u + 1, carry)

    fill(s0_ref, 0)
    fin = lax.fori_loop(0, ((qi + 1) // 2 + 1) // 2, body, tuple(init))
    for h in range(hp):
        acc = fin[2 * h + 1]
        o_t = acc[0:dh] / jnp.maximum(acc[dh:dh + 1], 1e-30)
        o_ref[:, cols[h]] = o_t.T.astype(o_ref.dtype)


def _moba(zt, zn, slopes, *, batch, seq, qt_row, vt_row, k_col, hp=4):
    blk = MOBA_BLOCK
    assert seq % blk == 0 and MOBA_HEADS % hp == 0
    assert qt_row % hp == 0 and vt_row % hp == 0 and k_col % hp == 0
    nblk = seq // blk
    n_top = min(MOBA_TOPK, nblk - 1)
    assert n_top > 0
    dh = HEAD_DIM
    kern = functools.partial(_moba_kernel, nblk=nblk, blk=blk, n_top=n_top, c2=dh ** -0.5 * LOG2E, hp=hp)
    return pl.pallas_call(
        kern,
        out_shape=jax.ShapeDtypeStruct((batch * seq, MOBA_HEADS * dh), BF16),
        grid=(batch, MOBA_HEADS // hp, nblk),
        in_specs=[pl.BlockSpec(memory_space=pltpu.SMEM),
                  pl.BlockSpec((hp * dh, blk), lambda b, h, i: (qt_row // hp + h, b * nblk + i)),
                  pl.BlockSpec((seq, hp * dh), lambda b, h, i: (b, k_col // hp + h)),
                  pl.BlockSpec((hp * dh, seq), lambda b, h, i: (vt_row // hp + h, b))],
        out_specs=pl.BlockSpec((blk, hp * dh), lambda b, h, i: (b * nblk + i, h)),
        scratch_shapes=[pltpu.VMEM((hp, nblk, dh), BF16),
                        pltpu.VMEM((hp, 2 * blk, blk), F32),
                        pltpu.VMEM((hp, nblk, blk), F32),
                        pltpu.VMEM((hp, 2 * blk, blk), F32),
                        pltpu.VMEM((hp, 2 * blk, blk), F32)],
        compiler_params=_params("parallel", "parallel", "arbitrary"),
        name="moba_attention",
    )(slopes, zt, zn, zt)


def _compress_kernel(x_ref, pos_ref, w1_ref, w2_ref, o_ref, ot_ref, *, nc):
    dh = x_ref.shape[-1]
    half = NSA_CMP_STRIDE
    acc_lo = jnp.zeros((nc, dh), F32)
    acc_hi = jnp.zeros((nc, dh), F32)
    for p in range(half):
        xp = x_ref[pl.ds(p, nc, stride=half), :]
        lo = (xp + pos_ref[0, p:p + 1, :]).astype(BF16)
        hi = (xp + pos_ref[0, half + p:half + p + 1, :]).astype(BF16)
        acc_lo += jnp.dot(lo, w1_ref[0, p * dh:(p + 1) * dh, :], preferred_element_type=F32)
        acc_hi += jnp.dot(hi, w1_ref[0, (half + p) * dh:(half + p + 1) * dh, :], preferred_element_type=F32)
    pre = acc_lo + pltpu.roll(acc_hi, nc - 1, 0)
    hid = jax.nn.gelu(pre)
    out = jnp.dot(hid.astype(BF16), w2_ref[0], preferred_element_type=F32)
    o_ref[0, 0] = out.astype(o_ref.dtype)
    ot_ref[0, 0] = out.T.astype(ot_ref.dtype)


def _nsa_compress(zf, pos, w1, w2, *, batch, seq):
    dh = HEAD_DIM
    g = NSA_KV_GROUPS
    nc = seq // NSA_CMP_STRIDE
    return pl.pallas_call(
        functools.partial(_compress_kernel, nc=nc),
        out_shape=(jax.ShapeDtypeStruct((batch, 2 * g, nc, dh), BF16),
                   jax.ShapeDtypeStruct((batch, 2 * g, dh, nc), BF16)),
        grid=(batch, 2 * g),
        in_specs=[pl.BlockSpec((seq, dh), lambda b, c: (b, c)),
                  pl.BlockSpec((1, NSA_CMP_LEN, dh), lambda b, c: (c // g, 0, 0)),
                  pl.BlockSpec((1, NSA_CMP_LEN * dh, dh), lambda b, c: (c // g, 0, 0)),
                  pl.BlockSpec((1, dh, dh), lambda b, c: (c // g, 0, 0))],
        out_specs=(pl.BlockSpec((1, 1, nc, dh), lambda b, c: (b, c, 0, 0)),
                   pl.BlockSpec((1, 1, dh, nc), lambda b, c: (b, c, 0, 0))),
        compiler_params=_params("parallel", "arbitrary"),
        name="nsa_compress",
    )(zf, pos, w1, w2)


def _nsa_kernel(slope_ref, qt_ref, kc_ref, vct_ref, ks_ref, vst_ref, kw_ref, vwt_ref, gate_ref, o_ref,
                sel_ref, bias_ref, member_ref, ecmp_ref, flag_ref, *, tq, tk, nc, ns, k_top, c2):
    g = pl.program_id(1)
    qi = pl.program_id(2)
    dh = HEAD_DIM
    nr = NSA_GROUP_SIZE
    nl = nr * tq
    bpt = tk // NSA_SEL_LEN
    q0 = qi * tq

    lane = lax.broadcasted_iota(jnp.int32, (1, nl), 1)
    tl = q0 + lane % tq
    head = lane // tq
    slope2 = jnp.zeros((1, nl), F32)
    for r in range(nr):
        slope2 = jnp.where(head == r, slope_ref[g * nr + r] * LOG2E, slope2)

    @pl.when(qi == 0)
    def _():
        kc_i = lax.broadcasted_iota(jnp.int32, (2 * tk, nl), 0)
        bias_ref[...] = slope2 * (lane % tq - kc_i).astype(F32)
        sj = lax.broadcasted_iota(jnp.int32, (ns, nc), 0) * NSA_SEL_LEN
        ci = lax.broadcasted_iota(jnp.int32, (ns, nc), 1) * NSA_CMP_STRIDE
        member_ref[...] = ((ci < sj + NSA_SEL_LEN) & (ci + NSA_CMP_LEN > sj)).astype(BF16)
        ecmp_ref[...] = slope2 * (lax.broadcasted_iota(jnp.int32, (nc, nl), 0) * NSA_CMP_STRIDE
                                  + (NSA_CMP_LEN - 1)).astype(F32)

    qb = qt_ref[...]
    qt = jnp.concatenate([qb[r * dh:(r + 1) * dh, :] for r in range(nr)], axis=1)

    span = NSA_WINDOW + tq
    w0 = pl.multiple_of(jnp.maximum(q0 - NSA_WINDOW, 0), tq)
    raw_cmp = jnp.dot(kc_ref[0, 0], qt, preferred_element_type=F32)
    raw_win = jnp.dot(kw_ref[pl.ds(w0, span), :], qt, preferred_element_type=F32)

    cend = lax.broadcasted_iota(jnp.int32, (nc, 1), 0) * NSA_CMP_STRIDE + (NSA_CMP_LEN - 1)
    s = (raw_cmp * c2 + ecmp_ref[...]) - slope2 * tl.astype(F32)
    vis = cend <= tl
    s = jnp.where(vis, s, NEG_INF)
    e = jnp.where(vis, jnp.exp2(s - jnp.max(s, axis=0, keepdims=True)), 0.0)
    p_cmp = e * (1.0 / jnp.maximum(jnp.sum(e, axis=0, keepdims=True), 1e-30))
    o_cmp = jnp.dot(vct_ref[0, 0], p_cmp.astype(BF16), preferred_element_type=F32)

    p_sum = p_cmp[:, 0:tq]
    for r in range(1, nr):
        p_sum = p_sum + p_cmp[:, r * tq:(r + 1) * tq]
    member = member_ref[...]
    imp = sum(jnp.dot(member, part, preferred_element_type=F32) for part in _split3(p_sum))

    rel = (w0 + lax.broadcasted_iota(jnp.int32, (span, 1), 0)) - tl
    s = (raw_win * c2 - bias_ref[0:span, :]) + slope2 * (w0 - q0).astype(F32)
    s = jnp.where((rel <= 0) & (rel > -NSA_WINDOW), s, NEG_INF)
    acc = _pv_with_sums(vwt_ref[:, pl.ds(w0, span)], jnp.exp2(s - jnp.max(s, axis=0, keepdims=True)))
    o_win = acc[0:dh] / jnp.maximum(acc[dh:dh + 1], 1e-30)

    own = (q0 + lax.broadcasted_iota(jnp.int32, (1, tq), 1)) // NSA_SEL_LEN
    sid = lax.broadcasted_iota(jnp.int32, (ns, tq), 0)
    forced = (sid == 0) | (sid == own) | (sid == own - 1)
    imp = jnp.where(forced, imp + SEL_FORCE, jnp.where(sid > own, -1.0, imp))
    sel_ref[...] = _top_k_rows(imp, k_top).astype(F32)

    def sel_rows(b0, nb):
        rows = sel_ref[pl.ds(pl.multiple_of(b0, bpt), nb), :]
        return jnp.concatenate([rows] * nr, axis=1)

    def scores(k0, nk):
        return jnp.dot(ks_ref[pl.ds(k0, nk), :], qt, preferred_element_type=F32) * c2 - bias_ref[0:nk, :]

    def blockwise(x):
        nb = x.shape[0]
        return jnp.broadcast_to(x[:, None, :], (nb, NSA_SEL_LEN, nl)).reshape(nb * NSA_SEL_LEN, nl)

    jd = q0 // tk
    kd = pl.multiple_of(jd * tk, tk)
    kpos = kd + lax.broadcasted_iota(jnp.int32, (tk, 1), 0)
    off = slope2 * (q0 - kd).astype(F32)
    ok = (blockwise(sel_rows(jd * bpt, bpt)) > 0.5) & (kpos <= tl)
    s = jnp.where(ok, scores(kd, tk), NEG_INF)
    m0 = jnp.max(s, axis=0, keepdims=True) - off
    acc0 = _pv_with_sums(vst_ref[:, pl.ds(kd, tk)], jnp.exp2(s - (m0 + off)))

    def sweep(i, live, carry):
        m_i, acc = carry
        k0 = pl.multiple_of(i * (2 * tk), 2 * tk)
        s = scores(k0, 2 * tk)
        off = slope2 * (q0 - k0).astype(F32)
        rows = jnp.concatenate([live] * nr, axis=1) > 0.5
        blk_max = jnp.max(s.reshape(2 * bpt, NSA_SEL_LEN, nl), axis=1)
        m_tile = jnp.max(jnp.where(rows, blk_max, NEG_INF), axis=0, keepdims=True)
        m_new = jnp.maximum(m_i, m_tile - off)
        shift = jnp.where(rows, m_new + off, -NEG_INF)
        upd = _pv_with_sums(vst_ref[:, pl.ds(k0, 2 * tk)], jnp.exp2(s - blockwise(shift)))
        return m_new, jnp.exp2(m_i - m_new) * acc + upd

    vis_sel = jnp.where(sid < jd * bpt, sel_ref[...], 0.0)
    pair_any = jnp.max(jnp.max(vis_sel.reshape(ns // (2 * bpt), 2 * bpt, tq), axis=1), axis=1, keepdims=True)
    for i in range(ns // (2 * bpt)):
        flag_ref[i] = (pair_any[i, 0] > 0.5).astype(jnp.int32)

    def body(i, carry):
        b0 = pl.multiple_of(i * (2 * bpt), 2 * bpt)
        bid = b0 + lax.broadcasted_iota(jnp.int32, (2 * bpt, 1), 0)
        live = jnp.where(bid < jd * bpt, sel_ref[pl.ds(b0, 2 * bpt), :], 0.0)
        return lax.cond(flag_ref[i] > 0, lambda c: sweep(i, live, c), lambda c: c, carry)

    _, acc = lax.fori_loop(0, (jd + 1) // 2, body, (m0, acc0))
    o_slc = acc[0:dh] / jnp.maximum(acc[dh:dh + 1], 1e-30)

    gt = gate_ref[0]
    for r in range(nr):
        sl = slice(r * tq, (r + 1) * tq)
        o = (gt[3 * r:3 * r + 1] * o_cmp[:, sl] + gt[3 * r + 1:3 * r + 2] * o_slc[:, sl]
             + gt[3 * r + 2:3 * r + 3] * o_win[:, sl])
        o_ref[:, r * dh:(r + 1) * dh] = o.T.astype(o_ref.dtype)


def _nsa(zt, zn, cmp_n, cmp_t, gates_t, slopes, *, batch, seq, qt_row, vst_row, vwt_row, ks_col, kw_col,
         tq=256, tk=512):
    dh = HEAD_DIM
    g = NSA_KV_GROUPS
    nr = NSA_GROUP_SIZE
    tk = min(tk, seq)
    assert seq % (2 * tk) == 0 and tk % tq == 0 and NSA_WINDOW % tq == 0
    assert NSA_WINDOW + tq <= min(seq, 2 * tk)
    nq = seq // tq
    nc = seq // NSA_CMP_STRIDE
    ns = seq // NSA_SEL_LEN
    k_top = min(NSA_N_SEL, ns)
    kern = functools.partial(_nsa_kernel, tq=tq, tk=tk, nc=nc, ns=ns, k_top=k_top, c2=dh ** -0.5 * LOG2E)
    key_blk = lambda col: pl.BlockSpec((seq, dh), lambda b, gg, i: (b, col + gg))
    val_blk = lambda row: pl.BlockSpec((dh, seq), lambda b, gg, i: (row + gg, b))
    return pl.pallas_call(
        kern,
        out_shape=jax.ShapeDtypeStruct((batch * seq, NSA_HEADS * dh), BF16),
        grid=(batch, g, nq),
        in_specs=[pl.BlockSpec(memory_space=pltpu.SMEM),
                  pl.BlockSpec((nr * dh, tq), lambda b, gg, i: (qt_row // nr + gg, b * nq + i)),
                  pl.BlockSpec((1, 1, nc, dh), lambda b, gg, i: (b, gg, 0, 0)),
                  pl.BlockSpec((1, 1, dh, nc), lambda b, gg, i: (b, g + gg, 0, 0)),
                  key_blk(ks_col), val_blk(vst_row), key_blk(kw_col), val_blk(vwt_row),
                  pl.BlockSpec((1, gates_t.shape[1], tq), lambda b, gg, i: (gg, 0, b * nq + i))],
        out_specs=pl.BlockSpec((tq, nr * dh), lambda b, gg, i: (b * nq + i, gg)),
        scratch_shapes=[pltpu.VMEM((ns, tq), F32),
                        pltpu.VMEM((2 * tk, nr * tq), F32),
                        pltpu.VMEM((ns, nc), BF16),
                        pltpu.VMEM((nc, nr * tq), F32),
                        pltpu.SMEM((ns // (2 * tk // NSA_SEL_LEN),), jnp.int32)],
        compiler_params=_params("parallel", "parallel", "arbitrary"),
        name="nsa_attention",
    )(slopes, zt, cmp_n, cmp_t, zn, zt, zn, zt, gates_t)


def _conv_kernel(gb_ref, gc_ref, h_ref, gcp_ref, hp_ref, w_ref, o_ref, *, tiles_per_seq):
    i = pl.program_id(0)
    u = gc_ref[...] * h_ref[...]
    prev = gcp_ref[...] * hp_ref[...]
    prev = jnp.where(i % tiles_per_seq == 0, 0.0, prev)
    r = lax.broadcasted_iota(jnp.int32, u.shape, 0)
    u1 = jnp.where(r == 0, prev[7:8], pltpu.roll(u, 1, 0))
    u2 = jnp.where(r == 0, prev[6:7], jnp.where(r == 1, prev[7:8], pltpu.roll(u, 2, 0)))
    w = w_ref[...]
    y = gb_ref[...] * (w[0:1] * u2 + w[1:2] * u1 + w[2:3] * u)
    o_ref[...] = y.astype(o_ref.dtype)


def _short_conv(z, w, *, seq, gb_col, gc_col, h_col, tt=512, tc=512):
    m = z.shape[0]
    c = w.shape[1]
    tt = min(tt, seq)
    assert seq % tt == 0 and c % tc == 0 and tt % 8 == 0
    sub = tt // 8
    cur = lambda col: pl.BlockSpec((tt, tc), lambda i, j: (i, col + j))
    prv = lambda col: pl.BlockSpec((8, tc), lambda i, j: (jnp.maximum(i * sub - 1, 0), col + j))
    return pl.pallas_call(
        functools.partial(_conv_kernel, tiles_per_seq=seq // tt),
        out_shape=jax.ShapeDtypeStruct((m, c), BF16),
        grid=(m // tt, c // tc),
        in_specs=[cur(gb_col), cur(gc_col), cur(h_col), prv(gc_col), prv(h_col),
                  pl.BlockSpec((3, tc), lambda i, j: (0, j))],
        out_specs=pl.BlockSpec((tt, tc), lambda i, j: (i, j)),
        compiler_params=_params("parallel", "arbitrary"),
        name="short_conv",
    )(z, z, z, z, z, w)


def _gate_decay_kernel(x_ref, w1_ref, w2_ref, b_ref, o_ref):
    za = jnp.dot(x_ref[...], w1_ref[...], preferred_element_type=F32)
    pre = jnp.dot(za.astype(BF16), w2_ref[...], preferred_element_type=F32) + b_ref[...]
    ls = -(jnp.maximum(-pre, 0.0) + jnp.log1p(jnp.exp(-jnp.abs(pre))))
    o_ref[...] = ls / GLA_GATE_TAU


def _gate_decay(x, w1, w2, b, *, tm=1024):
    m, k = x.shape
    r = w1.shape[1]
    n = w2.shape[1]
    tm = min(tm, m)
    return pl.pallas_call(
        _gate_decay_kernel,
        out_shape=jax.ShapeDtypeStruct((m, n), F32),
        grid=(m // tm,),
        in_specs=[pl.BlockSpec((tm, k), lambda i: (i, 0)),
                  pl.BlockSpec((k, r), lambda i: (0, 0)),
                  pl.BlockSpec((r, n), lambda i: (0, 0)),
                  pl.BlockSpec((1, n), lambda i: (0, 0))],
        out_specs=pl.BlockSpec((tm, n), lambda i: (i, 0)),
        compiler_params=_params("parallel"),
        name="gla_gate_decay",
    )(x, w1, w2, b.reshape(1, n))


def _gla_kernel(q_ref, k_ref, v_ref, g_ref, la_ref, ng_ref, o_ref, st_ref, *, tc):
    L = GLA_CHUNK
    dk, dv = GLA_DK, GLA_DV
    hp = q_ref.shape[-1] // dk

    @pl.when(pl.program_id(2) == 0)
    def _():
        st_ref[...] = jnp.zeros_like(st_ref)

    nch = tc // L
    ri = lax.broadcasted_iota(jnp.int32, (tc, tc), 0)
    ci = lax.broadcasted_iota(jnp.int32, (tc, tc), 1)
    causal = (ci <= ri) & (ri // L == ci // L)
    tri = causal.astype(BF16)
    ng = ng_ref[...]

    b = sum(jnp.dot(tri, part, preferred_element_type=F32) for part in _split3(la_ref[...]))
    b3 = b.reshape(nch, L, hp * dk)
    b_last = b3[:, L - 1:L, :]
    q_t = ((q_ref[...] * dk ** -0.5) * jnp.exp(b)).astype(BF16)
    k_raw = k_ref[...]
    k_t = (k_raw * jnp.exp(-b)).astype(BF16)
    k_d = (k_raw.reshape(nch, L, hp * dk) * jnp.exp(b_last - b3)).astype(BF16)
    dec = jnp.exp(b_last)
    v = v_ref[...].astype(BF16)

    ks = [slice(h * dk, (h + 1) * dk) for h in range(hp)]
    vs = [slice(h * dv, (h + 1) * dv) for h in range(hp)]
    att = [lax.dot_general(q_t[:, ks[h]], k_t[:, ks[h]], NT_DIMS, preferred_element_type=F32) for h in range(hp)]
    u_t = [[lax.dot_general(v[c * L:(c + 1) * L, vs[h]], k_d[c][:, ks[h]], TN_DIMS, preferred_element_type=F32)
            for c in range(nch)] for h in range(hp)]
    o = [jnp.dot(jnp.where(causal, att[h], 0.0).astype(BF16), v[:, vs[h]], preferred_element_type=F32)
         for h in range(hp)]

    states = []
    for h in range(hp):
        st = st_ref[h]
        per_chunk = []
        for c in range(nch):
            per_chunk.append(st.astype(BF16))
            st = st * dec[c][:, ks[h]] + u_t[h][c]
        st_ref[h] = st
        states.append(per_chunk)
    inter = [[lax.dot_general(q_t[c * L:(c + 1) * L, ks[h]], states[h][c], NT_DIMS, preferred_element_type=F32)
              for c in range(nch)] for h in range(hp)]
    for h in range(hp):
        oh = o[h] + jnp.concatenate(inter[h], axis=0)
        oh = oh * lax.rsqrt(jnp.mean(oh * oh, axis=-1, keepdims=True) + NORM_EPS) * ng
        o_ref[:, vs[h]] = (oh * jax.nn.silu(g_ref[:, vs[h]])).astype(o_ref.dtype)


def _gla(z, la, norm_g, *, batch, seq, q_col, k_col, v_col, g_col, tc=256, hp=2):
    dk, dv = GLA_DK, GLA_DV
    tc = min(tc, seq)
    assert seq % tc == 0 and tc % GLA_CHUNK == 0 and GLA_HEADS % hp == 0
    nt = seq // tc
    qk = lambda col: pl.BlockSpec((tc, hp * dk), lambda b, h, i: (b * nt + i, col + h))
    vg = lambda col: pl.BlockSpec((tc, hp * dv), lambda b, h, i: (b * nt + i, col + h))
    return pl.pallas_call(
        functools.partial(_gla_kernel, tc=tc),
        out_shape=jax.ShapeDtypeStruct((batch * seq, GLA_HEADS * dv), BF16),
        grid=(batch, GLA_HEADS // hp, nt),
        in_specs=[qk(q_col), qk(k_col), vg(v_col), vg(g_col), qk(0),
                  pl.BlockSpec((1, dv), lambda b, h, i: (0, 0))],
        out_specs=pl.BlockSpec((tc, hp * dv), lambda b, h, i: (b * nt + i, h)),
        scratch_shapes=[pltpu.VMEM((hp, dv, dk), F32)],
        compiler_params=_params("parallel", "parallel", "arbitrary"),
        name="gla",
    )(z, z, z, z, la, norm_g.reshape(1, dv))


def _pad_cols(w, n):
    return jnp.pad(w, ((0, 0), (0, n - w.shape[1])))


def _ffn_sublayer(xf, xb, wg, wu, wd_b, layer, ln_g, ln_b, want_bf16=True):
    gu = _ffn_gu(xb, wg, wu, layer)
    v = _matmul(gu, wd_b, F32, layer=layer, res=xf, res_coef=0.5, tm=512, tn=512)
    return _layer_norm(v, ln_g, ln_b, want_bf16=want_bf16)


def _alibi_slopes(n):
    return jnp.exp2(-8.0 * jnp.arange(1, n + 1, dtype=F32) / n)


def _attn_sublayer(xf, xb, w_in, w_out, pos_k, w1_k, w2_k, pos_v, w1_v, w2_v, ln_g, ln_b, *, batch, seq):
    dh = HEAD_DIM
    nq = NSA_HEADS * dh
    kv = NSA_KV_GROUPS * dh
    nm = MOBA_HEADS * dh
    n_gate = NSA_HEADS * 3
    c_kc, c_ks, c_vs, c_kw, c_vw, c_gate = (nq + i * kv for i in (0, 2, 3, 4, 5, 6))
    c_mq = c_gate + n_gate
    c_mk, c_mv = c_mq + nm, c_mq + 2 * nm
    col = lambda a, n: w_in[:, a:a + n]
    w_n = jnp.concatenate([col(c_ks, kv), col(c_kw, kv), col(c_mk, nm)], axis=1).astype(BF16)
    w_t = jnp.concatenate([col(0, nq), col(c_vs, kv), col(c_vw, kv), col(c_mq, nm), col(c_mv, nm)],
                          axis=1).T.astype(BF16)
    w_f = col(c_kc, 2 * kv).astype(BF16)
    w_g = _pad_cols(col(c_gate, n_gate), LANE).T.astype(BF16)
    zn = _matmul(xb, w_n, BF16)
    zt = _matmul_nt(w_t, xb, BF16)
    zf = _matmul(xb, w_f, F32)
    gates_t = _matmul_nt(w_g, xb, F32, act="sigmoid")[:n_gate]
    gates_t = jnp.pad(gates_t.reshape(NSA_KV_GROUPS, 3 * NSA_GROUP_SIZE, batch * seq), ((0, 0), (0, 4), (0, 0)))

    slopes = _alibi_slopes(N_ATTN_HEADS)
    cmp_n, cmp_t = _nsa_compress(zf, jnp.stack([pos_k, pos_v]), jnp.stack([w1_k, w1_v]).astype(BF16),
                                 jnp.stack([w2_k, w2_v]).astype(BF16), batch=batch, seq=seq)
    u = nq // dh
    o_nsa = _nsa(zt, zn, cmp_n, cmp_t, gates_t, slopes[0::2], batch=batch, seq=seq,
                 qt_row=0, vst_row=u, vwt_row=u + 4, ks_col=0, kw_col=4)
    o_moba = _moba(zt, zn, slopes[1::2], batch=batch, seq=seq, qt_row=u + 8, vt_row=u + 24, k_col=8)
    w_o = w_out.astype(BF16)
    v = _matmul2(o_nsa, o_moba, w_o[:nq], w_o[nq:], xf, 1.0)
    return _layer_norm(v, ln_g, ln_b)


def _mix_sublayer(xf, xb, w_in, w_out, conv_w, w_a2, b_a, norm_g, ln_g, ln_b, *, batch, seq):
    cc = CONV_CHANNELS
    hk = GLA_HEADS * GLA_DK
    hv = GLA_HEADS * GLA_DV
    c_za = 3 * cc + 2 * hk + 2 * hv
    z = _matmul(xb, w_in[:, :c_za].astype(BF16), F32)
    w_za = _pad_cols(w_in[:, c_za:], LANE).astype(BF16)
    w_a2p = jnp.pad(w_a2, ((0, LANE - w_a2.shape[0]), (0, 0))).astype(BF16)
    la = _gate_decay(xb, w_za, w_a2p, b_a)
    y_conv = _short_conv(z, conv_w, seq=seq, gb_col=0, gc_col=cc // 512, h_col=2 * cc // 512)
    hp = 2
    y_gla = _gla(z, la, norm_g, batch=batch, seq=seq, q_col=3 * cc // (hp * GLA_DK),
                 k_col=(3 * cc + hk) // (hp * GLA_DK), v_col=(3 * cc + 2 * hk) // (hp * GLA_DV),
                 g_col=(3 * cc + 2 * hk + hv) // (hp * GLA_DV), hp=hp)
    w_o = w_out.astype(BF16)
    v = _matmul2(y_conv, y_gla, w_o[:cc], w_o[cc:], xf, 1.0)
    return _layer_norm(v, ln_g, ln_b)


def kernel(x, ln_g, ln_b, ffn_pre_wg, ffn_pre_wu, ffn_pre_wd, ffn_post_wg, ffn_post_wu, ffn_post_wd,
           att_w_in, att_w_out, nsa_pos_k, nsa_w1_k, nsa_w2_k, nsa_pos_v, nsa_w1_v, nsa_w2_v,
           mix_w_in, mix_w_out, conv_w, gla_w_a2, gla_b_a, gla_norm_g):
    batch, seq, d = x.shape
    xf = x.reshape(batch * seq, d)
    xb = xf.astype(BF16)
    pre_wd, post_wd = ffn_pre_wd.astype(BF16), ffn_post_wd.astype(BF16)
    for layer in range(DEPTH):
        xf, xb = _ffn_sublayer(xf, xb, ffn_pre_wg, ffn_pre_wu, pre_wd, layer, ln_g[layer, 0], ln_b[layer, 0])
        i = layer // 2
        if layer % 2 == 0:
            xf, xb = _attn_sublayer(xf, xb, att_w_in[i], att_w_out[i], nsa_pos_k[i], nsa_w1_k[i], nsa_w2_k[i],
                                    nsa_pos_v[i], nsa_w1_v[i], nsa_w2_v[i], ln_g[layer, 1], ln_b[layer, 1],
                                    batch=batch, seq=seq)
        else:
            xf, xb = _mix_sublayer(xf, xb, mix_w_in[i], mix_w_out[i], conv_w[i], gla_w_a2[i], gla_b_a[i],
                                   gla_norm_g[i], ln_g[layer, 1], ln_b[layer, 1], batch=batch, seq=seq)
        xf, xb = _ffn_sublayer(xf, xb, ffn_post_wg, ffn_post_wu, post_wd, layer, ln_g[layer, 2], ln_b[layer, 2],
                               want_bf16=layer + 1 < DEPTH)
    return xf.reshape(batch, seq, d)
```

```python
import functools
import math

import jax
import jax.numpy as jnp
from jax import lax
from jax.experimental import pallas as pl
from jax.experimental.pallas import tpu as pltpu

F32 = jnp.float32
BF16 = jnp.bfloat16

D_MODEL = 4096
DEPTH = 2
HEAD_DIM = 128
NSA_HEADS = 16
NSA_KV_GROUPS = 4
NSA_GROUP_SIZE = 4
NSA_CMP_STRIDE = 16
NSA_CMP_LEN = 32
NSA_SEL_LEN = 64
NSA_N_SEL = 16
NSA_WINDOW = 512
MOBA_HEADS = 16
MOBA_BLOCK = 256
MOBA_TOPK = 3
N_ATTN_HEADS = NSA_HEADS + MOBA_HEADS
CONV_CHANNELS = 2048
GLA_HEADS = 16
GLA_DK = 64
GLA_DV = 128
GLA_GATE_RANK = 16
GLA_GATE_TAU = 16.0
GLA_CHUNK = 64
D_FF = 11008
ALPHA = (2 * DEPTH) ** 0.25
LN_EPS = 1e-5
NORM_EPS = 1e-6
NEG_INF = -1e30
SEL_FORCE = 1e4
LOG2E = math.log2(math.e)

LANE = 128
ONES_ROWS = 16
VMEM_LIMIT = 60 * 1024 * 1024

NT_DIMS = (((1,), (1,)), ((), ()))
TN_DIMS = (((0,), (0,)), ((), ()))


def _params(*sem):
    return pltpu.CompilerParams(dimension_semantics=sem, vmem_limit_bytes=VMEM_LIMIT)


def _top_k_rows(vals, k):
    n = vals.shape[0]
    rid = lax.broadcasted_iota(jnp.int32, vals.shape, 0)
    sel = jnp.zeros(vals.shape, jnp.bool_)
    g = vals
    for _ in range(k):
        m = jnp.max(g, axis=0, keepdims=True)
        idx = jnp.min(jnp.where(g == m, rid, n), axis=0, keepdims=True)
        pick = rid == idx
        sel = sel | pick
        g = jnp.where(pick, -jnp.inf, g)
    return sel


def _split3(x):
    hi = x.astype(BF16)
    rest = x - hi.astype(F32)
    mid = rest.astype(BF16)
    return hi, mid, (rest - mid.astype(F32)).astype(BF16)


def _pv_with_sums(vt, p):
    ones = jnp.ones((ONES_ROWS, vt.shape[1]), BF16)
    return jnp.dot(jnp.concatenate([vt, ones], axis=0), p.astype(BF16), preferred_element_type=F32)


def _mm_kernel(a_ref, b_ref, *rest, res_coef):
    r = jnp.dot(a_ref[...], b_ref[0], preferred_element_type=F32)
    if res_coef is not None:
        r = ALPHA * rest[0][...] + res_coef * r
    o_ref = rest[-1]
    o_ref[...] = r.astype(o_ref.dtype)


def _matmul(a, b, out_dtype, *, layer=0, res=None, res_coef=None, tm=1024, tn=1024):
    if b.ndim == 2:
        b = b[None]
    m, k = a.shape
    n = b.shape[2]
    tm, tn = min(tm, m), min(tn, n)
    assert m % tm == 0 and n % tn == 0 and (res is None) == (res_coef is None)
    tile = pl.BlockSpec((tm, tn), lambda i, j: (i, j))
    return pl.pallas_call(
        functools.partial(_mm_kernel, res_coef=res_coef),
        out_shape=jax.ShapeDtypeStruct((m, n), out_dtype),
        grid=(m // tm, n // tn),
        in_specs=[pl.BlockSpec((tm, k), lambda i, j: (i, 0)),
                  pl.BlockSpec((1, k, tn), lambda i, j: (layer, 0, j))] + ([] if res is None else [tile]),
        out_specs=tile,
        compiler_params=_params("parallel", "arbitrary"),
        name="matmul",
    )(a, b, *([] if res is None else [res]))


def _mm_nt_kernel(w_ref, x_ref, o_ref, *, act):
    r = lax.dot_general(w_ref[...], x_ref[...], NT_DIMS, preferred_element_type=F32)
    if act == "sigmoid":
        r = jax.nn.sigmoid(r)
    o_ref[...] = r.astype(o_ref.dtype)


def _matmul_nt(wt, x, out_dtype, *, tn=1024, tm=1024, act=None):
    n, k = wt.shape
    m = x.shape[0]
    tn, tm = min(tn, n), min(tm, m)
    assert m % tm == 0 and n % tn == 0
    return pl.pallas_call(
        functools.partial(_mm_nt_kernel, act=act),
        out_shape=jax.ShapeDtypeStruct((n, m), out_dtype),
        grid=(m // tm, n // tn),
        in_specs=[pl.BlockSpec((tn, k), lambda i, j: (j, 0)),
                  pl.BlockSpec((tm, k), lambda i, j: (i, 0))],
        out_specs=pl.BlockSpec((tn, tm), lambda i, j: (j, i)),
        compiler_params=_params("parallel", "arbitrary"),
        name="matmul_nt",
    )(wt, x)


def _mm2_kernel(a1_ref, a2_ref, b1_ref, b2_ref, x_ref, o_ref, *, res_coef):
    r = jnp.dot(a1_ref[...], b1_ref[...], preferred_element_type=F32)
    r = r + jnp.dot(a2_ref[...], b2_ref[...], preferred_element_type=F32)
    o_ref[...] = ALPHA * x_ref[...] + res_coef * r


def _matmul2(a1, a2, b1, b2, res, res_coef, *, tm=1024, tn=1024):
    m, k1 = a1.shape
    k2 = a2.shape[1]
    n = b1.shape[1]
    tm, tn = min(tm, m), min(tn, n)
    assert m % tm == 0 and n % tn == 0
    tile = pl.BlockSpec((tm, tn), lambda i, j: (i, j))
    return pl.pallas_call(
        functools.partial(_mm2_kernel, res_coef=res_coef),
        out_shape=jax.ShapeDtypeStruct((m, n), F32),
        grid=(m // tm, n // tn),
        in_specs=[pl.BlockSpec((tm, k1), lambda i, j: (i, 0)),
                  pl.BlockSpec((tm, k2), lambda i, j: (i, 0)),
                  pl.BlockSpec((k1, tn), lambda i, j: (0, j)),
                  pl.BlockSpec((k2, tn), lambda i, j: (0, j)),
                  tile],
        out_specs=tile,
        compiler_params=_params("parallel", "arbitrary"),
        name="matmul2",
    )(a1, a2, b1, b2, res)


def _ffn_gu_kernel(x_ref, wg_ref, wu_ref, o_ref, wb_ref):
    tn = o_ref.shape[1]

    @pl.when(pl.program_id(1) == 0)
    def _():
        wb_ref[:, 0:tn] = wg_ref[0].astype(BF16)
        wb_ref[:, tn:2 * tn] = wu_ref[0].astype(BF16)

    r = jnp.dot(x_ref[...], wb_ref[...], preferred_element_type=F32)
    o_ref[...] = (jax.nn.silu(r[:, 0:tn]) * r[:, tn:2 * tn]).astype(o_ref.dtype)


def _ffn_gu(x, wg, wu, layer, *, tm=2048, tn=256):
    m, k = x.shape
    n = wg.shape[2]
    tm, tn = min(tm, m), min(tn, n)
    assert m % tm == 0 and n % tn == 0
    w_spec = pl.BlockSpec((1, k, tn), lambda j, i: (layer, 0, j))
    return pl.pallas_call(
        _ffn_gu_kernel,
        out_shape=jax.ShapeDtypeStruct((m, n), BF16),
        grid=(n // tn, m // tm),
        in_specs=[pl.BlockSpec((tm, k), lambda j, i: (i, 0)), w_spec, w_spec],
        out_specs=pl.BlockSpec((tm, tn), lambda j, i: (i, j)),
        scratch_shapes=[pltpu.VMEM((k, 2 * tn), BF16)],
        compiler_params=_params("parallel", "arbitrary"),
        name="ffn_gate_up",
    )(x, wg, wu)


def _ln_kernel(v_ref, g_ref, b_ref, of_ref, *maybe_ob_ref):
    v = v_ref[...]
    mu = jnp.mean(v, axis=-1, keepdims=True)
    d = v - mu
    var = jnp.mean(d * d, axis=-1, keepdims=True)
    y = d * lax.rsqrt(var + LN_EPS) * g_ref[...] + b_ref[...]
    of_ref[...] = y
    for ob_ref in maybe_ob_ref:
        ob_ref[...] = y.astype(BF16)


def _layer_norm(v, g, b, *, want_bf16=True, tm=256):
    m, d = v.shape
    tm = min(tm, m)
    assert m % tm == 0
    row = pl.BlockSpec((tm, d), lambda i: (i, 0))
    vec = pl.BlockSpec((1, d), lambda i: (0, 0))
    out_shape = [jax.ShapeDtypeStruct((m, d), F32)] + [jax.ShapeDtypeStruct((m, d), BF16)] * want_bf16
    out = pl.pallas_call(
        _ln_kernel,
        out_shape=tuple(out_shape),
        grid=(m // tm,),
        in_specs=[row, vec, vec],
        out_specs=tuple([row] * len(out_shape)),
        compiler_params=_params("parallel"),
        name="layernorm",
    )(v, g.reshape(1, d), b.reshape(1, d))
    return (out[0], out[1]) if want_bf16 else (out[0], None)


def _moba_kernel(slope_ref, qt_ref, k_ref, vt_ref, o_ref, kmean_ref, bias_ref, sel_ref, s0_ref, s1_ref,
                 *, nblk, blk, n_top, c2, hp):
    hg = pl.program_id(1)
    qi = pl.program_id(2)
    dh = HEAD_DIM
    kc = lax.broadcasted_iota(jnp.int32, (blk, blk), 0)
    qr = lax.broadcasted_iota(jnp.int32, (blk, blk), 1)
    slope2 = [slope_ref[hg * hp + h] * LOG2E for h in range(hp)]
    cols = [slice(h * dh, (h + 1) * dh) for h in range(hp)]

    @pl.when(qi == 0)
    def _():
        kc2 = lax.broadcasted_iota(jnp.int32, (2 * blk, blk), 0)
        qr2 = lax.broadcasted_iota(jnp.int32, (2 * blk, blk), 1)
        for h in range(hp):
            kf = k_ref[:, cols[h]].astype(F32).reshape(nblk, blk, dh)
            kmean_ref[h] = (jnp.sum(kf, axis=1) / blk).astype(BF16)
            bias_ref[h] = slope2[h] * (qr2 - kc2).astype(F32)

    qt = [qt_ref[cols[h], :] for h in range(hp)]
    gate = [jnp.dot(kmean_ref[h], qt[h], preferred_element_type=F32) for h in range(hp)]
    past = lax.broadcasted_iota(jnp.int32, (nblk, blk), 0) < qi
    for h in range(hp):
        sel = _top_k_rows(jnp.where(past, gate[h], NEG_INF), n_top) & past
        sel_ref[h] = sel.astype(F32)

    def scores(h, k0, nk):
        return (jnp.dot(k_ref[pl.ds(k0, nk), cols[h]], qt[h], preferred_element_type=F32) * c2
                - bias_ref[h, 0:nk, :])

    def pv(h, k0, nk, p):
        return _pv_with_sums(vt_ref[cols[h], pl.ds(k0, nk)], p)

    kq = pl.multiple_of(qi * blk, blk)
    s0 = [jnp.where(kc <= qr, scores(h, kq, blk), NEG_INF) for h in range(hp)]
    m0 = [jnp.max(s0[h], axis=0, keepdims=True) for h in range(hp)]
    acc0 = [pv(h, kq, blk, jnp.exp2(s0[h] - m0[h])) for h in range(hp)]
    init = []
    for h in range(hp):
        init += [m0[h], acc0[h]]

    def pair_pos(t):
        j = qi - 1 - 2 * t
        pb = jnp.maximum(j - 1, 0)
        return j, pb, pl.multiple_of(pb * blk, blk)

    def fill(slot_ref, t):
        k0 = pair_pos(t)[2]
        for h in range(hp):
            slot_ref[h] = scores(h, k0, 2 * blk)

    def consume(slot_ref, t, carry):
        j, pb, k0 = pair_pos(t)
        s = [slot_ref[h] for h in range(hp)]
        valid = pb + lax.broadcasted_iota(jnp.int32, (2, 1), 0) <= j
        m_new, scale, p = [], [], []
        for h in range(hp):
            m_i = carry[2 * h]
            off = slope2[h] * ((qi - pb) * blk).astype(F32)
            rows = jnp.concatenate([sel_ref[h, pl.ds(pb, 1), :], sel_ref[h, pl.ds(pb + 1, 1), :]], axis=0)
            picked = (rows > 0.5) & valid
            blk_max = jnp.max(s[h].reshape(2, blk, blk), axis=1)
            m_pair = jnp.max(jnp.where(picked, blk_max, NEG_INF), axis=0, keepdims=True)
            m_h = jnp.maximum(m_i, m_pair - off)
            shift = jnp.where(picked, m_h + off, -NEG_INF)
            shift = jnp.broadcast_to(shift[:, None, :], (2, blk, blk)).reshape(2 * blk, blk)
            m_new.append(m_h)
            scale.append(jnp.exp2(m_i - m_h))
            p.append(jnp.exp2(s[h] - shift))
        upd = [pv(h, k0, 2 * blk, p[h]) for h in range(hp)]
        out = []
        for h in range(hp):
            out += [m_new[h], scale[h] * carry[2 * h + 1] + upd[h]]
        return tuple(out)

    def body(u, carry):
        fill(s1_ref, 2 * u + 1)
        carry = consume(s0_ref, 2 * u, carry)
        fill(s0_ref, 2 * u + 2)
        return consume(s1_ref, 2 * u + 1, carry)

    fill(s0_ref, 0)
    fin = lax.fori_loop(0, ((qi + 1) // 2 + 1) // 2, body, tuple(init))
    for h in range(hp):
        acc = fin[2 * h + 1]
        o_t = acc[0:dh] / jnp.maximum(acc[dh:dh + 1], 1e-30)
        o_ref[:, cols[h]] = o_t.T.astype(o_ref.dtype)


def _moba(zt, zn, slopes, *, batch, seq, qt_row, vt_row, k_col, hp=4):
    blk = MOBA_BLOCK
    assert seq % blk == 0 and MOBA_HEADS % hp == 0
    assert qt_row % hp == 0 and vt_row % hp == 0 and k_col % hp == 0
    nblk = seq // blk
    n_top = min(MOBA_TOPK, nblk - 1)
    assert n_top > 0
    dh = HEAD_DIM
    kern = functools.partial(_moba_kernel, nblk=nblk, blk=blk, n_top=n_top, c2=dh ** -0.5 * LOG2E, hp=hp)
    return pl.pallas_call(
        kern,
        out_shape=jax.ShapeDtypeStruct((batch * seq, MOBA_HEADS * dh), BF16),
        grid=(batch, MOBA_HEADS // hp, nblk),
        in_specs=[pl.BlockSpec(memory_space=pltpu.SMEM),
                  pl.BlockSpec((hp * dh, blk), lambda b, h, i: (qt_row // hp + h, b * nblk + i)),
                  pl.BlockSpec((seq, hp * dh), lambda b, h, i: (b, k_col // hp + h)),
                  pl.BlockSpec((hp * dh, seq), lambda b, h, i: (vt_row // hp + h, b))],
        out_specs=pl.BlockSpec((blk, hp * dh), lambda b, h, i: (b * nblk + i, h)),
        scratch_shapes=[pltpu.VMEM((hp, nblk, dh), BF16),
                        pltpu.VMEM((hp, 2 * blk, blk), F32),
                        pltpu.VMEM((hp, nblk, blk), F32),
                        pltpu.VMEM((hp, 2 * blk, blk), F32),
                        pltpu.VMEM((hp, 2 * blk, blk), F32)],
        compiler_params=_params("parallel", "parallel", "arbitrary"),
        name="moba_attention",
    )(slopes, zt, zn, zt)


def _compress_kernel(x_ref, pos_ref, w1_ref, w2_ref, o_ref, ot_ref, *, nc):
    dh = x_ref.shape[-1]
    half = NSA_CMP_STRIDE
    acc_lo = jnp.zeros((nc, dh), F32)
    acc_hi = jnp.zeros((nc, dh), F32)
    for p in range(half):
        xp = x_ref[pl.ds(p, nc, stride=half), :]
        lo = (xp + pos_ref[0, p:p + 1, :]).astype(BF16)
        hi = (xp + pos_ref[0, half + p:half + p + 1, :]).astype(BF16)
        acc_lo += jnp.dot(lo, w1_ref[0, p * dh:(p + 1) * dh, :], preferred_element_type=F32)
        acc_hi += jnp.dot(hi, w1_ref[0, (half + p) * dh:(half + p + 1) * dh, :], preferred_element_type=F32)
    pre = acc_lo + pltpu.roll(acc_hi, nc - 1, 0)
    hid = jax.nn.gelu(pre)
    out = jnp.dot(hid.astype(BF16), w2_ref[0], preferred_element_type=F32)
    o_ref[0, 0] = out.astype(o_ref.dtype)
    ot_ref[0, 0] = out.T.astype(ot_ref.dtype)


def _nsa_compress(zf, pos, w1, w2, *, batch, seq):
    dh = HEAD_DIM
    g = NSA_KV_GROUPS
    nc = seq // NSA_CMP_STRIDE
    return pl.pallas_call(
        functools.partial(_compress_kernel, nc=nc),
        out_shape=(jax.ShapeDtypeStruct((batch, 2 * g, nc, dh), BF16),
                   jax.ShapeDtypeStruct((batch, 2 * g, dh, nc), BF16)),
        grid=(batch, 2 * g),
        in_specs=[pl.BlockSpec((seq, dh), lambda b, c: (b, c)),
                  pl.BlockSpec((1, NSA_CMP_LEN, dh), lambda b, c: (c // g, 0, 0)),
                  pl.BlockSpec((1, NSA_CMP_LEN * dh, dh), lambda b, c: (c // g, 0, 0)),
                  pl.BlockSpec((1, dh, dh), lambda b, c: (c // g, 0, 0))],
        out_specs=(pl.BlockSpec((1, 1, nc, dh), lambda b, c: (b, c, 0, 0)),
                   pl.BlockSpec((1, 1, dh, nc), lambda b, c: (b, c, 0, 0))),
        compiler_params=_params("parallel", "arbitrary"),
        name="nsa_compress",
    )(zf, pos, w1, w2)


def _nsa_kernel(slope_ref, qt_ref, kc_ref, vct_ref, ks_ref, vst_ref, kw_ref, vwt_ref, gate_ref, o_ref,
                sel_ref, bias_ref, member_ref, ecmp_ref, flag_ref, *, tq, tk, nc, ns, k_top, c2):
    g = pl.program_id(1)
    qi = pl.program_id(2)
    dh = HEAD_DIM
    nr = NSA_GROUP_SIZE
    nl = nr * tq
    bpt = tk // NSA_SEL_LEN
    q0 = qi * tq

    lane = lax.broadcasted_iota(jnp.int32, (1, nl), 1)
    tl = q0 + lane % tq
    head = lane // tq
    slope2 = jnp.zeros((1, nl), F32)
    for r in range(nr):
        slope2 = jnp.where(head == r, slope_ref[g * nr + r] * LOG2E, slope2)

    @pl.when(qi == 0)
    def _():
        kc_i = lax.broadcasted_iota(jnp.int32, (2 * tk, nl), 0)
        bias_ref[...] = slope2 * (lane % tq - kc_i).astype(F32)
        sj = lax.broadcasted_iota(jnp.int32, (ns, nc), 0) * NSA_SEL_LEN
        ci = lax.broadcasted_iota(jnp.int32, (ns, nc), 1) * NSA_CMP_STRIDE
        member_ref[...] = ((ci < sj + NSA_SEL_LEN) & (ci + NSA_CMP_LEN > sj)).astype(BF16)
        ecmp_ref[...] = slope2 * (lax.broadcasted_iota(jnp.int32, (nc, nl), 0) * NSA_CMP_STRIDE
                                  + (NSA_CMP_LEN - 1)).astype(F32)

    qb = qt_ref[...]
    qt = jnp.concatenate([qb[r * dh:(r + 1) * dh, :] for r in range(nr)], axis=1)

    span = NSA_WINDOW + tq
    w0 = pl.multiple_of(jnp.maximum(q0 - NSA_WINDOW, 0), tq)
    raw_cmp = jnp.dot(kc_ref[0, 0], qt, preferred_element_type=F32)
    raw_win = jnp.dot(kw_ref[pl.ds(w0, span), :], qt, preferred_element_type=F32)

    cend = lax.broadcasted_iota(jnp.int32, (nc, 1), 0) * NSA_CMP_STRIDE + (NSA_CMP_LEN - 1)
    s = (raw_cmp * c2 + ecmp_ref[...]) - slope2 * tl.astype(F32)
    vis = cend <= tl
    s = jnp.where(vis, s, NEG_INF)
    e = jnp.where(vis, jnp.exp2(s - jnp.max(s, axis=0, keepdims=True)), 0.0)
    p_cmp = e * (1.0 / jnp.maximum(jnp.sum(e, axis=0, keepdims=True), 1e-30))
    o_cmp = jnp.dot(vct_ref[0, 0], p_cmp.astype(BF16), preferred_element_type=F32)

    p_sum = p_cmp[:, 0:tq]
    for r in range(1, nr):
        p_sum = p_sum + p_cmp[:, r * tq:(r + 1) * tq]
    member = member_ref[...]
    imp = sum(jnp.dot(member, part, preferred_element_type=F32) for part in _split3(p_sum))

    rel = (w0 + lax.broadcasted_iota(jnp.int32, (span, 1), 0)) - tl
    s = (raw_win * c2 - bias_ref[0:span, :]) + slope2 * (w0 - q0).astype(F32)
    s = jnp.where((rel <= 0) & (rel > -NSA_WINDOW), s, NEG_INF)
    acc = _pv_with_sums(vwt_ref[:, pl.ds(w0, span)], jnp.exp2(s - jnp.max(s, axis=0, keepdims=True)))
    o_win = acc[0:dh] / jnp.maximum(acc[dh:dh + 1], 1e-30)

    own = (q0 + lax.broadcasted_iota(jnp.int32, (1, tq), 1)) // NSA_SEL_LEN
    sid = lax.broadcasted_iota(jnp.int32, (ns, tq), 0)
    forced = (sid == 0) | (sid == own) | (sid == own - 1)
    imp = jnp.where(forced, imp + SEL_FORCE, jnp.where(sid > own, -1.0, imp))
    sel_ref[...] = _top_k_rows(imp, k_top).astype(F32)

    def sel_rows(b0, nb):
        rows = sel_ref[pl.ds(pl.multiple_of(b0, bpt), nb), :]
        return jnp.concatenate([rows] * nr, axis=1)

    def scores(k0, nk):
        return jnp.dot(ks_ref[pl.ds(k0, nk), :], qt, preferred_element_type=F32) * c2 - bias_ref[0:nk, :]

    def blockwise(x):
        nb = x.shape[0]
        return jnp.broadcast_to(x[:, None, :], (nb, NSA_SEL_LEN, nl)).reshape(nb * NSA_SEL_LEN, nl)

    jd = q0 // tk
    kd = pl.multiple_of(jd * tk, tk)
    kpos = kd + lax.broadcasted_iota(jnp.int32, (tk, 1), 0)
    off = slope2 * (q0 - kd).astype(F32)
    ok = (blockwise(sel_rows(jd * bpt, bpt)) > 0.5) & (kpos <= tl)
    s = jnp.where(ok, scores(kd, tk), NEG_INF)
    m0 = jnp.max(s, axis=0, keepdims=True) - off
    acc0 = _pv_with_sums(vst_ref[:, pl.ds(kd, tk)], jnp.exp2(s - (m0 + off)))

    def sweep(i, live, carry):
        m_i, acc = carry
        k0 = pl.multiple_of(i * (2 * tk), 2 * tk)
        s = scores(k0, 2 * tk)
        off = slope2 * (q0 - k0).astype(F32)
        rows = jnp.concatenate([live] * nr, axis=1) > 0.5
        blk_max = jnp.max(s.reshape(2 * bpt, NSA_SEL_LEN, nl), axis=1)
        m_tile = jnp.max(jnp.where(rows, blk_max, NEG_INF), axis=0, keepdims=True)
        m_new = jnp.maximum(m_i, m_tile - off)
        shift = jnp.where(rows, m_new + off, -NEG_INF)
        upd = _pv_with_sums(vst_ref[:, pl.ds(k0, 2 * tk)], jnp.exp2(s - blockwise(shift)))
        return m_new, jnp.exp2(m_i - m_new) * acc + upd

    vis_sel = jnp.where(sid < jd * bpt, sel_ref[...], 0.0)
    pair_any = jnp.max(jnp.max(vis_sel.reshape(ns // (2 * bpt), 2 * bpt, tq), axis=1), axis=1, keepdims=True)
    for i in range(ns // (2 * bpt)):
        flag_ref[i] = (pair_any[i, 0] > 0.5).astype(jnp.int32)

    def body(i, carry):
        b0 = pl.multiple_of(i * (2 * bpt), 2 * bpt)
        bid = b0 + lax.broadcasted_iota(jnp.int32, (2 * bpt, 1), 0)
        live = jnp.where(bid < jd * bpt, sel_ref[pl.ds(b0, 2 * bpt), :], 0.0)
        return lax.cond(flag_ref[i] > 0, lambda c: sweep(i, live, c), lambda c: c, carry)

    _, acc = lax.fori_loop(0, (jd + 1) // 2, body, (m0, acc0))
    o_slc = acc[0:dh] / jnp.maximum(acc[dh:dh + 1], 1e-30)

    gt = gate_ref[0]
    for r in range(nr):
        sl = slice(r * tq, (r + 1) * tq)
        o = (gt[3 * r:3 * r + 1] * o_cmp[:, sl] + gt[3 * r + 1:3 * r + 2] * o_slc[:, sl]
             + gt[3 * r + 2:3 * r + 3] * o_win[:, sl])
        o_ref[:, r * dh:(r + 1) * dh] = o.T.astype(o_ref.dtype)


def _nsa(zt, zn, cmp_n, cmp_t, gates_t, slopes, *, batch, seq, qt_row, vst_row, vwt_row, ks_col, kw_col,
         tq=256, tk=512):
    dh = HEAD_DIM
    g = NSA_KV_GROUPS
    nr = NSA_GROUP_SIZE
    tk = min(tk, seq)
    assert seq % (2 * tk) == 0 and tk % tq == 0 and NSA_WINDOW % tq == 0
    assert NSA_WINDOW + tq <= min(seq, 2 * tk)
    nq = seq // tq
    nc = seq // NSA_CMP_STRIDE
    ns = seq // NSA_SEL_LEN
    k_top = min(NSA_N_SEL, ns)
    kern = functools.partial(_nsa_kernel, tq=tq, tk=tk, nc=nc, ns=ns, k_top=k_top, c2=dh ** -0.5 * LOG2E)
    key_blk = lambda col: pl.BlockSpec((seq, dh), lambda b, gg, i: (b, col + gg))
    val_blk = lambda row: pl.BlockSpec((dh, seq), lambda b, gg, i: (row + gg, b))
    return pl.pallas_call(
        kern,
        out_shape=jax.ShapeDtypeStruct((batch * seq, NSA_HEADS * dh), BF16),
        grid=(batch, g, nq),
        in_specs=[pl.BlockSpec(memory_space=pltpu.SMEM),
                  pl.BlockSpec((nr * dh, tq), lambda b, gg, i: (qt_row // nr + gg, b * nq + i)),
                  pl.BlockSpec((1, 1, nc, dh), lambda b, gg, i: (b, gg, 0, 0)),
                  pl.BlockSpec((1, 1, dh, nc), lambda b, gg, i: (b, g + gg, 0, 0)),
                  key_blk(ks_col), val_blk(vst_row), key_blk(kw_col), val_blk(vwt_row),
                  pl.BlockSpec((1, gates_t.shape[1], tq), lambda b, gg, i: (gg, 0, b * nq + i))],
        out_specs=pl.BlockSpec((tq, nr * dh), lambda b, gg, i: (b * nq + i, gg)),
        scratch_shapes=[pltpu.VMEM((ns, tq), F32),
                        pltpu.VMEM((2 * tk, nr * tq), F32),
                        pltpu.VMEM((ns, nc), BF16),
                        pltpu.VMEM((nc, nr * tq), F32),
                        pltpu.SMEM((ns // (2 * tk // NSA_SEL_LEN),), jnp.int32)],
        compiler_params=_params("parallel", "parallel", "arbitrary"),
        name="nsa_attention",
    )(slopes, zt, cmp_n, cmp_t, zn, zt, zn, zt, gates_t)


def _conv_kernel(gb_ref, gc_ref, h_ref, gcp_ref, hp_ref, w_ref, o_ref, *, tiles_per_seq):
    i = pl.program_id(0)
    u = gc_ref[...] * h_ref[...]
    prev = gcp_ref[...] * hp_ref[...]
    prev = jnp.where(i % tiles_per_seq == 0, 0.0, prev)
    r = lax.broadcasted_iota(jnp.int32, u.shape, 0)
    u1 = jnp.where(r == 0, prev[7:8], pltpu.roll(u, 1, 0))
    u2 = jnp.where(r == 0, prev[6:7], jnp.where(r == 1, prev[7:8], pltpu.roll(u, 2, 0)))
    w = w_ref[...]
    y = gb_ref[...] * (w[0:1] * u2 + w[1:2] * u1 + w[2:3] * u)
    o_ref[...] = y.astype(o_ref.dtype)


def _short_conv(z, w, *, seq, gb_col, gc_col, h_col, tt=512, tc=512):
    m = z.shape[0]
    c = w.shape[1]
    tt = min(tt, seq)
    assert seq % tt == 0 and c % tc == 0 and tt % 8 == 0
    sub = tt // 8
    cur = lambda col: pl.BlockSpec((tt, tc), lambda i, j: (i, col + j))
    prv = lambda col: pl.BlockSpec((8, tc), lambda i, j: (jnp.maximum(i * sub - 1, 0), col + j))
    return pl.pallas_call(
        functools.partial(_conv_kernel, tiles_per_seq=seq // tt),
        out_shape=jax.ShapeDtypeStruct((m, c), BF16),
        grid=(m // tt, c // tc),
        in_specs=[cur(gb_col), cur(gc_col), cur(h_col), prv(gc_col), prv(h_col),
                  pl.BlockSpec((3, tc), lambda i, j: (0, j))],
        out_specs=pl.BlockSpec((tt, tc), lambda i, j: (i, j)),
        compiler_params=_params("parallel", "arbitrary"),
        name="short_conv",
    )(z, z, z, z, z, w)


def _gate_decay_kernel(x_ref, w1_ref, w2_ref, b_ref, o_ref):
    za = jnp.dot(x_ref[...], w1_ref[...], preferred_element_type=F32)
    pre = jnp.dot(za.astype(BF16), w2_ref[...], preferred_element_type=F32) + b_ref[...]
    ls = -(jnp.maximum(-pre, 0.0) + jnp.log1p(jnp.exp(-jnp.abs(pre))))
    o_ref[...] = ls / GLA_GATE_TAU


def _gate_decay(x, w1, w2, b, *, tm=1024):
    m, k = x.shape
    r = w1.shape[1]
    n = w2.shape[1]
    tm = min(tm, m)
    return pl.pallas_call(
        _gate_decay_kernel,
        out_shape=jax.ShapeDtypeStruct((m, n), F32),
        grid=(m // tm,),
        in_specs=[pl.BlockSpec((tm, k), lambda i: (i, 0)),
                  pl.BlockSpec((k, r), lambda i: (0, 0)),
                  pl.BlockSpec((r, n), lambda i: (0, 0)),
                  pl.BlockSpec((1, n), lambda i: (0, 0))],
        out_specs=pl.BlockSpec((tm, n), lambda i: (i, 0)),
        compiler_params=_params("parallel"),
        name="gla_gate_decay",
    )(x, w1, w2, b.reshape(1, n))


def _gla_kernel(q_ref, k_ref, v_ref, g_ref, la_ref, ng_ref, o_ref, st_ref, *, tc):
    L = GLA_CHUNK
    dk, dv = GLA_DK, GLA_DV
    hp = q_ref.shape[-1] // dk

    @pl.when(pl.program_id(2) == 0)
    def _():
        st_ref[...] = jnp.zeros_like(st_ref)

    nch = tc // L
    ri = lax.broadcasted_iota(jnp.int32, (tc, tc), 0)
    ci = lax.broadcasted_iota(jnp.int32, (tc, tc), 1)
    causal = (ci <= ri) & (ri // L == ci // L)
    tri = causal.astype(BF16)
    ng = ng_ref[...]

    b = sum(jnp.dot(tri, part, preferred_element_type=F32) for part in _split3(la_ref[...]))
    b3 = b.reshape(nch, L, hp * dk)
    b_last = b3[:, L - 1:L, :]
    q_t = ((q_ref[...] * dk ** -0.5) * jnp.exp(b)).astype(BF16)
    k_raw = k_ref[...]
    k_t = (k_raw * jnp.exp(-b)).astype(BF16)
    k_d = (k_raw.reshape(nch, L, hp * dk) * jnp.exp(b_last - b3)).astype(BF16)
    dec = jnp.exp(b_last)
    v = v_ref[...].astype(BF16)

    ks = [slice(h * dk, (h + 1) * dk) for h in range(hp)]
    vs = [slice(h * dv, (h + 1) * dv) for h in range(hp)]
    att = [lax.dot_general(q_t[:, ks[h]], k_t[:, ks[h]], NT_DIMS, preferred_element_type=F32) for h in range(hp)]
    u_t = [[lax.dot_general(v[c * L:(c + 1) * L, vs[h]], k_d[c][:, ks[h]], TN_DIMS, preferred_element_type=F32)
            for c in range(nch)] for h in range(hp)]
    o = [jnp.dot(jnp.where(causal, att[h], 0.0).astype(BF16), v[:, vs[h]], preferred_element_type=F32)
         for h in range(hp)]

    states = []
    for h in range(hp):
        st = st_ref[h]
        per_chunk = []
        for c in range(nch):
            per_chunk.append(st.astype(BF16))
            st = st * dec[c][:, ks[h]] + u_t[h][c]
        st_ref[h] = st
        states.append(per_chunk)
    inter = [[lax.dot_general(q_t[c * L:(c + 1) * L, ks[h]], states[h][c], NT_DIMS, preferred_element_type=F32)
              for c in range(nch)] for h in range(hp)]
    for h in range(hp):
        oh = o[h] + jnp.concatenate(inter[h], axis=0)
        oh = oh * lax.rsqrt(jnp.mean(oh * oh, axis=-1, keepdims=True) + NORM_EPS) * ng
        o_ref[:, vs[h]] = (oh * jax.nn.silu(g_ref[:, vs[h]])).astype(o_ref.dtype)


def _gla(z, la, norm_g, *, batch, seq, q_col, k_col, v_col, g_col, tc=256, hp=2):
    dk, dv = GLA_DK, GLA_DV
    tc = min(tc, seq)
    assert seq % tc == 0 and tc % GLA_CHUNK == 0 and GLA_HEADS % hp == 0
    nt = seq // tc
    qk = lambda col: pl.BlockSpec((tc, hp * dk), lambda b, h, i: (b * nt + i, col + h))
    vg = lambda col: pl.BlockSpec((tc, hp * dv), lambda b, h, i: (b * nt + i, col + h))
    return pl.pallas_call(
        functools.partial(_gla_kernel, tc=tc),
        out_shape=jax.ShapeDtypeStruct((batch * seq, GLA_HEADS * dv), BF16),
        grid=(batch, GLA_HEADS // hp, nt),
        in_specs=[qk(q_col), qk(k_col), vg(v_col), vg(g_col), qk(0),
                  pl.BlockSpec((1, dv), lambda b, h, i: (0, 0))],
        out_specs=pl.BlockSpec((tc, hp * dv), lambda b, h, i: (b * nt + i, h)),
        scratch_shapes=[pltpu.VMEM((hp, dv, dk), F32)],
        compiler_params=_params("parallel", "parallel", "arbitrary"),
        name="gla",
    )(z, z, z, z, la, norm_g.reshape(1, dv))


def _pad_cols(w, n):
    return jnp.pad(w, ((0, 0), (0, n - w.shape[1])))


def _ffn_sublayer(xf, xb, wg, wu, wd_b, layer, ln_g, ln_b, want_bf16=True):
    gu = _ffn_gu(xb, wg, wu, layer)
    v = _matmul(gu, wd_b, F32, layer=layer, res=xf, res_coef=0.5, tm=512, tn=512)
    return _layer_norm(v, ln_g, ln_b, want_bf16=want_bf16)


def _alibi_slopes(n):
    return jnp.exp2(-8.0 * jnp.arange(1, n + 1, dtype=F32) / n)


def _attn_sublayer(xf, xb, w_in, w_out, pos_k, w1_k, w2_k, pos_v, w1_v, w2_v, ln_g, ln_b, *, batch, seq):
    dh = HEAD_DIM
    nq = NSA_HEADS * dh
    kv = NSA_KV_GROUPS * dh
    nm = MOBA_HEADS * dh
    n_gate = NSA_HEADS * 3
    c_kc, c_ks, c_vs, c_kw, c_vw, c_gate = (nq + i * kv for i in (0, 2, 3, 4, 5, 6))
    c_mq = c_gate + n_gate
    c_mk, c_mv = c_mq + nm, c_mq + 2 * nm
    col = lambda a, n: w_in[:, a:a + n]
    w_n = jnp.concatenate([col(c_ks, kv), col(c_kw, kv), col(c_mk, nm)], axis=1).astype(BF16)
    w_t = jnp.concatenate([col(0, nq), col(c_vs, kv), col(c_vw, kv), col(c_mq, nm), col(c_mv, nm)],
                          axis=1).T.astype(BF16)
    w_f = col(c_kc, 2 * kv).astype(BF16)
    w_g = _pad_cols(col(c_gate, n_gate), LANE).T.astype(BF16)
    zn = _matmul(xb, w_n, BF16)
    zt = _matmul_nt(w_t, xb, BF16)
    zf = _matmul(xb, w_f, F32)
    gates_t = _matmul_nt(w_g, xb, F32, act="sigmoid")[:n_gate]
    gates_t = jnp.pad(gates_t.reshape(NSA_KV_GROUPS, 3 * NSA_GROUP_SIZE, batch * seq), ((0, 0), (0, 4), (0, 0)))

    slopes = _alibi_slopes(N_ATTN_HEADS)
    cmp_n, cmp_t = _nsa_compress(zf, jnp.stack([pos_k, pos_v]), jnp.stack([w1_k, w1_v]).astype(BF16),
                                 jnp.stack([w2_k, w2_v]).astype(BF16), batch=batch, seq=seq)
    u = nq // dh
    o_nsa = _nsa(zt, zn, cmp_n, cmp_t, gates_t, slopes[0::2], batch=batch, seq=seq,
                 qt_row=0, vst_row=u, vwt_row=u + 4, ks_col=0, kw_col=4)
    o_moba = _moba(zt, zn, slopes[1::2], batch=batch, seq=seq, qt_row=u + 8, vt_row=u + 24, k_col=8)
    w_o = w_out.astype(BF16)
    v = _matmul2(o_nsa, o_moba, w_o[:nq], w_o[nq:], xf, 1.0)
    return _layer_norm(v, ln_g, ln_b)


def _mix_sublayer(xf, xb, w_in, w_out, conv_w, w_a2, b_a, norm_g, ln_g, ln_b, *, batch, seq):
    cc = CONV_CHANNELS
    hk = GLA_HEADS * GLA_DK
    hv = GLA_HEADS * GLA_DV
    c_za = 3 * cc + 2 * hk + 2 * hv
    z = _matmul(xb, w_in[:, :c_za].astype(BF16), F32)
    w_za = _pad_cols(w_in[:, c_za:], LANE).astype(BF16)
    w_a2p = jnp.pad(w_a2, ((0, LANE - w_a2.shape[0]), (0, 0))).astype(BF16)
    la = _gate_decay(xb, w_za, w_a2p, b_a)
    y_conv = _short_conv(z, conv_w, seq=seq, gb_col=0, gc_col=cc // 512, h_col=2 * cc // 512)
    hp = 4
    y_gla = _gla(z, la, norm_g, batch=batch, seq=seq, q_col=3 * cc // (hp * GLA_DK),
                 k_col=(3 * cc + hk) // (hp * GLA_DK), v_col=(3 * cc + 2 * hk) // (hp * GLA_DV),
                 g_col=(3 * cc + 2 * hk + hv) // (hp * GLA_DV), hp=hp)
    w_o = w_out.astype(BF16)
    v = _matmul2(y_conv, y_gla, w_o[:cc], w_o[cc:], xf, 1.0)
    return _layer_norm(v, ln_g, ln_b)


def kernel(x, ln_g, ln_b, ffn_pre_wg, ffn_pre_wu, ffn_pre_wd, ffn_post_wg, ffn_post_wu, ffn_post_wd,
           att_w_in, att_w_out, nsa_pos_k, nsa_w1_k, nsa_w2_k, nsa_pos_v, nsa_w1_v, nsa_w2_v,
           mix_w_in, mix_w_out, conv_w, gla_w_a2, gla_b_a, gla_norm_g):
    batch, seq, d = x.shape
    xf = x.reshape(batch * seq, d)
    xb = xf.astype(BF16)
    pre_wd, post_wd = ffn_pre_wd.astype(BF16), ffn_post_wd.astype(BF16)
    for layer in range(DEPTH):
        xf, xb = _ffn_sublayer(xf, xb, ffn_pre_wg, ffn_pre_wu, pre_wd, layer, ln_g[layer, 0], ln_b[layer, 0])
        i = layer // 2
        if layer % 2 == 0:
            xf, xb = _attn_sublayer(xf, xb, att_w_in[i], att_w_out[i], nsa_pos_k[i], nsa_w1_k[i], nsa_w2_k[i],
                                    nsa_pos_v[i], nsa_w1_v[i], nsa_w2_v[i], ln_g[layer, 1], ln_b[layer, 1],
                                    batch=batch, seq=seq)
        else:
            xf, xb = _mix_sublayer(xf, xb, mix_w_in[i], mix_w_out[i], conv_w[i], gla_w_a2[i], gla_b_a[i],
                                   gla_norm_g[i], ln_g[layer, 1], ln_b[layer, 1], batch=batch, seq=seq)
        xf, xb = _ffn_sublayer(xf, xb, ffn_post_wg, ffn_post_wu, post_wd, layer, ln_g[layer, 2], ln_b[layer, 2],
                               want_bf16=layer + 1 < DEPTH)
    return xf.reshape(batch, seq, d)
```
